```python
import math
import jax, jax.numpy as jnp
from jax import lax
import numpy as np

D_MODEL = 1024
BATCH = 4
SEQ = 4096
DEPTH = 1

CHUNK = 64
MIX_WIDTH = D_MODEL
HG_WIDTH = MIX_WIDTH // 2
CONV_WIDTH = MIX_WIDTH - HG_WIDTH
HG_EXPAND = 128
HG_HEADS = HG_WIDTH // HG_EXPAND
HG_DK = HG_EXPAND
HG_DV = HG_WIDTH // HG_HEADS
CONV_K = 3
N_MEM = 256
XA_HEADS = 4
XA_HEAD_DIM = D_MODEL // XA_HEADS
D_FF = ((8 * D_MODEL // 3 + 255) // 256) * 256
IN_COLS = 4 * HG_WIDTH + 3 * CONV_WIDTH
EPS = 1e-6

kernel_name = "hymba_hgrn2_shortconv_macaron_block"


def rmsnorm(x, g):
    xf = x.astype(jnp.float32)
    y = xf * lax.rsqrt(jnp.mean(xf * xf, axis=-1, keepdims=True) + EPS)
    return (y * g.astype(jnp.float32)).astype(x.dtype)


def swiglu(h, w_gu, w_down):
    gate, up = jnp.split(h @ w_gu, 2, axis=-1)
    return (jax.nn.silu(gate) * up) @ w_down


def hgrn2_chunkwise(q, log_f, k, v):
    B, T, H, DK = q.shape
    DV = v.shape[-1]
    n = T // CHUNK

    def to_chunks(a):
        return a.astype(jnp.float32).reshape(B, n, CHUNK, H, a.shape[-1]).transpose(1, 0, 3, 2, 4)

    qc, lfc, kc, vc = to_chunks(q), to_chunks(log_f), to_chunks(k), to_chunks(v)
    causal = jnp.tril(jnp.ones((CHUNK, CHUNK), dtype=bool))

    def step(S, inp):
        qb, lfb, kb, vb = inp
        b = jnp.cumsum(lfb, axis=2)
        diff = b[:, :, :, None, :] - b[:, :, None, :, :]
        decay = jnp.exp(jnp.where(causal[None, None, :, :, None], diff, -jnp.inf))
        attn = jnp.einsum('bhtk,bhtsk,bhsk->bhts', qb, decay, kb)
        o = attn @ vb + jnp.einsum('bhtk,bhkv->bhtv', qb * jnp.exp(b), S)
        b_last = b[:, :, -1:, :]
        S_new = jnp.exp(b_last[:, :, 0, :])[..., None] * S + jnp.einsum(
            'bhsk,bhsv->bhkv', kb * jnp.exp(b_last - b), vb)
        return S_new, o

    S0 = jnp.zeros((B, H, DK, DV), jnp.float32)
    _, o = lax.scan(step, S0, (qc, lfc, kc, vc))
    return o.transpose(1, 0, 3, 2, 4).reshape(B, T, H, DV)


def causal_dwconv(u, w):
    C = u.shape[-1]
    return lax.conv_general_dilated(
        u, w.astype(u.dtype)[:, None, :], window_strides=(1,), padding=[(CONV_K - 1, 0)],
        dimension_numbers=('NWC', 'WIO', 'NWC'), feature_group_count=C)


def token_mixing(h, w_in, lb, hg_norm, conv_w, conv_norm, w_out):
    B, T, _ = h.shape
    p = h @ w_in
    q, fz, i, g, gb, gc, u = jnp.split(
        p, np.cumsum([HG_WIDTH] * 4 + [CONV_WIDTH] * 2).tolist(), axis=-1)
    lbf = lb.astype(jnp.float32)
    f = lbf + (1.0 - lbf) * jax.nn.sigmoid(fz.astype(jnp.float32))
    log_f = jnp.log(f)
    k = 1.0 - f
    shp = (B, T, HG_HEADS, HG_DK)
    o_hg = hgrn2_chunkwise(jax.nn.silu(q).reshape(shp), log_f.reshape(shp),
                           k.reshape(shp), i.reshape(B, T, HG_HEADS, HG_DV))
    o_hg = rmsnorm(o_hg.astype(h.dtype), hg_norm).reshape(B, T, HG_WIDTH) * jax.nn.silu(g)
    o_cv = rmsnorm(gb * causal_dwconv(gc * u, conv_w), conv_norm)
    return jnp.concatenate([o_hg, o_cv], axis=-1) @ w_out


def cross_attention(h, m, wq, wkv, wo):
    B, T, _ = h.shape
    M = m.shape[1]
    q = (h @ wq).reshape(B, T, XA_HEADS, XA_HEAD_DIM)
    k, v = jnp.split(m @ wkv, 2, axis=-1)
    k = k.reshape(B, M, XA_HEADS, XA_HEAD_DIM)
    v = v.reshape(B, M, XA_HEADS, XA_HEAD_DIM)
    s = jnp.einsum('bthd,bmhd->bhtm', q, k).astype(jnp.float32) * (1.0 / math.sqrt(XA_HEAD_DIM))
    pr = jax.nn.softmax(s, axis=-1).astype(v.dtype)
    o = jnp.einsum('bhtm,bmhd->bthd', pr, v).reshape(B, T, D_MODEL)
    return o @ wo


def setup_inputs(seed: int = 0) -> dict:
    key = jax.random.key(seed)
    ks = iter(jax.random.split(key, 32))

    def nrm(shape, scale):
        return jax.random.normal(next(ks), shape, jnp.float32) * scale

    def gain(shape):
        return 1.0 + nrm(shape, 0.02)

    L = DEPTH
    return {
        "x": nrm((BATCH, SEQ, D_MODEL), 1.0),
        "mem": nrm((BATCH, N_MEM, D_MODEL), 1.0),
        "ffn1_norm": gain((L, D_MODEL)),
        "ffn1_w_gu": nrm((L, D_MODEL, 2 * D_FF), D_MODEL ** -0.5),
        "ffn1_w_down": nrm((L, D_FF, D_MODEL), D_FF ** -0.5),
        "mix_norm": gain((L, D_MODEL)),
        "w_in": nrm((L, D_MODEL, IN_COLS), D_MODEL ** -0.5),
        "lb_logits": nrm((DEPTH + 1, HG_WIDTH), 0.5),
        "hg_norm": gain((L, HG_DV)),
        "conv_w": nrm((L, CONV_K, CONV_WIDTH), CONV_K ** -0.5),
        "conv_norm": gain((L, CONV_WIDTH)),
        "w_out": nrm((L, MIX_WIDTH, D_MODEL), MIX_WIDTH ** -0.5),
        "xa_norm": gain((L, D_MODEL)),
        "mem_norm": gain((L, D_MODEL)),
        "xa_wq": nrm((L, D_MODEL, D_MODEL), D_MODEL ** -0.5),
        "xa_wkv": nrm((L, D_MODEL, 2 * D_MODEL), D_MODEL ** -0.5),
        "xa_wo": nrm((L, D_MODEL, D_MODEL), D_MODEL ** -0.5),
        "ffn2_norm": gain((L, D_MODEL)),
        "ffn2_w_gu": nrm((L, D_MODEL, 2 * D_FF), D_MODEL ** -0.5),
        "ffn2_w_down": nrm((L, D_FF, D_MODEL), D_FF ** -0.5),
        "final_norm": gain((D_MODEL,)),
    }


def reference(x, mem, ffn1_norm, ffn1_w_gu, ffn1_w_down, mix_norm, w_in, lb_logits,
              hg_norm, conv_w, conv_norm, w_out, xa_norm, mem_norm, xa_wq, xa_wkv, xa_wo,
              ffn2_norm, ffn2_w_gu, ffn2_w_down, final_norm):
    lower_bounds = jnp.cumsum(jax.nn.softmax(lb_logits.astype(jnp.float32), axis=0), axis=0)
    for l in range(DEPTH):
        x = x + 0.5 * swiglu(rmsnorm(x, ffn1_norm[l]), ffn1_w_gu[l], ffn1_w_down[l])
        x = x + token_mixing(rmsnorm(x, mix_norm[l]), w_in[l], lower_bounds[l],
                             hg_norm[l], conv_w[l], conv_norm[l], w_out[l])
        x = x + cross_attention(rmsnorm(x, xa_norm[l]), rmsnorm(mem, mem_norm[l]),
                                xa_wq[l], xa_wkv[l], xa_wo[l])
        x = x + 0.5 * swiglu(rmsnorm(x, ffn2_norm[l]), ffn2_w_gu[l], ffn2_w_down[l])
    return rmsnorm(x, final_norm)
```

```python
import functools
import math

import jax
import jax.numpy as jnp
from jax import lax
from jax.experimental import pallas as pl
from jax.experimental.pallas import tpu as pltpu

D_MODEL = 1024
CHUNK = 64
SUB = 8
HG_WIDTH = 512
CONV_WIDTH = 512
HG_HEADS = 4
HG_DK = 128
CONV_K = 3
N_MEM = 256
XA_HEADS = 4
XA_HEAD_DIM = 256
D_FF = 2816
IN_COLS = 4 * HG_WIDTH + 3 * CONV_WIDTH
EPS = 1e-6

V7X_VMEM_BYTES = 64 * 1024 * 1024
MIB = 1024 * 1024

FFN_ROWS = 512
FFN_COLS = 256
MIX_ROWS = 512
XA_ROWS = 512
KV_ROWS = 256

BF16 = jnp.bfloat16
F32 = jnp.float32


def _vmem_limit(estimate_bytes):
    return int(min(estimate_bytes * 5 // 4 + 8 * MIB, V7X_VMEM_BYTES - 6 * MIB))


def _resident(shape):
    nd = len(shape)
    return pl.BlockSpec(shape, lambda *_: (0,) * nd, pipeline_mode=pl.Buffered(1))


def _rmsnorm(x, g):
    ms = jnp.mean(x * x, axis=-1, keepdims=True)
    return x * lax.rsqrt(ms + EPS) * g


def _silu(x):
    return x * (1.0 / (1.0 + jnp.exp(-x)))


def _dot(a, b):
    return jnp.dot(a, b, preferred_element_type=F32)


def _dot_nt(a, b):
    return lax.dot_general(a, b, (((1,), (1,)), ((), ())), preferred_element_type=F32)


def _dot_tn(a, b):
    return lax.dot_general(a, b, (((0,), (0,)), ((), ())), preferred_element_type=F32)


def _ffn_kernel(x_ref, g_ref, wgu_ref, wd_ref, fin_ref, o_ref, act_ref, *, final_norm):
    x = x_ref[...]
    h = _rmsnorm(x, g_ref[...]).astype(BF16)
    for c in range(D_FF // FFN_COLS):
        lo = c * FFN_COLS
        gate = _dot(h, wgu_ref[:, lo:lo + FFN_COLS])
        up = _dot(h, wgu_ref[:, D_FF + lo:D_FF + lo + FFN_COLS])
        act_ref[:, lo:lo + FFN_COLS] = (_silu(gate) * up).astype(BF16)
    y = x + 0.5 * _dot(act_ref[...], wd_ref[...])
    if final_norm:
        y = _rmsnorm(y, fin_ref[...])
    o_ref[...] = y


def _ffn(x2d, norm, w_gu, w_down, fin, *, final_norm):
    n = x2d.shape[0]
    tm = FFN_ROWS
    est = (4 * tm * D_MODEL * 4 + w_gu.size * 2 + w_down.size * 2 + tm * D_FF * 2
           + 4 * tm * FFN_COLS * 4)
    return pl.pallas_call(
        functools.partial(_ffn_kernel, final_norm=final_norm),
        grid=(n // tm,),
        in_specs=[
            pl.BlockSpec((tm, D_MODEL), lambda i: (i, 0)),
            _resident((1, D_MODEL)),
            _resident((D_MODEL, 2 * D_FF)),
            _resident((D_FF, D_MODEL)),
            _resident((1, D_MODEL)),
        ],
        out_specs=pl.BlockSpec((tm, D_MODEL), lambda i: (i, 0)),
        out_shape=jax.ShapeDtypeStruct((n, D_MODEL), F32),
        scratch_shapes=[pltpu.VMEM((tm, D_FF), BF16)],
        compiler_params=pltpu.CompilerParams(
            dimension_semantics=("arbitrary",), vmem_limit_bytes=_vmem_limit(est)),
        name="ffn_final" if final_norm else "ffn",
    )(x2d, norm.reshape(1, D_MODEL), w_gu, w_down, fin.reshape(1, D_MODEL))


def _split3(x):
    hi = x.astype(BF16)
    r = x - hi.astype(F32)
    mid = r.astype(BF16)
    lo = (r - mid.astype(F32)).astype(BF16)
    return hi, mid, lo


def _hgrn2_masks():
    t = lax.broadcasted_iota(jnp.int32, (CHUNK, CHUNK), 0)
    s = lax.broadcasted_iota(jnp.int32, (CHUNK, CHUNK), 1)
    levels = []
    blk = CHUNK
    while blk > SUB:
        half = blk // 2
        levels.append((t // blk == s // blk) & (t % blk >= half) & (s % blk < half))
        blk = half
    diag = [(s == (t // SUB) * SUB + j) & (t % SUB >= j) for j in range(SUB)]
    tri = (s <= t)
    return levels, diag, tri


def _mix_kernel(x_ref, nrm_ref, win_ref, lbl_ref, hgn_ref, cw_ref, cn_ref, wout_ref, o_ref,
                p_ref, st_ref, ybuf_ref, mix_ref):
    tt = MIX_ROWS

    @pl.when(pl.program_id(1) == 0)
    def _():
        st_ref[...] = jnp.zeros_like(st_ref)
        ybuf_ref[0:SUB, :] = jnp.zeros((SUB, CONV_WIDTH), F32)

    x = x_ref[0]
    h = _rmsnorm(x, nrm_ref[...]).astype(BF16)
    for c in range(IN_COLS // 512):
        p_ref[:, c * 512:(c + 1) * 512] = _dot(h, win_ref[:, c * 512:(c + 1) * 512])

    lbl = lbl_ref[...]
    lmax = jnp.max(lbl, axis=0, keepdims=True)
    lexp = jnp.exp(lbl - lmax)
    lb_all = lexp[0:1, :] / jnp.sum(lexp, axis=0, keepdims=True)

    level_masks, diag_masks, tri = _hgrn2_masks()
    tri_bf = jnp.where(tri, 1.0, 0.0).astype(BF16)
    hgn = hgn_ref[...]

    def chunk_body(c, carry):
        r0 = pl.multiple_of(c * CHUNK, CHUNK)
        rows = pl.ds(r0, CHUNK)
        for hh in range(HG_HEADS):
            c0 = hh * HG_DK
            q = p_ref[rows, pl.ds(c0, HG_DK)]
            fz = p_ref[rows, pl.ds(HG_WIDTH + c0, HG_DK)]
            v = p_ref[rows, pl.ds(2 * HG_WIDTH + c0, HG_DK)]
            gt = p_ref[rows, pl.ds(3 * HG_WIDTH + c0, HG_DK)]
            lb = lb_all[:, c0:c0 + HG_DK]
            f = lb + (1.0 - lb) * (1.0 / (1.0 + jnp.exp(-fz)))
            g = jnp.log(f)
            k = 1.0 - f
            qs = _silu(q)
            g_hi, g_mid, g_lo = _split3(g)
            b = _dot(tri_bf, g_hi) + _dot(tri_bf, g_mid) + _dot(tri_bf, g_lo)
            b_end = b[CHUNK - 1:CHUNK, :]
            v_bf = v.astype(BF16)

            attn = jnp.zeros((CHUNK, CHUNK), F32)
            blk = CHUNK
            for lm in level_masks:
                half = blk // 2
                nb = CHUNK // blk
                ref = b.reshape(nb, blk, HG_DK)[:, half - 1:half, :]
                ref = jnp.broadcast_to(ref, (nb, blk, HG_DK)).reshape(CHUNK, HG_DK)
                z = jnp.exp(-jnp.abs(b - ref))
                a = _dot_nt((qs * z).astype(BF16), (k * z).astype(BF16))
                attn = jnp.where(lm, a, attn)
                blk = half
            nb = CHUNK // SUB
            b3 = b.reshape(nb, SUB, HG_DK)
            k3 = k.reshape(nb, SUB, HG_DK)
            qs3 = qs.reshape(nb, SUB, HG_DK)
            for j, dm in enumerate(diag_masks):
                bj = jnp.broadcast_to(b3[:, j:j + 1, :], (nb, SUB, HG_DK))
                kj = jnp.broadcast_to(k3[:, j:j + 1, :], (nb, SUB, HG_DK))
                pj = qs3 * kj * jnp.exp(jnp.minimum(b3 - bj, 0.0))
                rj = jnp.sum(pj.reshape(CHUNK, HG_DK), axis=-1, keepdims=True)
                attn = jnp.where(dm, rj, attn)

            st = st_ref[hh]
            o = _dot(attn.astype(BF16), v_bf)
            o = o + _dot_nt((qs * jnp.exp(b)).astype(BF16), st.astype(BF16))
            ke = (k * jnp.exp(b_end - b)).astype(BF16)
            st_ref[hh] = st * jnp.exp(b_end) + _dot_tn(v_bf, ke)

            on = _rmsnorm(o, hgn) * _silu(gt)
            mix_ref[rows, pl.ds(c0, HG_DK)] = on.astype(BF16)
        return carry

    lax.fori_loop(0, tt // CHUNK, chunk_body, 0)

    cw = cw_ref[...]
    cn = cn_ref[...]
    cb = 4 * HG_WIDTH
    for j in range(tt // CHUNK):
        r = j * CHUNK
        y = p_ref[r:r + CHUNK, cb + CONV_WIDTH:cb + 2 * CONV_WIDTH] * \
            p_ref[r:r + CHUNK, cb + 2 * CONV_WIDTH:cb + 3 * CONV_WIDTH]
        ybuf_ref[SUB + r:SUB + r + CHUNK, :] = y
        y1 = ybuf_ref[SUB + r - 1:SUB + r - 1 + CHUNK, :]
        y2 = ybuf_ref[SUB + r - 2:SUB + r - 2 + CHUNK, :]
        conv = cw[0:1, :] * y2 + cw[1:2, :] * y1 + cw[2:3, :] * y
        ocv = _rmsnorm(p_ref[r:r + CHUNK, cb:cb + CONV_WIDTH] * conv, cn)
        mix_ref[r:r + CHUNK, HG_WIDTH:HG_WIDTH + CONV_WIDTH] = ocv.astype(BF16)
    ybuf_ref[0:SUB, :] = ybuf_ref[tt:tt + SUB, :]

    o_ref[0] = x + _dot(mix_ref[...], wout_ref[...])


def _mix(x, norm, w_in, lb_logits, hg_norm, conv_w, conv_norm, w_out):
    bsz, t, _ = x.shape
    tt = MIX_ROWS
    est = (4 * tt * D_MODEL * 4 + w_in.size * 2 + w_out.size * 2 + tt * IN_COLS * 4
           + HG_HEADS * HG_DK * HG_DK * 4 + (tt + SUB) * CONV_WIDTH * 4 + tt * D_MODEL * 2)
    return pl.pallas_call(
        _mix_kernel,
        grid=(bsz, t // tt),
        in_specs=[
            pl.BlockSpec((1, tt, D_MODEL), lambda b, i: (b, i, 0)),
            _resident((1, D_MODEL)),
            _resident((D_MODEL, IN_COLS)),
            _resident(lb_logits.shape),
            _resident((1, HG_DK)),
            _resident((CONV_K, CONV_WIDTH)),
            _resident((1, CONV_WIDTH)),
            _resident((D_MODEL, D_MODEL)),
        ],
        out_specs=pl.BlockSpec((1, tt, D_MODEL), lambda b, i: (b, i, 0)),
        out_shape=jax.ShapeDtypeStruct(x.shape, F32),
        scratch_shapes=[
            pltpu.VMEM((tt, IN_COLS), F32),
            pltpu.VMEM((HG_HEADS, HG_DK, HG_DK), F32),
            pltpu.VMEM((tt + SUB, CONV_WIDTH), F32),
            pltpu.VMEM((tt, D_MODEL), BF16),
        ],
        compiler_params=pltpu.CompilerParams(
            dimension_semantics=("arbitrary", "arbitrary"), vmem_limit_bytes=_vmem_limit(est)),
        name="mix",
    )(x, norm.reshape(1, D_MODEL), w_in, lb_logits, hg_norm.reshape(1, HG_DK), conv_w,
      conv_norm.reshape(1, CONV_WIDTH), w_out)


def _kv_kernel(m_ref, g_ref, w_ref, o_ref):
    h = _rmsnorm(m_ref[...], g_ref[...]).astype(BF16)
    o_ref[...] = _dot(h, w_ref[...]).astype(BF16)


def _kv(mem2d, norm, w_kv):
    n = mem2d.shape[0]
    est = 2 * KV_ROWS * D_MODEL * 4 + w_kv.size * 2 + 2 * KV_ROWS * 2 * D_MODEL * 2
    return pl.pallas_call(
        _kv_kernel,
        grid=(n // KV_ROWS,),
        in_specs=[
            pl.BlockSpec((KV_ROWS, D_MODEL), lambda i: (i, 0)),
            _resident((1, D_MODEL)),
            _resident((D_MODEL, 2 * D_MODEL)),
        ],
        out_specs=pl.BlockSpec((KV_ROWS, 2 * D_MODEL), lambda i: (i, 0)),
        out_shape=jax.ShapeDtypeStruct((n, 2 * D_MODEL), BF16),
        compiler_params=pltpu.CompilerParams(
            dimension_semantics=("arbitrary",), vmem_limit_bytes=_vmem_limit(est)),
        name="kv",
    )(mem2d, norm.reshape(1, D_MODEL), w_kv)


def _xattn_kernel(x_ref, g_ref, wq_ref, kv_ref, wo_ref, o_ref, att_ref):
    x = x_ref[0]
    h = _rmsnorm(x, g_ref[...]).astype(BF16)
    scale = 1.0 / math.sqrt(XA_HEAD_DIM)
    for hh in range(XA_HEADS):
        c0 = hh * XA_HEAD_DIM
        q = (_dot(h, wq_ref[:, c0:c0 + XA_HEAD_DIM]) * scale).astype(BF16)
        s = _dot_nt(q, kv_ref[0, :, c0:c0 + XA_HEAD_DIM])
        e = jnp.exp(s - jnp.max(s, axis=-1, keepdims=True))
        den = jnp.sum(e, axis=-1, keepdims=True)
        o = _dot(e.astype(BF16), kv_ref[0, :, D_MODEL + c0:D_MODEL + c0 + XA_HEAD_DIM])
        att_ref[:, c0:c0 + XA_HEAD_DIM] = (o * (1.0 / den)).astype(BF16)
    o_ref[0] = x + _dot(att_ref[...], wo_ref[...])


def _xattn(x, norm, w_q, kv, w_o):
    bsz, t, _ = x.shape
    tm = XA_ROWS
    est = (4 * tm * D_MODEL * 4 + w_q.size * 2 + w_o.size * 2 + 2 * N_MEM * 2 * D_MODEL * 2
           + tm * D_MODEL * 2)
    return pl.pallas_call(
        _xattn_kernel,
        grid=(bsz, t // tm),
        in_specs=[
            pl.BlockSpec((1, tm, D_MODEL), lambda b, i: (b, i, 0)),
            _resident((1, D_MODEL)),
            _resident((D_MODEL, D_MODEL)),
            pl.BlockSpec((1, N_MEM, 2 * D_MODEL), lambda b, i: (b, 0, 0)),
            _resident((D_MODEL, D_MODEL)),
        ],
        out_specs=pl.BlockSpec((1, tm, D_MODEL), lambda b, i: (b, i, 0)),
        out_shape=jax.ShapeDtypeStruct(x.shape, F32),
        scratch_shapes=[pltpu.VMEM((tm, D_MODEL), BF16)],
        compiler_params=pltpu.CompilerParams(
            dimension_semantics=("arbitrary", "arbitrary"), vmem_limit_bytes=_vmem_limit(est)),
        name="xattn",
    )(x, norm.reshape(1, D_MODEL), w_q, kv, w_o)


def kernel(x, mem, ffn1_norm, ffn1_w_gu, ffn1_w_down, mix_norm, w_in, lb_logits, hg_norm, conv_w,
           conv_norm, w_out, xa_norm, mem_norm, xa_wq, xa_wkv, xa_wo, ffn2_norm, ffn2_w_gu,
           ffn2_w_down, final_norm):
    bsz, t, d = x.shape
    assert d == D_MODEL and t % MIX_ROWS == 0 and (bsz * t) % FFN_ROWS == 0
    assert ffn1_norm.shape[0] == 1, "single-layer block"
    bf = lambda w: w[0].astype(BF16)

    x1 = _ffn(x.reshape(bsz * t, d), ffn1_norm[0], bf(ffn1_w_gu), bf(ffn1_w_down), final_norm,
              final_norm=False)
    x2 = _mix(x1.reshape(bsz, t, d), mix_norm[0], bf(w_in), lb_logits, hg_norm[0], conv_w[0],
              conv_norm[0], bf(w_out))
    kv = _kv(mem.reshape(bsz * N_MEM, d), mem_norm[0], bf(xa_wkv)).reshape(bsz, N_MEM, 2 * d)
    x3 = _xattn(x2, xa_norm[0], bf(xa_wq), kv, bf(xa_wo))
    y = _ffn(x3.reshape(bsz * t, d), ffn2_norm[0], bf(ffn2_w_gu), bf(ffn2_w_down), final_norm,
             final_norm=True)
    return y.reshape(bsz, t, d)
```

```python
import functools
import math

import jax
import jax.numpy as jnp
from jax import lax
from jax.experimental import pallas as pl
from jax.experimental.pallas import tpu as pltpu

D_MODEL = 1024
CHUNK = 64
SUB = 8
HG_WIDTH = 512
CONV_WIDTH = 512
HG_HEADS = 4
HG_DK = 128
CONV_K = 3
N_MEM = 256
XA_HEADS = 4
XA_HEAD_DIM = 256
D_FF = 2816
IN_COLS = 4 * HG_WIDTH + 3 * CONV_WIDTH
EPS = 1e-6

V7X_VMEM_BYTES = 64 * 1024 * 1024
MIB = 1024 * 1024

FFN_ROWS = 512
FFN_COLS = 256
MIX_ROWS = 1024
MACRO = 128
PIECE = 512
XA_ROWS = 512
KV_ROWS = 256

BF16 = jnp.bfloat16
F32 = jnp.float32


def _vmem_limit(estimate_bytes):
    return int(min(estimate_bytes * 5 // 4 + 8 * MIB, V7X_VMEM_BYTES - 6 * MIB))


def _resident(shape):
    nd = len(shape)
    return pl.BlockSpec(shape, lambda *_: (0,) * nd, pipeline_mode=pl.Buffered(1))


def _rmsnorm(x, g):
    ms = jnp.mean(x * x, axis=-1, keepdims=True)
    return x * lax.rsqrt(ms + EPS) * g


def _silu(x):
    return x * (1.0 / (1.0 + jnp.exp(-x)))


def _dot(a, b):
    return jnp.dot(a, b, preferred_element_type=F32)


def _dot_nt(a, b):
    return lax.dot_general(a, b, (((1,), (1,)), ((), ())), preferred_element_type=F32)


def _dot_tn(a, b):
    return lax.dot_general(a, b, (((0,), (0,)), ((), ())), preferred_element_type=F32)


def _ffn_kernel(x_ref, g_ref, wgu_ref, wd_ref, fin_ref, o_ref, act_ref, *, final_norm):
    x = x_ref[...]
    h = _rmsnorm(x, g_ref[...]).astype(BF16)
    for c in range(D_FF // FFN_COLS):
        lo = c * FFN_COLS
        gate = _dot(h, wgu_ref[:, lo:lo + FFN_COLS])
        up = _dot(h, wgu_ref[:, D_FF + lo:D_FF + lo + FFN_COLS])
        act_ref[:, lo:lo + FFN_COLS] = (_silu(gate) * up).astype(BF16)
    y = x + 0.5 * _dot(act_ref[...], wd_ref[...])
    if final_norm:
        y = _rmsnorm(y, fin_ref[...])
    o_ref[...] = y


def _ffn(x2d, norm, w_gu, w_down, fin, *, final_norm):
    n = x2d.shape[0]
    tm = FFN_ROWS
    est = (4 * tm * D_MODEL * 4 + w_gu.size * 2 + w_down.size * 2 + tm * D_FF * 2
           + 4 * tm * FFN_COLS * 4)
    return pl.pallas_call(
        functools.partial(_ffn_kernel, final_norm=final_norm),
        grid=(n // tm,),
        in_specs=[
            pl.BlockSpec((tm, D_MODEL), lambda i: (i, 0)),
            _resident((1, D_MODEL)),
            _resident((D_MODEL, 2 * D_FF)),
            _resident((D_FF, D_MODEL)),
            _resident((1, D_MODEL)),
        ],
        out_specs=pl.BlockSpec((tm, D_MODEL), lambda i: (i, 0)),
        out_shape=jax.ShapeDtypeStruct((n, D_MODEL), F32),
        scratch_shapes=[pltpu.VMEM((tm, D_FF), BF16)],
        compiler_params=pltpu.CompilerParams(
            dimension_semantics=("arbitrary",), vmem_limit_bytes=_vmem_limit(est)),
        name="ffn_final" if final_norm else "ffn",
    )(x2d, norm.reshape(1, D_MODEL), w_gu, w_down, fin.reshape(1, D_MODEL))


def _split3(x):
    hi = x.astype(BF16)
    r = x - hi.astype(F32)
    mid = r.astype(BF16)
    lo = (r - mid.astype(F32)).astype(BF16)
    return hi, mid, lo


def _hgrn2_masks():
    t = lax.broadcasted_iota(jnp.int32, (CHUNK, CHUNK), 0)
    s = lax.broadcasted_iota(jnp.int32, (CHUNK, CHUNK), 1)
    levels = []
    blk = CHUNK
    while blk > SUB:
        half = blk // 2
        levels.append((t // blk == s // blk) & (t % blk >= half) & (s % blk < half))
        blk = half
    diag = [(s == (t // SUB) * SUB + j) & (t % SUB >= j) for j in range(SUB)]
    tri = (s <= t)
    return levels, diag, tri


def _mix_kernel(x_ref, nrm_ref, win_ref, lbl_ref, hgn_ref, cw_ref, cn_ref, wout_ref, o_ref,
                p_a, p_b, mix_a, mix_b, h_ref, st_ref, ybuf_ref):
    n_macro = MIX_ROWS // MACRO

    @pl.when(pl.program_id(1) == 0)
    def _():
        st_ref[...] = jnp.zeros_like(st_ref)
        ybuf_ref[0:SUB, :] = jnp.zeros((SUB, CONV_WIDTH), F32)

    mix_b[...] = jnp.zeros((MACRO, D_MODEL), BF16)

    lbl = lbl_ref[...]
    lmax = jnp.max(lbl, axis=0, keepdims=True)
    lexp = jnp.exp(lbl - lmax)
    lb_all = lexp[0:1, :] / jnp.sum(lexp, axis=0, keepdims=True)

    level_masks, diag_masks, tri = _hgrn2_masks()
    tri_bf = jnp.where(tri, 1.0, 0.0).astype(BF16)
    hgn = hgn_ref[...]
    nrm = nrm_ref[...]
    cw = cw_ref[...]
    cn = cn_ref[...]
    cb = 4 * HG_WIDTH

    def macro_rows(m):
        return pl.ds(pl.multiple_of(m * MACRO, MACRO), MACRO)

    def in_norm(m):
        h_ref[...] = _rmsnorm(x_ref[0, macro_rows(m), :], nrm).astype(BF16)

    def in_proj_piece(p_ref, c):
        cols = slice(c * PIECE, (c + 1) * PIECE)
        p_ref[:, cols] = _dot(h_ref[...], win_ref[:, cols])

    def out_proj_piece(m, mix_ref, c):
        rows = macro_rows(m)
        cols = slice(c * PIECE, (c + 1) * PIECE)
        o_ref[0, rows, cols] = x_ref[0, rows, cols] + _dot(mix_ref[...], wout_ref[:, cols])

    def in_proj(m, p_ref):
        in_norm(m)
        for c in range(IN_COLS // PIECE):
            in_proj_piece(p_ref, c)

    def out_proj(m, mix_ref):
        for c in range(D_MODEL // PIECE):
            out_proj_piece(m, mix_ref, c)

    def hgrn2_head(p_ref, mix_ref, rows, hh):
        c0 = hh * HG_DK
        q = p_ref[rows, pl.ds(c0, HG_DK)]
        fz = p_ref[rows, pl.ds(HG_WIDTH + c0, HG_DK)]
        v = p_ref[rows, pl.ds(2 * HG_WIDTH + c0, HG_DK)]
        gt = p_ref[rows, pl.ds(3 * HG_WIDTH + c0, HG_DK)]
        lb = lb_all[:, c0:c0 + HG_DK]
        f = lb + (1.0 - lb) * (1.0 / (1.0 + jnp.exp(-fz)))
        g = jnp.log(f)
        k = 1.0 - f
        qs = _silu(q)
        g_hi, g_mid, g_lo = _split3(g)
        b = _dot(tri_bf, g_hi) + _dot(tri_bf, g_mid) + _dot(tri_bf, g_lo)
        b_end = b[CHUNK - 1:CHUNK, :]
        v_bf = v.astype(BF16)

        attn = jnp.zeros((CHUNK, CHUNK), F32)
        blk = CHUNK
        for lm in level_masks:
            half = blk // 2
            nb = CHUNK // blk
            ref = b.reshape(nb, blk, HG_DK)[:, half - 1:half, :]
            ref = jnp.broadcast_to(ref, (nb, blk, HG_DK)).reshape(CHUNK, HG_DK)
            z = jnp.exp(-jnp.abs(b - ref))
            a = _dot_nt((qs * z).astype(BF16), (k * z).astype(BF16))
            attn = jnp.where(lm, a, attn)
            blk = half
        nb = CHUNK // SUB
        b3 = b.reshape(nb, SUB, HG_DK)
        k3 = k.reshape(nb, SUB, HG_DK)
        qs3 = qs.reshape(nb, SUB, HG_DK)
        for j, dm in enumerate(diag_masks):
            bj = jnp.broadcast_to(b3[:, j:j + 1, :], (nb, SUB, HG_DK))
            kj = jnp.broadcast_to(k3[:, j:j + 1, :], (nb, SUB, HG_DK))
            pj = qs3 * kj * jnp.exp(jnp.minimum(b3 - bj, 0.0))
            rj = jnp.sum(pj.reshape(CHUNK, HG_DK), axis=-1, keepdims=True)
            attn = jnp.where(dm, rj, attn)

        st = st_ref[hh]
        o = _dot(attn.astype(BF16), v_bf)
        o = o + _dot_nt((qs * jnp.exp(b)).astype(BF16), st.astype(BF16))
        ke = (k * jnp.exp(b_end - b)).astype(BF16)
        st_ref[hh] = st * jnp.exp(b_end) + _dot_tn(v_bf, ke)

        on = _rmsnorm(o, hgn) * _silu(gt)
        mix_ref[rows, pl.ds(c0, HG_DK)] = on.astype(BF16)

    def short_conv(p_ref, mix_ref, rows):
        y = p_ref[rows, cb + CONV_WIDTH:cb + 2 * CONV_WIDTH] * \
            p_ref[rows, cb + 2 * CONV_WIDTH:cb + 3 * CONV_WIDTH]
        ybuf_ref[SUB:SUB + CHUNK, :] = y
        y1 = ybuf_ref[SUB - 1:SUB - 1 + CHUNK, :]
        y2 = ybuf_ref[SUB - 2:SUB - 2 + CHUNK, :]
        conv = cw[0:1, :] * y2 + cw[1:2, :] * y1 + cw[2:3, :] * y
        ocv = _rmsnorm(p_ref[rows, cb:cb + CONV_WIDTH] * conv, cn)
        mix_ref[rows, HG_WIDTH:HG_WIDTH + CONV_WIDTH] = ocv.astype(BF16)
        ybuf_ref[0:SUB, :] = ybuf_ref[CHUNK:CHUNK + SUB, :]

    def stage(p_cur, mix_cur, m_in, p_in, m_out, mix_out):
        mixer_units = []
        for j in range(MACRO // CHUNK):
            rows = pl.ds(j * CHUNK, CHUNK)
            for hh in range(HG_HEADS):
                mixer_units.append(functools.partial(hgrn2_head, p_cur, mix_cur, rows, hh))
            mixer_units.append(functools.partial(short_conv, p_cur, mix_cur, rows))
        mxu_units = [functools.partial(in_proj_piece, p_in, c) for c in range(IN_COLS // PIECE)]
        mxu_units += [functools.partial(out_proj_piece, m_out, mix_out, c)
                      for c in range(D_MODEL // PIECE)]
        in_norm(m_in)
        for i in range(max(len(mixer_units), len(mxu_units))):
            if i < len(mxu_units):
                mxu_units[i]()
            if i < len(mixer_units):
                mixer_units[i]()

    in_proj(0, p_a)

    def trip(j, carry):
        m = 2 * j
        stage(p_a, mix_a, m + 1, p_b, jnp.maximum(m - 1, 0), mix_b)
        stage(p_b, mix_b, jnp.minimum(m + 2, n_macro - 1), p_a, m, mix_a)
        return carry

    lax.fori_loop(0, n_macro // 2, trip, 0)
    out_proj(n_macro - 1, mix_b)


def _mix(x, norm, w_in, lb_logits, hg_norm, conv_w, conv_norm, w_out):
    bsz, t, _ = x.shape
    tt = MIX_ROWS
    est = (4 * tt * D_MODEL * 4 + w_in.size * 2 + w_out.size * 2 + 2 * MACRO * IN_COLS * 4
           + HG_HEADS * HG_DK * HG_DK * 4 + (CHUNK + SUB) * CONV_WIDTH * 4
           + 3 * MACRO * D_MODEL * 2)
    return pl.pallas_call(
        _mix_kernel,
        grid=(bsz, t // tt),
        in_specs=[
            pl.BlockSpec((1, tt, D_MODEL), lambda b, i: (b, i, 0)),
            _resident((1, D_MODEL)),
            _resident((D_MODEL, IN_COLS)),
            _resident(lb_logits.shape),
            _resident((1, HG_DK)),
            _resident((CONV_K, CONV_WIDTH)),
            _resident((1, CONV_WIDTH)),
            _resident((D_MODEL, D_MODEL)),
        ],
        out_specs=pl.BlockSpec((1, tt, D_MODEL), lambda b, i: (b, i, 0)),
        out_shape=jax.ShapeDtypeStruct(x.shape, F32),
        scratch_shapes=[
            pltpu.VMEM((MACRO, IN_COLS), F32),
            pltpu.VMEM((MACRO, IN_COLS), F32),
            pltpu.VMEM((MACRO, D_MODEL), BF16),
            pltpu.VMEM((MACRO, D_MODEL), BF16),
            pltpu.VMEM((MACRO, D_MODEL), BF16),
            pltpu.VMEM((HG_HEADS, HG_DK, HG_DK), F32),
            pltpu.VMEM((CHUNK + SUB, CONV_WIDTH), F32),
        ],
        compiler_params=pltpu.CompilerParams(
            dimension_semantics=("arbitrary", "arbitrary"), vmem_limit_bytes=_vmem_limit(est)),
        name="mix",
    )(x, norm.reshape(1, D_MODEL), w_in, lb_logits, hg_norm.reshape(1, HG_DK), conv_w,
      conv_norm.reshape(1, CONV_WIDTH), w_out)


def _kv_kernel(m_ref, g_ref, w_ref, o_ref):
    h = _rmsnorm(m_ref[...], g_ref[...]).astype(BF16)
    o_ref[...] = _dot(h, w_ref[...]).astype(BF16)


def _kv(mem2d, norm, w_kv):
    n = mem2d.shape[0]
    est = 2 * KV_ROWS * D_MODEL * 4 + w_kv.size * 2 + 2 * KV_ROWS * 2 * D_MODEL * 2
    return pl.pallas_call(
        _kv_kernel,
        grid=(n // KV_ROWS,),
        in_specs=[
            pl.BlockSpec((KV_ROWS, D_MODEL), lambda i: (i, 0)),
            _resident((1, D_MODEL)),
            _resident((D_MODEL, 2 * D_MODEL)),
        ],
        out_specs=pl.BlockSpec((KV_ROWS, 2 * D_MODEL), lambda i: (i, 0)),
        out_shape=jax.ShapeDtypeStruct((n, 2 * D_MODEL), BF16),
        compiler_params=pltpu.CompilerParams(
            dimension_semantics=("arbitrary",), vmem_limit_bytes=_vmem_limit(est)),
        name="kv",
    )(mem2d, norm.reshape(1, D_MODEL), w_kv)


def _xattn_kernel(x_ref, g_ref, wq_ref, kv_ref, wo_ref, o_ref, att_ref):
    x = x_ref[0]
    h = _rmsnorm(x, g_ref[...]).astype(BF16)
    scale = 1.0 / math.sqrt(XA_HEAD_DIM)
    for hh in range(XA_HEADS):
        c0 = hh * XA_HEAD_DIM
        q = (_dot(h, wq_ref[:, c0:c0 + XA_HEAD_DIM]) * scale).astype(BF16)
        s = _dot_nt(q, kv_ref[0, :, c0:c0 + XA_HEAD_DIM])
        e = jnp.exp(s - jnp.max(s, axis=-1, keepdims=True))
        den = jnp.sum(e, axis=-1, keepdims=True)
        o = _dot(e.astype(BF16), kv_ref[0, :, D_MODEL + c0:D_MODEL + c0 + XA_HEAD_DIM])
        att_ref[:, c0:c0 + XA_HEAD_DIM] = (o * (1.0 / den)).astype(BF16)
    o_ref[0] = x + _dot(att_ref[...], wo_ref[...])


def _xattn(x, norm, w_q, kv, w_o):
    bsz, t, _ = x.shape
    tm = XA_ROWS
    est = (4 * tm * D_MODEL * 4 + w_q.size * 2 + w_o.size * 2 + 2 * N_MEM * 2 * D_MODEL * 2
           + tm * D_MODEL * 2)
    return pl.pallas_call(
        _xattn_kernel,
        grid=(bsz, t // tm),
        in_specs=[
            pl.BlockSpec((1, tm, D_MODEL), lambda b, i: (b, i, 0)),
            _resident((1, D_MODEL)),
            _resident((D_MODEL, D_MODEL)),
            pl.BlockSpec((1, N_MEM, 2 * D_MODEL), lambda b, i: (b, 0, 0)),
            _resident((D_MODEL, D_MODEL)),
        ],
        out_specs=pl.BlockSpec((1, tm, D_MODEL), lambda b, i: (b, i, 0)),
        out_shape=jax.ShapeDtypeStruct(x.shape, F32),
        scratch_shapes=[pltpu.VMEM((tm, D_MODEL), BF16)],
        compiler_params=pltpu.CompilerParams(
            dimension_semantics=("arbitrary", "arbitrary"), vmem_limit_bytes=_vmem_limit(est)),
        name="xattn",
    )(x, norm.reshape(1, D_MODEL), w_q, kv, w_o)


def kernel(x, mem, ffn1_norm, ffn1_w_gu, ffn1_w_down, mix_norm, w_in, lb_logits, hg_norm, conv_w,
           conv_norm, w_out, xa_norm, mem_norm, xa_wq, xa_wkv, xa_wo, ffn2_norm, ffn2_w_gu,
           ffn2_w_down, final_norm):
    bsz, t, d = x.shape
    assert d == D_MODEL and t % MIX_ROWS == 0 and (bsz * t) % FFN_ROWS == 0
    assert ffn1_norm.shape[0] == 1, "single-layer block"
    bf = lambda w: w[0].astype(BF16)

    x1 = _ffn(x.reshape(bsz * t, d), ffn1_norm[0], bf(ffn1_w_gu), bf(ffn1_w_down), final_norm,
              final_norm=False)
    x2 = _mix(x1.reshape(bsz, t, d), mix_norm[0], bf(w_in), lb_logits, hg_norm[0], conv_w[0],
              conv_norm[0], bf(w_out))
    kv = _kv(mem.reshape(bsz * N_MEM, d), mem_norm[0], bf(xa_wkv)).reshape(bsz, N_MEM, 2 * d)
    x3 = _xattn(x2, xa_norm[0], bf(xa_wq), kv, bf(xa_wo))
    y = _ffn(x3.reshape(bsz * t, d), ffn2_norm[0], bf(ffn2_w_gu), bf(ffn2_w_down), final_norm,
             final_norm=True)
    return y.reshape(bsz, t, d)
```

```python
import functools
import math

import jax
import jax.numpy as jnp
from jax import lax
from jax.experimental import pallas as pl
from jax.experimental.pallas import tpu as pltpu

D_MODEL = 1024
CHUNK = 64
SUB = 8
HG_WIDTH = 512
CONV_WIDTH = 512
HG_HEADS = 4
HG_DK = 128
CONV_K = 3
N_MEM = 256
XA_HEADS = 4
XA_HEAD_DIM = 256
D_FF = 2816
IN_COLS = 4 * HG_WIDTH + 3 * CONV_WIDTH
EPS = 1e-6

V7X_VMEM_BYTES = 64 * 1024 * 1024
MIB = 1024 * 1024

FFN_ROWS = 512
FFN_COLS = 256
MIX_ROWS = 1024
MACRO = 128
PIECE = 512
XA_ROWS = 512
KV_ROWS = 256

BF16 = jnp.bfloat16
F32 = jnp.float32


def _vmem_limit(estimate_bytes):
    return int(min(estimate_bytes * 5 // 4 + 8 * MIB, V7X_VMEM_BYTES - 6 * MIB))


def _resident(shape):
    nd = len(shape)
    return pl.BlockSpec(shape, lambda *_: (0,) * nd, pipeline_mode=pl.Buffered(1))


def _rmsnorm(x, g):
    ms = jnp.mean(x * x, axis=-1, keepdims=True)
    return x * lax.rsqrt(ms + EPS) * g


def _silu(x):
    return x * (1.0 / (1.0 + jnp.exp(-x)))


def _dot(a, b):
    return jnp.dot(a, b, preferred_element_type=F32)


def _dot_nt(a, b):
    return lax.dot_general(a, b, (((1,), (1,)), ((), ())), preferred_element_type=F32)


def _dot_tn(a, b):
    return lax.dot_general(a, b, (((0,), (0,)), ((), ())), preferred_element_type=F32)


def _ffn_kernel(x_ref, g_ref, wgu_ref, wd_ref, fin_ref, o_ref, act_ref, *, final_norm):
    x = x_ref[...]
    h = _rmsnorm(x, g_ref[...]).astype(BF16)
    for c in range(D_FF // FFN_COLS):
        lo = c * FFN_COLS
        gate = _dot(h, wgu_ref[:, lo:lo + FFN_COLS])
        up = _dot(h, wgu_ref[:, D_FF + lo:D_FF + lo + FFN_COLS])
        act_ref[:, lo:lo + FFN_COLS] = (_silu(gate) * up).astype(BF16)
    y = x + 0.5 * _dot(act_ref[...], wd_ref[...])
    if final_norm:
        y = _rmsnorm(y, fin_ref[...])
    o_ref[...] = y


def _ffn(x2d, norm, w_gu, w_down, fin, *, final_norm):
    n = x2d.shape[0]
    tm = FFN_ROWS
    est = (4 * tm * D_MODEL * 4 + w_gu.size * 2 + w_down.size * 2 + tm * D_FF * 2
           + 4 * tm * FFN_COLS * 4)
    return pl.pallas_call(
        functools.partial(_ffn_kernel, final_norm=final_norm),
        grid=(n // tm,),
        in_specs=[
            pl.BlockSpec((tm, D_MODEL), lambda i: (i, 0)),
            _resident((1, D_MODEL)),
            _resident((D_MODEL, 2 * D_FF)),
            _resident((D_FF, D_MODEL)),
            _resident((1, D_MODEL)),
        ],
        out_specs=pl.BlockSpec((tm, D_MODEL), lambda i: (i, 0)),
        out_shape=jax.ShapeDtypeStruct((n, D_MODEL), F32),
        scratch_shapes=[pltpu.VMEM((tm, D_FF), BF16)],
        compiler_params=pltpu.CompilerParams(
            dimension_semantics=("arbitrary",), vmem_limit_bytes=_vmem_limit(est)),
        name="ffn_final" if final_norm else "ffn",
    )(x2d, norm.reshape(1, D_MODEL), w_gu, w_down, fin.reshape(1, D_MODEL))


def _split3(x):
    hi = x.astype(BF16)
    r = x - hi.astype(F32)
    mid = r.astype(BF16)
    lo = (r - mid.astype(F32)).astype(BF16)
    return hi, mid, lo


def _hgrn2_masks():
    t = lax.broadcasted_iota(jnp.int32, (CHUNK, CHUNK), 0)
    s = lax.broadcasted_iota(jnp.int32, (CHUNK, CHUNK), 1)
    levels = []
    blk = CHUNK
    while blk > SUB:
        half = blk // 2
        levels.append((t // blk == s // blk) & (t % blk >= half) & (s % blk < half))
        blk = half
    diag = [(s == (t // SUB) * SUB + j) & (t % SUB >= j) for j in range(SUB)]
    tri = (s <= t)
    return levels, diag, tri


def _mix_kernel(x_ref, nrm_ref, win_ref, lbl_ref, hgn_ref, cw_ref, cn_ref, wout_ref, o_ref,
                p_a, p_b, mix_a, mix_b, h_a, h_b, st_ref, ybuf_ref):
    n_macro = MIX_ROWS // MACRO

    @pl.when(pl.program_id(1) == 0)
    def _():
        st_ref[...] = jnp.zeros_like(st_ref)
        ybuf_ref[0:SUB, :] = jnp.zeros((SUB, CONV_WIDTH), F32)

    mix_b[...] = jnp.zeros((MACRO, D_MODEL), BF16)

    lbl = lbl_ref[...]
    lmax = jnp.max(lbl, axis=0, keepdims=True)
    lexp = jnp.exp(lbl - lmax)
    lb_all = lexp[0:1, :] / jnp.sum(lexp, axis=0, keepdims=True)

    level_masks, diag_masks, tri = _hgrn2_masks()
    tri_bf = jnp.where(tri, 1.0, 0.0).astype(BF16)
    hgn = hgn_ref[...]
    nrm = nrm_ref[...]
    cw = cw_ref[...]
    cn = cn_ref[...]
    cb = 4 * HG_WIDTH

    def macro_rows(m):
        return pl.ds(pl.multiple_of(m * MACRO, MACRO), MACRO)

    def in_norm(m, h_ref):
        h_ref[...] = _rmsnorm(x_ref[0, macro_rows(m), :], nrm).astype(BF16)

    def in_proj_piece(h_ref, p_ref, c):
        cols = slice(c * PIECE, (c + 1) * PIECE)
        p_ref[:, cols] = _dot(h_ref[...], win_ref[:, cols])

    def out_proj_piece(m, mix_ref, c):
        rows = macro_rows(m)
        cols = slice(c * PIECE, (c + 1) * PIECE)
        o_ref[0, rows, cols] = x_ref[0, rows, cols] + _dot(mix_ref[...], wout_ref[:, cols])


    def hg_a(p_ref, rows, hh):
        c0 = hh * HG_DK
        q = p_ref[rows, pl.ds(c0, HG_DK)]
        fz = p_ref[rows, pl.ds(HG_WIDTH + c0, HG_DK)]
        v = p_ref[rows, pl.ds(2 * HG_WIDTH + c0, HG_DK)]
        lb = lb_all[:, c0:c0 + HG_DK]
        f = lb + (1.0 - lb) * (1.0 / (1.0 + jnp.exp(-fz)))
        g = jnp.log(f)
        return dict(hh=hh, rows=rows, k=1.0 - f, qs=_silu(q), g3=_split3(g), v_bf=v.astype(BF16))

    def hg_ma(u):
        g_hi, g_mid, g_lo = u.pop("g3")
        u["b"] = _dot(tri_bf, g_hi) + _dot(tri_bf, g_mid) + _dot(tri_bf, g_lo)

    def hg_b(u):
        b, k, qs = u["b"], u["k"], u["qs"]
        b_end = b[CHUNK - 1:CHUNK, :]
        ops = []
        blk = CHUNK
        for _ in level_masks:
            half = blk // 2
            nb = CHUNK // blk
            ref = b.reshape(nb, blk, HG_DK)[:, half - 1:half, :]
            ref = jnp.broadcast_to(ref, (nb, blk, HG_DK)).reshape(CHUNK, HG_DK)
            z = jnp.exp(-jnp.abs(b - ref))
            ops.append(((qs * z).astype(BF16), (k * z).astype(BF16)))
            blk = half
        u["level_ops"] = ops
        nb = CHUNK // SUB
        b3 = b.reshape(nb, SUB, HG_DK)
        k3 = k.reshape(nb, SUB, HG_DK)
        qs3 = qs.reshape(nb, SUB, HG_DK)
        diag = []
        for j in range(SUB):
            bj = jnp.broadcast_to(b3[:, j:j + 1, :], (nb, SUB, HG_DK))
            kj = jnp.broadcast_to(k3[:, j:j + 1, :], (nb, SUB, HG_DK))
            pj = qs3 * kj * jnp.exp(jnp.minimum(b3 - bj, 0.0))
            diag.append(jnp.sum(pj.reshape(CHUNK, HG_DK), axis=-1, keepdims=True))
        u["diag"] = diag
        u["qe"] = (qs * jnp.exp(b)).astype(BF16)
        u["ke"] = (k * jnp.exp(b_end - b)).astype(BF16)
        u["decay"] = jnp.exp(b_end)
        del u["b"], u["k"], u["qs"]

    def hg_mb(u):
        u["level_out"] = [_dot_nt(ql, kl) for ql, kl in u.pop("level_ops")]

    def hg_c(u):
        attn = jnp.zeros((CHUNK, CHUNK), F32)
        for lm, a in zip(level_masks, u.pop("level_out")):
            attn = jnp.where(lm, a, attn)
        for dm, rj in zip(diag_masks, u.pop("diag")):
            attn = jnp.where(dm, rj, attn)
        u["attn"] = attn.astype(BF16)

    def hg_mc(u):
        hh, v_bf = u["hh"], u.pop("v_bf")
        st = st_ref[hh]
        o = _dot(u.pop("attn"), v_bf)
        u["o"] = o + _dot_nt(u.pop("qe"), st.astype(BF16))
        st_ref[hh] = st * u.pop("decay") + _dot_tn(v_bf, u.pop("ke"))

    def hg_d(u, p_ref, mix_ref):
        c0 = u["hh"] * HG_DK
        gt = p_ref[u["rows"], pl.ds(3 * HG_WIDTH + c0, HG_DK)]
        on = _rmsnorm(u.pop("o"), hgn) * _silu(gt)
        mix_ref[u["rows"], pl.ds(c0, HG_DK)] = on.astype(BF16)

    def short_conv(p_ref, mix_ref, rows):
        y = p_ref[rows, cb + CONV_WIDTH:cb + 2 * CONV_WIDTH] * \
            p_ref[rows, cb + 2 * CONV_WIDTH:cb + 3 * CONV_WIDTH]
        ybuf_ref[SUB:SUB + CHUNK, :] = y
        y1 = ybuf_ref[SUB - 1:SUB - 1 + CHUNK, :]
        y2 = ybuf_ref[SUB - 2:SUB - 2 + CHUNK, :]
        conv = cw[0:1, :] * y2 + cw[1:2, :] * y1 + cw[2:3, :] * y
        ocv = _rmsnorm(p_ref[rows, cb:cb + CONV_WIDTH] * conv, cn)
        mix_ref[rows, HG_WIDTH:HG_WIDTH + CONV_WIDTH] = ocv.astype(BF16)
        ybuf_ref[0:SUB, :] = ybuf_ref[CHUNK:CHUNK + SUB, :]

    def stage(p_cur, mix_cur, h_in, p_in, m_out, mix_out, m_next, h_next):
        big = [functools.partial(in_proj_piece, h_in, p_in, c) for c in range(IN_COLS // PIECE)]
        big += [functools.partial(out_proj_piece, m_out, mix_out, c)
                for c in range(D_MODEL // PIECE)]
        chunks = [pl.ds(j * CHUNK, CHUNK) for j in range(MACRO // CHUNK)]
        big[0]()
        units = [hg_a(p_cur, rows, hh) for rows in chunks for hh in range(HG_HEADS)]
        for u in units:
            hg_ma(u)
        big[1](); big[2](); big[3]()
        for u in units:
            hg_b(u)
        short_conv(p_cur, mix_cur, chunks[0])
        for u in units:
            hg_mb(u)
        big[4](); big[5]()
        for u in units:
            hg_c(u)
        short_conv(p_cur, mix_cur, chunks[1])
        in_norm(m_next, h_next)
        for u in units[:HG_HEADS]:
            hg_mc(u)
        big[6]()
        for u in units[HG_HEADS:]:
            hg_mc(u)
        big[7](); big[8]()
        for u in units:
            hg_d(u, p_cur, mix_cur)

    last = n_macro - 1
    in_norm(0, h_b)
    for c in range(IN_COLS // PIECE):
        in_proj_piece(h_b, p_a, c)
    in_norm(1, h_a)

    def trip(j, carry):
        m = 2 * j
        stage(p_a, mix_a, h_a, p_b, jnp.maximum(m - 1, 0), mix_b, jnp.minimum(m + 2, last), h_b)
        stage(p_b, mix_b, h_b, p_a, m, mix_a, jnp.minimum(m + 3, last), h_a)
        return carry

    lax.fori_loop(0, n_macro // 2, trip, 0)
    for c in range(D_MODEL // PIECE):
        out_proj_piece(last, mix_b, c)


def _mix(x, norm, w_in, lb_logits, hg_norm, conv_w, conv_norm, w_out):
    bsz, t, _ = x.shape
    tt = MIX_ROWS
    est = (4 * tt * D_MODEL * 4 + w_in.size * 2 + w_out.size * 2 + 2 * MACRO * IN_COLS * 4
           + HG_HEADS * HG_DK * HG_DK * 4 + (CHUNK + SUB) * CONV_WIDTH * 4
           + 4 * MACRO * D_MODEL * 2)
    return pl.pallas_call(
        _mix_kernel,
        grid=(bsz, t // tt),
        in_specs=[
            pl.BlockSpec((1, tt, D_MODEL), lambda b, i: (b, i, 0)),
            _resident((1, D_MODEL)),
            _resident((D_MODEL, IN_COLS)),
            _resident(lb_logits.shape),
            _resident((1, HG_DK)),
            _resident((CONV_K, CONV_WIDTH)),
            _resident((1, CONV_WIDTH)),
            _resident((D_MODEL, D_MODEL)),
        ],
        out_specs=pl.BlockSpec((1, tt, D_MODEL), lambda b, i: (b, i, 0)),
        out_shape=jax.ShapeDtypeStruct(x.shape, F32),
        scratch_shapes=[
            pltpu.VMEM((MACRO, IN_COLS), F32),
            pltpu.VMEM((MACRO, IN_COLS), F32),
            pltpu.VMEM((MACRO, D_MODEL), BF16),
            pltpu.VMEM((MACRO, D_MODEL), BF16),
            pltpu.VMEM((MACRO, D_MODEL), BF16),
            pltpu.VMEM((MACRO, D_MODEL), BF16),
            pltpu.VMEM((HG_HEADS, HG_DK, HG_DK), F32),
            pltpu.VMEM((CHUNK + SUB, CONV_WIDTH), F32),
        ],
        compiler_params=pltpu.CompilerParams(
            dimension_semantics=("arbitrary", "arbitrary"), vmem_limit_bytes=_vmem_limit(est)),
        name="mix",
    )(x, norm.reshape(1, D_MODEL), w_in, lb_logits, hg_norm.reshape(1, HG_DK), conv_w,
      conv_norm.reshape(1, CONV_WIDTH), w_out)


def _kv_kernel(m_ref, g_ref, w_ref, o_ref):
    h = _rmsnorm(m_ref[...], g_ref[...]).astype(BF16)
    o_ref[...] = _dot(h, w_ref[...]).astype(BF16)


def _kv(mem2d, norm, w_kv):
    n = mem2d.shape[0]
    est = 2 * KV_ROWS * D_MODEL * 4 + w_kv.size * 2 + 2 * KV_ROWS * 2 * D_MODEL * 2
    return pl.pallas_call(
        _kv_kernel,
        grid=(n // KV_ROWS,),
        in_specs=[
            pl.BlockSpec((KV_ROWS, D_MODEL), lambda i: (i, 0)),
            _resident((1, D_MODEL)),
            _resident((D_MODEL, 2 * D_MODEL)),
        ],
        out_specs=pl.BlockSpec((KV_ROWS, 2 * D_MODEL), lambda i: (i, 0)),
        out_shape=jax.ShapeDtypeStruct((n, 2 * D_MODEL), BF16),
        compiler_params=pltpu.CompilerParams(
            dimension_semantics=("arbitrary",), vmem_limit_bytes=_vmem_limit(est)),
        name="kv",
    )(mem2d, norm.reshape(1, D_MODEL), w_kv)


def _xattn_kernel(x_ref, g_ref, wq_ref, kv_ref, wo_ref, o_ref, att_ref):
    x = x_ref[0]
    h = _rmsnorm(x, g_ref[...]).astype(BF16)
    scale = 1.0 / math.sqrt(XA_HEAD_DIM)
    for hh in range(XA_HEADS):
        c0 = hh * XA_HEAD_DIM
        q = (_dot(h, wq_ref[:, c0:c0 + XA_HEAD_DIM]) * scale).astype(BF16)
        s = _dot_nt(q, kv_ref[0, :, c0:c0 + XA_HEAD_DIM])
        e = jnp.exp(s - jnp.max(s, axis=-1, keepdims=True))
        den = jnp.sum(e, axis=-1, keepdims=True)
        o = _dot(e.astype(BF16), kv_ref[0, :, D_MODEL + c0:D_MODEL + c0 + XA_HEAD_DIM])
        att_ref[:, c0:c0 + XA_HEAD_DIM] = (o * (1.0 / den)).astype(BF16)
    o_ref[0] = x + _dot(att_ref[...], wo_ref[...])


def _xattn(x, norm, w_q, kv, w_o):
    bsz, t, _ = x.shape
    tm = XA_ROWS
    est = (4 * tm * D_MODEL * 4 + w_q.size * 2 + w_o.size * 2 + 2 * N_MEM * 2 * D_MODEL * 2
           + tm * D_MODEL * 2)
    return pl.pallas_call(
        _xattn_kernel,
        grid=(bsz, t // tm),
        in_specs=[
            pl.BlockSpec((1, tm, D_MODEL), lambda b, i: (b, i, 0)),
            _resident((1, D_MODEL)),
            _resident((D_MODEL, D_MODEL)),
            pl.BlockSpec((1, N_MEM, 2 * D_MODEL), lambda b, i: (b, 0, 0)),
            _resident((D_MODEL, D_MODEL)),
        ],
        out_specs=pl.BlockSpec((1, tm, D_MODEL), lambda b, i: (b, i, 0)),
        out_shape=jax.ShapeDtypeStruct(x.shape, F32),
        scratch_shapes=[pltpu.VMEM((tm, D_MODEL), BF16)],
        compiler_params=pltpu.CompilerParams(
            dimension_semantics=("arbitrary", "arbitrary"), vmem_limit_bytes=_vmem_limit(est)),
        name="xattn",
    )(x, norm.reshape(1, D_MODEL), w_q, kv, w_o)


def kernel(x, mem, ffn1_norm, ffn1_w_gu, ffn1_w_down, mix_norm, w_in, lb_logits, hg_norm, conv_w,
           conv_norm, w_out, xa_norm, mem_norm, xa_wq, xa_wkv, xa_wo, ffn2_norm, ffn2_w_gu,
           ffn2_w_down, final_norm):
    bsz, t, d = x.shape
    assert d == D_MODEL and t % MIX_ROWS == 0 and (bsz * t) % FFN_ROWS == 0
    assert ffn1_norm.shape[0] == 1, "single-layer block"
    bf = lambda w: w[0].astype(BF16)

    x1 = _ffn(x.reshape(bsz * t, d), ffn1_norm[0], bf(ffn1_w_gu), bf(ffn1_w_down), final_norm,
              final_norm=False)
    x2 = _mix(x1.reshape(bsz, t, d), mix_norm[0], bf(w_in), lb_logits, hg_norm[0], conv_w[0],
              conv_norm[0], bf(w_out))
    kv = _kv(mem.reshape(bsz * N_MEM, d), mem_norm[0], bf(xa_wkv)).reshape(bsz, N_MEM, 2 * d)
    x3 = _xattn(x2, xa_norm[0], bf(xa_wq), kv, bf(xa_wo))
    y = _ffn(x3.reshape(bsz * t, d), ffn2_norm[0], bf(ffn2_w_gu), bf(ffn2_w_down), final_norm,
             final_norm=True)
    return y.reshape(bsz, t, d)
```

```python
import functools
import math

import jax
import jax.numpy as jnp
from jax import lax
from jax.experimental import pallas as pl
from jax.experimental.pallas import tpu as pltpu

D_MODEL = 1024
CHUNK = 64
SUB = 8
DIAG = 4
HG_WIDTH = 512
CONV_WIDTH = 512
HG_HEADS = 4
HG_DK = 128
CONV_K = 3
N_MEM = 256
XA_HEADS = 4
XA_HEAD_DIM = 256
D_FF = 2816
IN_COLS = 4 * HG_WIDTH + 3 * CONV_WIDTH
EPS = 1e-6

V7X_VMEM_BYTES = 64 * 1024 * 1024
MIB = 1024 * 1024

FFN_ROWS = 512
FFN_COLS = 256
MIX_ROWS = 1024
MACRO = 128
PIECE = 512
XA_ROWS = 512
KV_ROWS = 256

BF16 = jnp.bfloat16
F32 = jnp.float32


def _vmem_limit(estimate_bytes):
    return int(min(estimate_bytes * 5 // 4 + 8 * MIB, V7X_VMEM_BYTES - 6 * MIB))


def _resident(shape):
    nd = len(shape)
    return pl.BlockSpec(shape, lambda *_: (0,) * nd, pipeline_mode=pl.Buffered(1))


def _rmsnorm(x, g):
    ms = jnp.mean(x * x, axis=-1, keepdims=True)
    return x * lax.rsqrt(ms + EPS) * g


def _silu(x):
    return x * (1.0 / (1.0 + jnp.exp(-x)))


def _dot(a, b):
    return jnp.dot(a, b, preferred_element_type=F32)


def _dot_nt(a, b):
    return lax.dot_general(a, b, (((1,), (1,)), ((), ())), preferred_element_type=F32)


def _dot_tn(a, b):
    return lax.dot_general(a, b, (((0,), (0,)), ((), ())), preferred_element_type=F32)


def _ffn_kernel(x_ref, g_ref, wgu_ref, wd_ref, fin_ref, o_ref, act_ref, *, final_norm):
    x = x_ref[...]
    h = _rmsnorm(x, g_ref[...]).astype(BF16)
    for c in range(D_FF // FFN_COLS):
        lo = c * FFN_COLS
        gate = _dot(h, wgu_ref[:, lo:lo + FFN_COLS])
        up = _dot(h, wgu_ref[:, D_FF + lo:D_FF + lo + FFN_COLS])
        act_ref[:, lo:lo + FFN_COLS] = (_silu(gate) * up).astype(BF16)
    y = x + 0.5 * _dot(act_ref[...], wd_ref[...])
    if final_norm:
        y = _rmsnorm(y, fin_ref[...])
    o_ref[...] = y


def _ffn(x2d, norm, w_gu, w_down, fin, *, final_norm):
    n = x2d.shape[0]
    tm = FFN_ROWS
    est = (4 * tm * D_MODEL * 4 + w_gu.size * 2 + w_down.size * 2 + tm * D_FF * 2
           + 4 * tm * FFN_COLS * 4)
    return pl.pallas_call(
        functools.partial(_ffn_kernel, final_norm=final_norm),
        grid=(n // tm,),
        in_specs=[
            pl.BlockSpec((tm, D_MODEL), lambda i: (i, 0)),
            _resident((1, D_MODEL)),
            _resident((D_MODEL, 2 * D_FF)),
            _resident((D_FF, D_MODEL)),
            _resident((1, D_MODEL)),
        ],
        out_specs=pl.BlockSpec((tm, D_MODEL), lambda i: (i, 0)),
        out_shape=jax.ShapeDtypeStruct((n, D_MODEL), F32),
        scratch_shapes=[pltpu.VMEM((tm, D_FF), BF16)],
        compiler_params=pltpu.CompilerParams(
            dimension_semantics=("arbitrary",), vmem_limit_bytes=_vmem_limit(est)),
        name="ffn_final" if final_norm else "ffn",
    )(x2d, norm.reshape(1, D_MODEL), w_gu, w_down, fin.reshape(1, D_MODEL))


def _split3(x):
    hi = x.astype(BF16)
    r = x - hi.astype(F32)
    mid = r.astype(BF16)
    lo = (r - mid.astype(F32)).astype(BF16)
    return hi, mid, lo


def _hgrn2_masks():
    t = lax.broadcasted_iota(jnp.int32, (CHUNK, CHUNK), 0)
    s = lax.broadcasted_iota(jnp.int32, (CHUNK, CHUNK), 1)
    levels = []
    blk = CHUNK
    while blk > DIAG:
        half = blk // 2
        levels.append((t // blk == s // blk) & (t % blk >= half) & (s % blk < half))
        blk = half
    diag = [(s == t - d) & (t % DIAG >= d) for d in range(DIAG)]
    tri = (s <= t)
    return levels, diag, tri


def _mix_kernel(x_ref, nrm_ref, win_ref, lbl_ref, hgn_ref, cw_ref, cn_ref, wout_ref, o_ref,
                p_a, p_b, mix_a, mix_b, h_a, h_b, st_ref, ybuf_ref):
    n_macro = MIX_ROWS // MACRO

    @pl.when(pl.program_id(1) == 0)
    def _():
        st_ref[...] = jnp.zeros_like(st_ref)
        ybuf_ref[0:SUB, :] = jnp.zeros((SUB, CONV_WIDTH), F32)

    mix_b[...] = jnp.zeros((MACRO, D_MODEL), BF16)

    lbl = lbl_ref[...]
    lmax = jnp.max(lbl, axis=0, keepdims=True)
    lexp = jnp.exp(lbl - lmax)
    lb_all = lexp[0:1, :] / jnp.sum(lexp, axis=0, keepdims=True)

    level_masks, diag_masks, tri = _hgrn2_masks()
    tri_bf = jnp.where(tri, 1.0, 0.0).astype(BF16)
    tri3_bf = jnp.concatenate([tri_bf] * 3, axis=1)
    hgn = hgn_ref[...]
    nrm = nrm_ref[...]
    cw = cw_ref[...]
    cn = cn_ref[...]
    cb = 4 * HG_WIDTH

    def macro_rows(m):
        return pl.ds(pl.multiple_of(m * MACRO, MACRO), MACRO)

    def in_norm(m, h_ref):
        h_ref[...] = _rmsnorm(x_ref[0, macro_rows(m), :], nrm).astype(BF16)

    def in_proj_piece(h_ref, p_ref, c):
        cols = slice(c * PIECE, (c + 1) * PIECE)
        p_ref[:, cols] = _dot(h_ref[...], win_ref[:, cols])

    def out_proj_piece(m, mix_ref, c):
        rows = macro_rows(m)
        cols = slice(c * PIECE, (c + 1) * PIECE)
        o_ref[0, rows, cols] = x_ref[0, rows, cols] + _dot(mix_ref[...], wout_ref[:, cols])


    def hg_a(p_ref, rows, hh):
        c0 = hh * HG_DK
        q = p_ref[rows, pl.ds(c0, HG_DK)]
        fz = p_ref[rows, pl.ds(HG_WIDTH + c0, HG_DK)]
        v = p_ref[rows, pl.ds(2 * HG_WIDTH + c0, HG_DK)]
        lb = lb_all[:, c0:c0 + HG_DK]
        f = lb + (1.0 - lb) * (1.0 / (1.0 + jnp.exp(-fz)))
        g = jnp.log(f)
        return dict(hh=hh, rows=rows, f=f, k=1.0 - f, qs=_silu(q), v_bf=v.astype(BF16),
                    g3=jnp.concatenate(_split3(g), axis=0))

    def hg_ma(u):
        u["b"] = _dot(tri3_bf, u.pop("g3"))

    def hg_b(u):
        b, k, qs = u["b"], u["k"], u["qs"]
        b_end = b[CHUNK - 1:CHUNK, :]
        ops = []
        blk = CHUNK
        for _ in level_masks:
            half = blk // 2
            nb = CHUNK // blk
            ref = b.reshape(nb, blk, HG_DK)[:, half - 1:half, :]
            ref = jnp.broadcast_to(ref, (nb, blk, HG_DK)).reshape(CHUNK, HG_DK)
            z = jnp.exp(-jnp.abs(b - ref))
            ops.append(((qs * z).astype(BF16), (k * z).astype(BF16)))
            blk = half
        u["level_ops"] = ops
        f = u.pop("f")
        diag = [jnp.sum(qs * k, axis=-1, keepdims=True)]
        decay = f
        for d in range(1, DIAG):
            f_d = pltpu.roll(f, d, 0)
            diag.append(jnp.sum(qs * (1.0 - f_d) * decay, axis=-1, keepdims=True))
            decay = decay * f_d
        u["diag"] = diag
        u["qe"] = (qs * jnp.exp(b)).astype(BF16)
        u["ke"] = (k * jnp.exp(b_end - b)).astype(BF16)
        u["decay"] = jnp.exp(b_end)
        del u["b"], u["k"], u["qs"]

    def hg_mb(u):
        u["level_out"] = [_dot_nt(ql, kl) for ql, kl in u.pop("level_ops")]

    def hg_c(u):
        attn = jnp.zeros((CHUNK, CHUNK), F32)
        for lm, a in zip(level_masks, u.pop("level_out")):
            attn = jnp.where(lm, a, attn)
        for dm, rj in zip(diag_masks, u.pop("diag")):
            attn = jnp.where(dm, rj, attn)
        u["attn"] = attn.astype(BF16)

    def hg_mc(u):
        hh, v_bf = u["hh"], u.pop("v_bf")
        st = st_ref[hh]
        o = _dot(u.pop("attn"), v_bf)
        u["o"] = o + _dot_nt(u.pop("qe"), st.astype(BF16))
        st_ref[hh] = st * u.pop("decay") + _dot_tn(v_bf, u.pop("ke"))

    def hg_d(u, p_ref, mix_ref):
        c0 = u["hh"] * HG_DK
        gt = p_ref[u["rows"], pl.ds(3 * HG_WIDTH + c0, HG_DK)]
        on = _rmsnorm(u.pop("o"), hgn) * _silu(gt)
        mix_ref[u["rows"], pl.ds(c0, HG_DK)] = on.astype(BF16)

    def short_conv(p_ref, mix_ref, rows):
        y = p_ref[rows, cb + CONV_WIDTH:cb + 2 * CONV_WIDTH] * \
            p_ref[rows, cb + 2 * CONV_WIDTH:cb + 3 * CONV_WIDTH]
        ybuf_ref[SUB:SUB + CHUNK, :] = y
        y1 = ybuf_ref[SUB - 1:SUB - 1 + CHUNK, :]
        y2 = ybuf_ref[SUB - 2:SUB - 2 + CHUNK, :]
        conv = cw[0:1, :] * y2 + cw[1:2, :] * y1 + cw[2:3, :] * y
        ocv = _rmsnorm(p_ref[rows, cb:cb + CONV_WIDTH] * conv, cn)
        mix_ref[rows, HG_WIDTH:HG_WIDTH + CONV_WIDTH] = ocv.astype(BF16)
        ybuf_ref[0:SUB, :] = ybuf_ref[CHUNK:CHUNK + SUB, :]

    def stage(p_cur, mix_cur, h_in, p_in, m_out, mix_out, m_next, h_next):
        big = [functools.partial(in_proj_piece, h_in, p_in, c) for c in range(IN_COLS // PIECE)]
        big += [functools.partial(out_proj_piece, m_out, mix_out, c)
                for c in range(D_MODEL // PIECE)]
        c0, c1 = [pl.ds(j * CHUNK, CHUNK) for j in range(MACRO // CHUNK)]

        def each(phase, units, *args):
            for u in units:
                phase(u, *args)

        big[0](); big[1]()
        g0 = [hg_a(p_cur, c0, hh) for hh in range(HG_HEADS)]
        g1 = [hg_a(p_cur, c1, hh) for hh in range(HG_HEADS)]
        each(hg_ma, g0); big[2]()
        short_conv(p_cur, mix_cur, c0)
        each(hg_ma, g1); big[3]()
        each(hg_b, g0); each(hg_mb, g0); big[4]()
        each(hg_b, g1); each(hg_mb, g1); big[5]()
        each(hg_c, g0); each(hg_mc, g0); big[6]()
        each(hg_c, g1)
        in_norm(m_next, h_next)
        each(hg_mc, g1); big[7](); big[8]()
        each(hg_d, g0, p_cur, mix_cur)
        each(hg_d, g1, p_cur, mix_cur)
        short_conv(p_cur, mix_cur, c1)

    last = n_macro - 1
    in_norm(0, h_b)
    for c in range(IN_COLS // PIECE):
        in_proj_piece(h_b, p_a, c)
    in_norm(1, h_a)

    def trip(j, carry):
        m = 2 * j
        stage(p_a, mix_a, h_a, p_b, jnp.maximum(m - 1, 0), mix_b, jnp.minimum(m + 2, last), h_b)
        stage(p_b, mix_b, h_b, p_a, m, mix_a, jnp.minimum(m + 3, last), h_a)
        return carry

    lax.fori_loop(0, n_macro // 2, trip, 0)
    for c in range(D_MODEL // PIECE):
        out_proj_piece(last, mix_b, c)


def _mix(x, norm, w_in, lb_logits, hg_norm, conv_w, conv_norm, w_out):
    bsz, t, _ = x.shape
    tt = MIX_ROWS
    est = (4 * tt * D_MODEL * 4 + w_in.size * 2 + w_out.size * 2 + 2 * MACRO * IN_COLS * 4
           + HG_HEADS * HG_DK * HG_DK * 4 + (CHUNK + SUB) * CONV_WIDTH * 4
           + 4 * MACRO * D_MODEL * 2)
    return pl.pallas_call(
        _mix_kernel,
        grid=(bsz, t // tt),
        in_specs=[
            pl.BlockSpec((1, tt, D_MODEL), lambda b, i: (b, i, 0)),
            _resident((1, D_MODEL)),
            _resident((D_MODEL, IN_COLS)),
            _resident(lb_logits.shape),
            _resident((1, HG_DK)),
            _resident((CONV_K, CONV_WIDTH)),
            _resident((1, CONV_WIDTH)),
            _resident((D_MODEL, D_MODEL)),
        ],
        out_specs=pl.BlockSpec((1, tt, D_MODEL), lambda b, i: (b, i, 0)),
        out_shape=jax.ShapeDtypeStruct(x.shape, F32),
        scratch_shapes=[
            pltpu.VMEM((MACRO, IN_COLS), F32),
            pltpu.VMEM((MACRO, IN_COLS), F32),
            pltpu.VMEM((MACRO, D_MODEL), BF16),
            pltpu.VMEM((MACRO, D_MODEL), BF16),
            pltpu.VMEM((MACRO, D_MODEL), BF16),
            pltpu.VMEM((MACRO, D_MODEL), BF16),
            pltpu.VMEM((HG_HEADS, HG_DK, HG_DK), F32),
            pltpu.VMEM((CHUNK + SUB, CONV_WIDTH), F32),
        ],
        compiler_params=pltpu.CompilerParams(
            dimension_semantics=("arbitrary", "arbitrary"), vmem_limit_bytes=_vmem_limit(est)),
        name="mix",
    )(x, norm.reshape(1, D_MODEL), w_in, lb_logits, hg_norm.reshape(1, HG_DK), conv_w,
      conv_norm.reshape(1, CONV_WIDTH), w_out)


def _kv_kernel(m_ref, g_ref, w_ref, o_ref):
    h = _rmsnorm(m_ref[...], g_ref[...]).astype(BF16)
    o_ref[...] = _dot(h, w_ref[...]).astype(BF16)


def _kv(mem2d, norm, w_kv):
    n = mem2d.shape[0]
    est = 2 * KV_ROWS * D_MODEL * 4 + w_kv.size * 2 + 2 * KV_ROWS * 2 * D_MODEL * 2
    return pl.pallas_call(
        _kv_kernel,
        grid=(n // KV_ROWS,),
        in_specs=[
            pl.BlockSpec((KV_ROWS, D_MODEL), lambda i: (i, 0)),
            _resident((1, D_MODEL)),
            _resident((D_MODEL, 2 * D_MODEL)),
        ],
        out_specs=pl.BlockSpec((KV_ROWS, 2 * D_MODEL), lambda i: (i, 0)),
        out_shape=jax.ShapeDtypeStruct((n, 2 * D_MODEL), BF16),
        compiler_params=pltpu.CompilerParams(
            dimension_semantics=("arbitrary",), vmem_limit_bytes=_vmem_limit(est)),
        name="kv",
    )(mem2d, norm.reshape(1, D_MODEL), w_kv)


def _xattn_kernel(x_ref, g_ref, wq_ref, kv_ref, wo_ref, o_ref, att_ref):
    x = x_ref[0]
    h = _rmsnorm(x, g_ref[...]).astype(BF16)
    scale = 1.0 / math.sqrt(XA_HEAD_DIM)
    for hh in range(XA_HEADS):
        c0 = hh * XA_HEAD_DIM
        q = (_dot(h, wq_ref[:, c0:c0 + XA_HEAD_DIM]) * scale).astype(BF16)
        s = _dot_nt(q, kv_ref[0, :, c0:c0 + XA_HEAD_DIM])
        e = jnp.exp(s - jnp.max(s, axis=-1, keepdims=True))
        den = jnp.sum(e, axis=-1, keepdims=True)
        o = _dot(e.astype(BF16), kv_ref[0, :, D_MODEL + c0:D_MODEL + c0 + XA_HEAD_DIM])
        att_ref[:, c0:c0 + XA_HEAD_DIM] = (o * (1.0 / den)).astype(BF16)
    o_ref[0] = x + _dot(att_ref[...], wo_ref[...])


def _xattn(x, norm, w_q, kv, w_o):
    bsz, t, _ = x.shape
    tm = XA_ROWS
    est = (4 * tm * D_MODEL * 4 + w_q.size * 2 + w_o.size * 2 + 2 * N_MEM * 2 * D_MODEL * 2
           + tm * D_MODEL * 2)
    return pl.pallas_call(
        _xattn_kernel,
        grid=(bsz, t // tm),
        in_specs=[
            pl.BlockSpec((1, tm, D_MODEL), lambda b, i: (b, i, 0)),
            _resident((1, D_MODEL)),
            _resident((D_MODEL, D_MODEL)),
            pl.BlockSpec((1, N_MEM, 2 * D_MODEL), lambda b, i: (b, 0, 0)),
            _resident((D_MODEL, D_MODEL)),
        ],
        out_specs=pl.BlockSpec((1, tm, D_MODEL), lambda b, i: (b, i, 0)),
        out_shape=jax.ShapeDtypeStruct(x.shape, F32),
        scratch_shapes=[pltpu.VMEM((tm, D_MODEL), BF16)],
        compiler_params=pltpu.CompilerParams(
            dimension_semantics=("arbitrary", "arbitrary"), vmem_limit_bytes=_vmem_limit(est)),
        name="xattn",
    )(x, norm.reshape(1, D_MODEL), w_q, kv, w_o)


def kernel(x, mem, ffn1_norm, ffn1_w_gu, ffn1_w_down, mix_norm, w_in, lb_logits, hg_norm, conv_w,
           conv_norm, w_out, xa_norm, mem_norm, xa_wq, xa_wkv, xa_wo, ffn2_norm, ffn2_w_gu,
           ffn2_w_down, final_norm):
    bsz, t, d = x.shape
    assert d == D_MODEL and t % MIX_ROWS == 0 and (bsz * t) % FFN_ROWS == 0
    assert ffn1_norm.shape[0] == 1, "single-layer block"
    bf = lambda w: w[0].astype(BF16)

    x1 = _ffn(x.reshape(bsz * t, d), ffn1_norm[0], bf(ffn1_w_gu), bf(ffn1_w_down), final_norm,
              final_norm=False)
    x2 = _mix(x1.reshape(bsz, t, d), mix_norm[0], bf(w_in), lb_logits, hg_norm[0], conv_w[0],
              conv_norm[0], bf(w_out))
    kv = _kv(mem.reshape(bsz * N_MEM, d), mem_norm[0], bf(xa_wkv)).reshape(bsz, N_MEM, 2 * d)
    x3 = _xattn(x2, xa_norm[0], bf(xa_wq), kv, bf(xa_wo))
    y = _ffn(x3.reshape(bsz * t, d), ffn2_norm[0], bf(ffn2_w_gu), bf(ffn2_w_down), final_norm,
             final_norm=True)
    return y.reshape(bsz, t, d)
```

```python
import functools
import math

import jax
import jax.numpy as jnp
from jax import lax
from jax.experimental import pallas as pl
from jax.experimental.pallas import tpu as pltpu

D_MODEL = 1024
CHUNK = 64
SUB = 8
DIAG = 4
HG_WIDTH = 512
CONV_WIDTH = 512
HG_HEADS = 4
HG_DK = 128
CONV_K = 3
N_MEM = 256
XA_HEADS = 4
XA_HEAD_DIM = 256
D_FF = 2816
IN_COLS = 4 * HG_WIDTH + 3 * CONV_WIDTH
EPS = 1e-6

V7X_VMEM_BYTES = 64 * 1024 * 1024
MIB = 1024 * 1024

FFN_ROWS = 512
FFN_COLS = 256
MIX_ROWS = 1024
MACRO = 128
PIECE = 512
XA_ROWS = 1024
XA_SUB = 256
KV_ROWS = 256

BF16 = jnp.bfloat16
F32 = jnp.float32


def _vmem_limit(estimate_bytes):
    return int(min(estimate_bytes * 5 // 4 + 8 * MIB, V7X_VMEM_BYTES - 6 * MIB))


def _resident(shape):
    nd = len(shape)
    return pl.BlockSpec(shape, lambda *_: (0,) * nd, pipeline_mode=pl.Buffered(1))


def _rmsnorm(x, g):
    ms = jnp.mean(x * x, axis=-1, keepdims=True)
    return x * lax.rsqrt(ms + EPS) * g


def _silu(x):
    return x * (1.0 / (1.0 + jnp.exp(-x)))


def _dot(a, b):
    return jnp.dot(a, b, preferred_element_type=F32)


def _dot_nt(a, b):
    return lax.dot_general(a, b, (((1,), (1,)), ((), ())), preferred_element_type=F32)


def _dot_tn(a, b):
    return lax.dot_general(a, b, (((0,), (0,)), ((), ())), preferred_element_type=F32)


def _ffn_kernel(x_ref, g_ref, wgu_ref, wd_ref, fin_ref, o_ref, act_ref, *, final_norm):
    x = x_ref[...]
    h = _rmsnorm(x, g_ref[...]).astype(BF16)
    for c in range(D_FF // FFN_COLS):
        lo = c * FFN_COLS
        gate = _dot(h, wgu_ref[:, lo:lo + FFN_COLS])
        up = _dot(h, wgu_ref[:, D_FF + lo:D_FF + lo + FFN_COLS])
        act_ref[:, lo:lo + FFN_COLS] = (_silu(gate) * up).astype(BF16)
    y = x + 0.5 * _dot(act_ref[...], wd_ref[...])
    if final_norm:
        y = _rmsnorm(y, fin_ref[...])
    o_ref[...] = y


def _ffn(x2d, norm, w_gu, w_down, fin, *, final_norm):
    n = x2d.shape[0]
    tm = FFN_ROWS
    est = (4 * tm * D_MODEL * 4 + w_gu.size * 2 + w_down.size * 2 + tm * D_FF * 2
           + 4 * tm * FFN_COLS * 4)
    return pl.pallas_call(
        functools.partial(_ffn_kernel, final_norm=final_norm),
        grid=(n // tm,),
        in_specs=[
            pl.BlockSpec((tm, D_MODEL), lambda i: (i, 0)),
            _resident((1, D_MODEL)),
            _resident((D_MODEL, 2 * D_FF)),
            _resident((D_FF, D_MODEL)),
            _resident((1, D_MODEL)),
        ],
        out_specs=pl.BlockSpec((tm, D_MODEL), lambda i: (i, 0)),
        out_shape=jax.ShapeDtypeStruct((n, D_MODEL), F32),
        scratch_shapes=[pltpu.VMEM((tm, D_FF), BF16)],
        compiler_params=pltpu.CompilerParams(
            dimension_semantics=("arbitrary",), vmem_limit_bytes=_vmem_limit(est)),
        name="ffn_final" if final_norm else "ffn",
    )(x2d, norm.reshape(1, D_MODEL), w_gu, w_down, fin.reshape(1, D_MODEL))


def _split3(x):
    hi = x.astype(BF16)
    r = x - hi.astype(F32)
    mid = r.astype(BF16)
    lo = (r - mid.astype(F32)).astype(BF16)
    return hi, mid, lo


def _hgrn2_masks():
    t = lax.broadcasted_iota(jnp.int32, (CHUNK, CHUNK), 0)
    s = lax.broadcasted_iota(jnp.int32, (CHUNK, CHUNK), 1)
    levels = []
    blk = CHUNK
    while blk > DIAG:
        half = blk // 2
        levels.append((t // blk == s // blk) & (t % blk >= half) & (s % blk < half))
        blk = half
    diag = [(s == t - d) & (t % DIAG >= d) for d in range(DIAG)]
    tri = (s <= t)
    return levels, diag, tri


def _mix_kernel(x_ref, nrm_ref, win_ref, lbl_ref, hgn_ref, cw_ref, cn_ref, wout_ref, o_ref,
                p_a, p_b, mix_a, mix_b, h_a, h_b, st_ref, ybuf_ref):
    n_macro = MIX_ROWS // MACRO

    @pl.when(pl.program_id(1) == 0)
    def _():
        st_ref[...] = jnp.zeros_like(st_ref)
        ybuf_ref[0:SUB, :] = jnp.zeros((SUB, CONV_WIDTH), F32)

    mix_b[...] = jnp.zeros((MACRO, D_MODEL), BF16)

    lbl = lbl_ref[...]
    lmax = jnp.max(lbl, axis=0, keepdims=True)
    lexp = jnp.exp(lbl - lmax)
    lb_all = lexp[0:1, :] / jnp.sum(lexp, axis=0, keepdims=True)

    level_masks, diag_masks, tri = _hgrn2_masks()
    tri_bf = jnp.where(tri, 1.0, 0.0).astype(BF16)
    tri3_bf = jnp.concatenate([tri_bf] * 3, axis=1)
    hgn = hgn_ref[...]
    nrm = nrm_ref[...]
    cw = cw_ref[...]
    cn = cn_ref[...]
    cb = 4 * HG_WIDTH

    def macro_rows(m):
        return pl.ds(pl.multiple_of(m * MACRO, MACRO), MACRO)

    def in_norm(m, h_ref):
        h_ref[...] = _rmsnorm(x_ref[0, macro_rows(m), :], nrm).astype(BF16)

    def in_proj_piece(h_ref, p_ref, c):
        cols = slice(c * PIECE, (c + 1) * PIECE)
        p_ref[:, cols] = _dot(h_ref[...], win_ref[:, cols])

    def out_proj_piece(m, mix_ref, c):
        rows = macro_rows(m)
        cols = slice(c * PIECE, (c + 1) * PIECE)
        o_ref[0, rows, cols] = x_ref[0, rows, cols] + _dot(mix_ref[...], wout_ref[:, cols])


    def hg_a(p_ref, rows, hh):
        c0 = hh * HG_DK
        q = p_ref[rows, pl.ds(c0, HG_DK)]
        fz = p_ref[rows, pl.ds(HG_WIDTH + c0, HG_DK)]
        v = p_ref[rows, pl.ds(2 * HG_WIDTH + c0, HG_DK)]
        lb = lb_all[:, c0:c0 + HG_DK]
        f = lb + (1.0 - lb) * (1.0 / (1.0 + jnp.exp(-fz)))
        g = jnp.log(f)
        return dict(hh=hh, rows=rows, f=f, k=1.0 - f, qs=_silu(q), v_bf=v.astype(BF16),
                    g3=jnp.concatenate(_split3(g), axis=0))

    def hg_ma(u):
        u["b"] = _dot(tri3_bf, u.pop("g3"))

    def hg_b(u):
        b, k, qs = u["b"], u["k"], u["qs"]
        b_end = b[CHUNK - 1:CHUNK, :]
        ops = []
        blk = CHUNK
        for _ in level_masks:
            half = blk // 2
            nb = CHUNK // blk
            ref = b.reshape(nb, blk, HG_DK)[:, half - 1:half, :]
            ref = jnp.broadcast_to(ref, (nb, blk, HG_DK)).reshape(CHUNK, HG_DK)
            z = jnp.exp(-jnp.abs(b - ref))
            ops.append(((qs * z).astype(BF16), (k * z).astype(BF16)))
            blk = half
        u["level_ops"] = ops
        f = u.pop("f")
        diag = [jnp.sum(qs * k, axis=-1, keepdims=True)]
        decay = f
        for d in range(1, DIAG):
            f_d = pltpu.roll(f, d, 0)
            diag.append(jnp.sum(qs * (1.0 - f_d) * decay, axis=-1, keepdims=True))
            decay = decay * f_d
        u["diag"] = diag
        u["qe"] = (qs * jnp.exp(b)).astype(BF16)
        u["ke"] = (k * jnp.exp(b_end - b)).astype(BF16)
        u["decay"] = jnp.exp(b_end)
        del u["b"], u["k"], u["qs"]

    def hg_mb(u):
        u["level_out"] = [_dot_nt(ql, kl) for ql, kl in u.pop("level_ops")]

    def hg_c(u):
        attn = jnp.zeros((CHUNK, CHUNK), F32)
        for lm, a in zip(level_masks, u.pop("level_out")):
            attn = jnp.where(lm, a, attn)
        for dm, rj in zip(diag_masks, u.pop("diag")):
            attn = jnp.where(dm, rj, attn)
        u["attn"] = attn.astype(BF16)

    def hg_mc(u):
        hh, v_bf = u["hh"], u.pop("v_bf")
        st = st_ref[hh]
        o = _dot(u.pop("attn"), v_bf)
        u["o"] = o + _dot_nt(u.pop("qe"), st.astype(BF16))
        st_ref[hh] = st * u.pop("decay") + _dot_tn(v_bf, u.pop("ke"))

    def hg_d(u, p_ref, mix_ref):
        c0 = u["hh"] * HG_DK
        gt = p_ref[u["rows"], pl.ds(3 * HG_WIDTH + c0, HG_DK)]
        on = _rmsnorm(u.pop("o"), hgn) * _silu(gt)
        mix_ref[u["rows"], pl.ds(c0, HG_DK)] = on.astype(BF16)

    def short_conv(p_ref, mix_ref, rows):
        y = p_ref[rows, cb + CONV_WIDTH:cb + 2 * CONV_WIDTH] * \
            p_ref[rows, cb + 2 * CONV_WIDTH:cb + 3 * CONV_WIDTH]
        ybuf_ref[SUB:SUB + CHUNK, :] = y
        y1 = ybuf_ref[SUB - 1:SUB - 1 + CHUNK, :]
        y2 = ybuf_ref[SUB - 2:SUB - 2 + CHUNK, :]
        conv = cw[0:1, :] * y2 + cw[1:2, :] * y1 + cw[2:3, :] * y
        ocv = _rmsnorm(p_ref[rows, cb:cb + CONV_WIDTH] * conv, cn)
        mix_ref[rows, HG_WIDTH:HG_WIDTH + CONV_WIDTH] = ocv.astype(BF16)
        ybuf_ref[0:SUB, :] = ybuf_ref[CHUNK:CHUNK + SUB, :]

    def stage(p_cur, mix_cur, h_in, p_in, m_out, mix_out, m_next, h_next):
        big = [functools.partial(in_proj_piece, h_in, p_in, c) for c in range(IN_COLS // PIECE)]
        big += [functools.partial(out_proj_piece, m_out, mix_out, c)
                for c in range(D_MODEL // PIECE)]
        c0, c1 = [pl.ds(j * CHUNK, CHUNK) for j in range(MACRO // CHUNK)]

        def each(phase, units, *args):
            for u in units:
                phase(u, *args)

        big[0](); big[1]()
        g0 = [hg_a(p_cur, c0, hh) for hh in range(HG_HEADS)]
        g1 = [hg_a(p_cur, c1, hh) for hh in range(HG_HEADS)]
        each(hg_ma, g0); big[2]()
        short_conv(p_cur, mix_cur, c0)
        each(hg_ma, g1); big[3]()
        each(hg_b, g0); each(hg_mb, g0); big[4]()
        each(hg_b, g1); each(hg_mb, g1); big[5]()
        each(hg_c, g0); each(hg_mc, g0); big[6]()
        each(hg_c, g1)
        in_norm(m_next, h_next)
        each(hg_mc, g1); big[7](); big[8]()
        each(hg_d, g0, p_cur, mix_cur)
        each(hg_d, g1, p_cur, mix_cur)
        short_conv(p_cur, mix_cur, c1)

    last = n_macro - 1
    in_norm(0, h_b)
    for c in range(IN_COLS // PIECE):
        in_proj_piece(h_b, p_a, c)
    in_norm(1, h_a)

    def trip(j, carry):
        m = 2 * j
        stage(p_a, mix_a, h_a, p_b, jnp.maximum(m - 1, 0), mix_b, jnp.minimum(m + 2, last), h_b)
        stage(p_b, mix_b, h_b, p_a, m, mix_a, jnp.minimum(m + 3, last), h_a)
        return carry

    lax.fori_loop(0, n_macro // 2, trip, 0)
    for c in range(D_MODEL // PIECE):
        out_proj_piece(last, mix_b, c)


def _mix(x, norm, w_in, lb_logits, hg_norm, conv_w, conv_norm, w_out):
    bsz, t, _ = x.shape
    tt = MIX_ROWS
    est = (4 * tt * D_MODEL * 4 + w_in.size * 2 + w_out.size * 2 + 2 * MACRO * IN_COLS * 4
           + HG_HEADS * HG_DK * HG_DK * 4 + (CHUNK + SUB) * CONV_WIDTH * 4
           + 4 * MACRO * D_MODEL * 2)
    return pl.pallas_call(
        _mix_kernel,
        grid=(bsz, t // tt),
        in_specs=[
            pl.BlockSpec((1, tt, D_MODEL), lambda b, i: (b, i, 0)),
            _resident((1, D_MODEL)),
            _resident((D_MODEL, IN_COLS)),
            _resident(lb_logits.shape),
            _resident((1, HG_DK)),
            _resident((CONV_K, CONV_WIDTH)),
            _resident((1, CONV_WIDTH)),
            _resident((D_MODEL, D_MODEL)),
        ],
        out_specs=pl.BlockSpec((1, tt, D_MODEL), lambda b, i: (b, i, 0)),
        out_shape=jax.ShapeDtypeStruct(x.shape, F32),
        scratch_shapes=[
            pltpu.VMEM((MACRO, IN_COLS), F32),
            pltpu.VMEM((MACRO, IN_COLS), F32),
            pltpu.VMEM((MACRO, D_MODEL), BF16),
            pltpu.VMEM((MACRO, D_MODEL), BF16),
            pltpu.VMEM((MACRO, D_MODEL), BF16),
            pltpu.VMEM((MACRO, D_MODEL), BF16),
            pltpu.VMEM((HG_HEADS, HG_DK, HG_DK), F32),
            pltpu.VMEM((CHUNK + SUB, CONV_WIDTH), F32),
        ],
        compiler_params=pltpu.CompilerParams(
            dimension_semantics=("arbitrary", "arbitrary"), vmem_limit_bytes=_vmem_limit(est)),
        name="mix",
    )(x, norm.reshape(1, D_MODEL), w_in, lb_logits, hg_norm.reshape(1, HG_DK), conv_w,
      conv_norm.reshape(1, CONV_WIDTH), w_out)


def _kv_kernel(m_ref, g_ref, w_ref, o_ref):
    h = _rmsnorm(m_ref[...], g_ref[...]).astype(BF16)
    o_ref[...] = _dot(h, w_ref[...]).astype(BF16)


def _kv(mem2d, norm, w_kv):
    n = mem2d.shape[0]
    est = 2 * KV_ROWS * D_MODEL * 4 + w_kv.size * 2 + 2 * KV_ROWS * 2 * D_MODEL * 2
    return pl.pallas_call(
        _kv_kernel,
        grid=(n // KV_ROWS,),
        in_specs=[
            pl.BlockSpec((KV_ROWS, D_MODEL), lambda i: (i, 0)),
            _resident((1, D_MODEL)),
            _resident((D_MODEL, 2 * D_MODEL)),
        ],
        out_specs=pl.BlockSpec((KV_ROWS, 2 * D_MODEL), lambda i: (i, 0)),
        out_shape=jax.ShapeDtypeStruct((n, 2 * D_MODEL), BF16),
        compiler_params=pltpu.CompilerParams(
            dimension_semantics=("arbitrary",), vmem_limit_bytes=_vmem_limit(est)),
        name="kv",
    )(mem2d, norm.reshape(1, D_MODEL), w_kv)


def _xattn_kernel(x_ref, g_ref, wq_ref, kv_ref, wo_ref, o_ref, h_ref, q_ref, s_ref, p_ref,
                  att_ref):
    n_sub = XA_ROWS // XA_SUB
    scale = 1.0 / math.sqrt(XA_HEAD_DIM)
    gain = g_ref[...]
    heads = [slice(hh * XA_HEAD_DIM, (hh + 1) * XA_HEAD_DIM) for hh in range(XA_HEADS)]

    def rows(i):
        return slice(i * XA_SUB, (i + 1) * XA_SUB)

    def norm(i):
        h_ref[rows(i), :] = _rmsnorm(x_ref[0, rows(i), :], gain).astype(BF16)

    def q_proj(i):
        for hd in heads:
            q_ref[rows(i), hd] = (_dot(h_ref[rows(i), :], wq_ref[:, hd]) * scale).astype(BF16)

    def scores(i):
        for hd in heads:
            s_ref[i % 2, :, hd] = _dot_nt(q_ref[rows(i), hd], kv_ref[0, :, hd])

    def softmax(i):
        for hd in heads:
            s = s_ref[i % 2, :, hd]
            e = jnp.exp(s - jnp.max(s, axis=-1, keepdims=True))
            p_ref[i % 2, :, hd] = (e * (1.0 / jnp.sum(e, axis=-1, keepdims=True))).astype(BF16)

    def pv(i):
        for hh, hd in enumerate(heads):
            v = kv_ref[0, :, D_MODEL + hh * XA_HEAD_DIM:D_MODEL + (hh + 1) * XA_HEAD_DIM]
            att_ref[rows(i), hd] = _dot(p_ref[i % 2, :, hd], v).astype(BF16)

    def out_proj(i):
        o_ref[0, rows(i), :] = x_ref[0, rows(i), :] + _dot(att_ref[rows(i), :], wo_ref[...])

    norm(0)
    if n_sub > 1:
        norm(1)
    q_proj(0)
    scores(0)
    for i in range(n_sub):
        if i + 1 < n_sub:
            q_proj(i + 1)
        softmax(i)
        pv(i)
        if i + 1 < n_sub:
            scores(i + 1)
        if i + 2 < n_sub:
            norm(i + 2)
        out_proj(i)


def _xattn(x, norm, w_q, kv, w_o):
    bsz, t, _ = x.shape
    tm = XA_ROWS
    est = (4 * tm * D_MODEL * 4 + w_q.size * 2 + w_o.size * 2 + 2 * N_MEM * 2 * D_MODEL * 2
           + 3 * tm * D_MODEL * 2 + 2 * XA_SUB * D_MODEL * (4 + 2))
    return pl.pallas_call(
        _xattn_kernel,
        grid=(bsz, t // tm),
        in_specs=[
            pl.BlockSpec((1, tm, D_MODEL), lambda b, i: (b, i, 0)),
            _resident((1, D_MODEL)),
            _resident((D_MODEL, D_MODEL)),
            pl.BlockSpec((1, N_MEM, 2 * D_MODEL), lambda b, i: (b, 0, 0)),
            _resident((D_MODEL, D_MODEL)),
        ],
        out_specs=pl.BlockSpec((1, tm, D_MODEL), lambda b, i: (b, i, 0)),
        out_shape=jax.ShapeDtypeStruct(x.shape, F32),
        scratch_shapes=[
            pltpu.VMEM((tm, D_MODEL), BF16),
            pltpu.VMEM((tm, D_MODEL), BF16),
            pltpu.VMEM((2, XA_SUB, D_MODEL), F32),
            pltpu.VMEM((2, XA_SUB, D_MODEL), BF16),
            pltpu.VMEM((tm, D_MODEL), BF16),
        ],
        compiler_params=pltpu.CompilerParams(
            dimension_semantics=("arbitrary", "arbitrary"), vmem_limit_bytes=_vmem_limit(est)),
        name="xattn",
    )(x, norm.reshape(1, D_MODEL), w_q, kv, w_o)


def kernel(x, mem, ffn1_norm, ffn1_w_gu, ffn1_w_down, mix_norm, w_in, lb_logits, hg_norm, conv_w,
           conv_norm, w_out, xa_norm, mem_norm, xa_wq, xa_wkv, xa_wo, ffn2_norm, ffn2_w_gu,
           ffn2_w_down, final_norm):
    bsz, t, d = x.shape
    assert d == D_MODEL and t % MIX_ROWS == 0 and (bsz * t) % FFN_ROWS == 0
    assert ffn1_norm.shape[0] == 1, "single-layer block"
    bf = lambda w: w[0].astype(BF16)

    x1 = _ffn(x.reshape(bsz * t, d), ffn1_norm[0], bf(ffn1_w_gu), bf(ffn1_w_down), final_norm,
              final_norm=False)
    x2 = _mix(x1.reshape(bsz, t, d), mix_norm[0], bf(w_in), lb_logits, hg_norm[0], conv_w[0],
              conv_norm[0], bf(w_out))
    kv = _kv(mem.reshape(bsz * N_MEM, d), mem_norm[0], bf(xa_wkv)).reshape(bsz, N_MEM, 2 * d)
    x3 = _xattn(x2, xa_norm[0], bf(xa_wq), kv, bf(xa_wo))
    y = _ffn(x3.reshape(bsz * t, d), ffn2_norm[0], bf(ffn2_w_gu), bf(ffn2_w_down), final_norm,
             final_norm=True)
    return y.reshape(bsz, t, d)
```

```python
import functools
import math

import jax
import jax.numpy as jnp
from jax import lax
from jax.experimental import pallas as pl
from jax.experimental.pallas import tpu as pltpu

D_MODEL = 1024
CHUNK = 64
SUB = 8
DIAG = 4
HG_WIDTH = 512
CONV_WIDTH = 512
HG_HEADS = 4
HG_DK = 128
CONV_K = 3
N_MEM = 256
XA_HEADS = 4
XA_HEAD_DIM = 256
D_FF = 2816
IN_COLS = 4 * HG_WIDTH + 3 * CONV_WIDTH
EPS = 1e-6

V7X_VMEM_BYTES = 64 * 1024 * 1024
MIB = 1024 * 1024

FFN_ROWS = 512
FFN_COLS = 256
MIX_ROWS = 1024
MACRO = 128
PIECE = 512
XA_ROWS = 1024
XA_SUB = 256
KV_ROWS = 256

BF16 = jnp.bfloat16
F32 = jnp.float32


def _vmem_limit(estimate_bytes):
    return int(min(estimate_bytes * 5 // 4 + 8 * MIB, V7X_VMEM_BYTES - 6 * MIB))


def _resident(shape):
    nd = len(shape)
    return pl.BlockSpec(shape, lambda *_: (0,) * nd, pipeline_mode=pl.Buffered(1))


def _rmsnorm(x, g):
    ms = jnp.mean(x * x, axis=-1, keepdims=True)
    return x * lax.rsqrt(ms + EPS) * g


def _silu(x):
    return x * (1.0 / (1.0 + jnp.exp(-x)))


def _dot(a, b):
    return jnp.dot(a, b, preferred_element_type=F32)


def _dot_nt(a, b):
    return lax.dot_general(a, b, (((1,), (1,)), ((), ())), preferred_element_type=F32)


def _dot_tn(a, b):
    return lax.dot_general(a, b, (((0,), (0,)), ((), ())), preferred_element_type=F32)


def _ffn_kernel(x_ref, g_ref, wgu_ref, wd_ref, fin_ref, o_ref, act_ref, *, final_norm):
    x = x_ref[...]
    h = _rmsnorm(x, g_ref[...]).astype(BF16)
    for c in range(D_FF // FFN_COLS):
        lo = c * FFN_COLS
        gate = _dot(h, wgu_ref[:, lo:lo + FFN_COLS])
        up = _dot(h, wgu_ref[:, D_FF + lo:D_FF + lo + FFN_COLS])
        act_ref[:, lo:lo + FFN_COLS] = (_silu(gate) * up).astype(BF16)
    y = x + 0.5 * _dot(act_ref[...], wd_ref[...])
    if final_norm:
        y = _rmsnorm(y, fin_ref[...])
    o_ref[...] = y


def _ffn(x2d, norm, w_gu, w_down, fin, *, final_norm):
    n = x2d.shape[0]
    tm = FFN_ROWS
    est = (4 * tm * D_MODEL * 4 + w_gu.size * 2 + w_down.size * 2 + tm * D_FF * 2
           + 4 * tm * FFN_COLS * 4)
    return pl.pallas_call(
        functools.partial(_ffn_kernel, final_norm=final_norm),
        grid=(n // tm,),
        in_specs=[
            pl.BlockSpec((tm, D_MODEL), lambda i: (i, 0)),
            _resident((1, D_MODEL)),
            _resident((D_MODEL, 2 * D_FF)),
            _resident((D_FF, D_MODEL)),
            _resident((1, D_MODEL)),
        ],
        out_specs=pl.BlockSpec((tm, D_MODEL), lambda i: (i, 0)),
        out_shape=jax.ShapeDtypeStruct((n, D_MODEL), F32),
        scratch_shapes=[pltpu.VMEM((tm, D_FF), BF16)],
        compiler_params=pltpu.CompilerParams(
            dimension_semantics=("arbitrary",), vmem_limit_bytes=_vmem_limit(est)),
        name="ffn_final" if final_norm else "ffn",
    )(x2d, norm.reshape(1, D_MODEL), w_gu, w_down, fin.reshape(1, D_MODEL))


def _chunk_cumsum(g):
    n = g.shape[-1]
    nb = CHUNK // SUB
    row = lax.broadcasted_iota(jnp.int32, (nb, SUB, n), 1)
    s = g.reshape(nb, SUB, n)
    for shift in (1, 2, 4):
        s = s + jnp.where(row >= shift, pltpu.roll(s, shift, 1), 0.0)
    carry = [jnp.zeros((1, 1, n), F32)]
    for i in range(1, nb):
        carry.append(carry[-1] + s[i - 1:i, SUB - 1:SUB, :])
    return (s + jnp.concatenate(carry, axis=0)).reshape(CHUNK, n)


def _hgrn2_masks():
    t = lax.broadcasted_iota(jnp.int32, (CHUNK, CHUNK), 0)
    s = lax.broadcasted_iota(jnp.int32, (CHUNK, CHUNK), 1)
    levels = []
    blk = CHUNK
    while blk > DIAG:
        half = blk // 2
        levels.append((t // blk == s // blk) & (t % blk >= half) & (s % blk < half))
        blk = half
    diag = [(s == t - d) & (t % DIAG >= d) for d in range(DIAG)]
    return levels, diag


def _mix_kernel(x_ref, nrm_ref, win_ref, lbl_ref, hgn_ref, cw_ref, cn_ref, wout_ref, o_ref,
                p_a, p_b, mix_a, mix_b, h_a, h_b, st_ref, ybuf_ref):
    n_macro = MIX_ROWS // MACRO

    @pl.when(pl.program_id(1) == 0)
    def _():
        st_ref[...] = jnp.zeros_like(st_ref)
        ybuf_ref[0:SUB, :] = jnp.zeros((SUB, CONV_WIDTH), F32)

    mix_b[...] = jnp.zeros((MACRO, D_MODEL), BF16)

    lbl = lbl_ref[...]
    lmax = jnp.max(lbl, axis=0, keepdims=True)
    lexp = jnp.exp(lbl - lmax)
    lb_all = lexp[0:1, :] / jnp.sum(lexp, axis=0, keepdims=True)

    level_masks, diag_masks = _hgrn2_masks()
    hgn = hgn_ref[...]
    nrm = nrm_ref[...]
    cw = cw_ref[...]
    cn = cn_ref[...]
    cb = 4 * HG_WIDTH

    def macro_rows(m):
        return pl.ds(pl.multiple_of(m * MACRO, MACRO), MACRO)

    def in_norm(m, h_ref):
        h_ref[...] = _rmsnorm(x_ref[0, macro_rows(m), :], nrm).astype(BF16)

    def in_proj_piece(h_ref, p_ref, c):
        cols = slice(c * PIECE, (c + 1) * PIECE)
        p_ref[:, cols] = _dot(h_ref[...], win_ref[:, cols])

    def out_proj_piece(m, mix_ref, c):
        rows = macro_rows(m)
        cols = slice(c * PIECE, (c + 1) * PIECE)
        o_ref[0, rows, cols] = x_ref[0, rows, cols] + _dot(mix_ref[...], wout_ref[:, cols])


    def hg_ab(p_ref, rows, hh):
        c0 = hh * HG_DK
        q = p_ref[rows, pl.ds(c0, HG_DK)]
        fz = p_ref[rows, pl.ds(HG_WIDTH + c0, HG_DK)]
        v = p_ref[rows, pl.ds(2 * HG_WIDTH + c0, HG_DK)]
        gt = p_ref[rows, pl.ds(3 * HG_WIDTH + c0, HG_DK)]
        lb = lb_all[:, c0:c0 + HG_DK]
        f = lb + (1.0 - lb) * (1.0 / (1.0 + jnp.exp(-fz)))
        k = 1.0 - f
        qs = _silu(q)
        b = _chunk_cumsum(jnp.log(f))
        b_end = b[CHUNK - 1:CHUNK, :]
        u = dict(hh=hh, rows=rows, v_bf=v.astype(BF16), gate=_silu(gt))
        ops = []
        blk = CHUNK
        for _ in level_masks:
            half = blk // 2
            nb = CHUNK // blk
            ref = b.reshape(nb, blk, HG_DK)[:, half - 1:half, :]
            ref = jnp.broadcast_to(ref, (nb, blk, HG_DK)).reshape(CHUNK, HG_DK)
            z = jnp.exp(-jnp.abs(b - ref))
            ops.append(((qs * z).astype(BF16), (k * z).astype(BF16)))
            blk = half
        u["level_ops"] = ops
        diag = [jnp.sum(qs * k, axis=-1, keepdims=True)]
        decay = f
        for d in range(1, DIAG):
            f_d = pltpu.roll(f, d, 0)
            diag.append(jnp.sum(qs * (1.0 - f_d) * decay, axis=-1, keepdims=True))
            decay = decay * f_d
        u["diag"] = diag
        u["qe"] = (qs * jnp.exp(b)).astype(BF16)
        u["ke"] = (k * jnp.exp(b_end - b)).astype(BF16)
        u["decay"] = jnp.exp(b_end)
        return u

    def hg_mb(u):
        u["level_out"] = [_dot_nt(ql, kl) for ql, kl in u.pop("level_ops")]

    def hg_c(u):
        attn = jnp.zeros((CHUNK, CHUNK), F32)
        for lm, a in zip(level_masks, u.pop("level_out")):
            attn = jnp.where(lm, a, attn)
        for dm, rj in zip(diag_masks, u.pop("diag")):
            attn = jnp.where(dm, rj, attn)
        u["attn"] = attn.astype(BF16)

    def hg_mc(u):
        hh, v_bf = u["hh"], u.pop("v_bf")
        st = st_ref[hh]
        o = _dot(u.pop("attn"), v_bf)
        u["o"] = o + _dot_nt(u.pop("qe"), st.astype(BF16))
        st_ref[hh] = st * u.pop("decay") + _dot_tn(v_bf, u.pop("ke"))

    def hg_d(u, mix_ref):
        on = _rmsnorm(u.pop("o"), hgn) * u.pop("gate")
        mix_ref[u["rows"], pl.ds(u["hh"] * HG_DK, HG_DK)] = on.astype(BF16)

    def short_conv(p_ref, mix_ref, rows):
        y = p_ref[rows, cb + CONV_WIDTH:cb + 2 * CONV_WIDTH] * \
            p_ref[rows, cb + 2 * CONV_WIDTH:cb + 3 * CONV_WIDTH]
        ybuf_ref[SUB:SUB + CHUNK, :] = y
        y1 = ybuf_ref[SUB - 1:SUB - 1 + CHUNK, :]
        y2 = ybuf_ref[SUB - 2:SUB - 2 + CHUNK, :]
        conv = cw[0:1, :] * y2 + cw[1:2, :] * y1 + cw[2:3, :] * y
        ocv = _rmsnorm(p_ref[rows, cb:cb + CONV_WIDTH] * conv, cn)
        mix_ref[rows, HG_WIDTH:HG_WIDTH + CONV_WIDTH] = ocv.astype(BF16)
        ybuf_ref[0:SUB, :] = ybuf_ref[CHUNK:CHUNK + SUB, :]

    def each(phase, units, *args):
        for u in units:
            phase(u, *args)

    c0, c1 = [pl.ds(j * CHUNK, CHUNK) for j in range(MACRO // CHUNK)]
    heads = range(HG_HEADS)
    last = n_macro - 1

    in_norm(0, h_b)
    for c in range(IN_COLS // PIECE):
        in_proj_piece(h_b, p_a, c)
    in_norm(1, h_a)

    def trip(j, carry):
        m = 2 * j
        big_a = [functools.partial(in_proj_piece, h_a, p_b, c) for c in range(IN_COLS // PIECE)]
        big_a += [functools.partial(out_proj_piece, jnp.maximum(m - 1, 0), mix_b, c)
                  for c in range(D_MODEL // PIECE)]
        big_b = [functools.partial(in_proj_piece, h_b, p_a, c) for c in range(IN_COLS // PIECE)]
        big_b += [functools.partial(out_proj_piece, m, mix_a, c) for c in range(D_MODEL // PIECE)]

        big_a[0](); big_a[1](); big_a[2](); big_a[3]()
        g0 = [hg_ab(p_a, c0, hh) for hh in heads]
        each(hg_mb, g0); big_a[4](); big_a[5]()
        g1 = [hg_ab(p_a, c1, hh) for hh in heads]
        short_conv(p_a, mix_a, c0)
        each(hg_c, g0)
        each(hg_mb, g1); each(hg_mc, g0); big_a[6](); big_a[7](); big_a[8]()
        g2 = [hg_ab(p_b, c0, hh) for hh in heads]
        short_conv(p_a, mix_a, c1)
        each(hg_c, g1)
        each(hg_d, g0, mix_a)
        in_norm(jnp.minimum(m + 2, last), h_b)
        each(hg_mb, g2); each(hg_mc, g1); big_b[0](); big_b[1](); big_b[2]()
        g3 = [hg_ab(p_b, c1, hh) for hh in heads]
        short_conv(p_b, mix_b, c0)
        each(hg_c, g2)
        each(hg_d, g1, mix_a)
        each(hg_mb, g3); each(hg_mc, g2); big_b[3](); big_b[4](); big_b[5]()
        short_conv(p_b, mix_b, c1)
        each(hg_c, g3)
        each(hg_d, g2, mix_b)
        in_norm(jnp.minimum(m + 3, last), h_a)
        each(hg_mc, g3); big_b[6](); big_b[7](); big_b[8]()
        each(hg_d, g3, mix_b)
        return carry

    lax.fori_loop(0, n_macro // 2, trip, 0)
    for c in range(D_MODEL // PIECE):
        out_proj_piece(last, mix_b, c)


def _mix(x, norm, w_in, lb_logits, hg_norm, conv_w, conv_norm, w_out):
    bsz, t, _ = x.shape
    tt = MIX_ROWS
    est = (4 * tt * D_MODEL * 4 + w_in.size * 2 + w_out.size * 2 + 2 * MACRO * IN_COLS * 4
           + HG_HEADS * HG_DK * HG_DK * 4 + (CHUNK + SUB) * CONV_WIDTH * 4
           + 4 * MACRO * D_MODEL * 2)
    return pl.pallas_call(
        _mix_kernel,
        grid=(bsz, t // tt),
        in_specs=[
            pl.BlockSpec((1, tt, D_MODEL), lambda b, i: (b, i, 0)),
            _resident((1, D_MODEL)),
            _resident((D_MODEL, IN_COLS)),
            _resident(lb_logits.shape),
            _resident((1, HG_DK)),
            _resident((CONV_K, CONV_WIDTH)),
            _resident((1, CONV_WIDTH)),
            _resident((D_MODEL, D_MODEL)),
        ],
        out_specs=pl.BlockSpec((1, tt, D_MODEL), lambda b, i: (b, i, 0)),
        out_shape=jax.ShapeDtypeStruct(x.shape, F32),
        scratch_shapes=[
            pltpu.VMEM((MACRO, IN_COLS), F32),
            pltpu.VMEM((MACRO, IN_COLS), F32),
            pltpu.VMEM((MACRO, D_MODEL), BF16),
            pltpu.VMEM((MACRO, D_MODEL), BF16),
            pltpu.VMEM((MACRO, D_MODEL), BF16),
            pltpu.VMEM((MACRO, D_MODEL), BF16),
            pltpu.VMEM((HG_HEADS, HG_DK, HG_DK), F32),
            pltpu.VMEM((CHUNK + SUB, CONV_WIDTH), F32),
        ],
        compiler_params=pltpu.CompilerParams(
            dimension_semantics=("arbitrary", "arbitrary"), vmem_limit_bytes=_vmem_limit(est)),
        name="mix",
    )(x, norm.reshape(1, D_MODEL), w_in, lb_logits, hg_norm.reshape(1, HG_DK), conv_w,
      conv_norm.reshape(1, CONV_WIDTH), w_out)


def _kv_kernel(m_ref, g_ref, w_ref, o_ref):
    h = _rmsnorm(m_ref[...], g_ref[...]).astype(BF16)
    o_ref[...] = _dot(h, w_ref[...]).astype(BF16)


def _kv(mem2d, norm, w_kv):
    n = mem2d.shape[0]
    est = 2 * KV_ROWS * D_MODEL * 4 + w_kv.size * 2 + 2 * KV_ROWS * 2 * D_MODEL * 2
    return pl.pallas_call(
        _kv_kernel,
        grid=(n // KV_ROWS,),
        in_specs=[
            pl.BlockSpec((KV_ROWS, D_MODEL), lambda i: (i, 0)),
            _resident((1, D_MODEL)),
            _resident((D_MODEL, 2 * D_MODEL)),
        ],
        out_specs=pl.BlockSpec((KV_ROWS, 2 * D_MODEL), lambda i: (i, 0)),
        out_shape=jax.ShapeDtypeStruct((n, 2 * D_MODEL), BF16),
        compiler_params=pltpu.CompilerParams(
            dimension_semantics=("arbitrary",), vmem_limit_bytes=_vmem_limit(est)),
        name="kv",
    )(mem2d, norm.reshape(1, D_MODEL), w_kv)


def _xattn_kernel(x_ref, g_ref, wq_ref, kv_ref, wo_ref, o_ref, h_ref, q_ref, s_ref, p_ref,
                  att_ref):
    n_sub = XA_ROWS // XA_SUB
    scale = 1.0 / math.sqrt(XA_HEAD_DIM)
    gain = g_ref[...]
    heads = [slice(hh * XA_HEAD_DIM, (hh + 1) * XA_HEAD_DIM) for hh in range(XA_HEADS)]

    def rows(i):
        return slice(i * XA_SUB, (i + 1) * XA_SUB)

    def norm(i):
        h_ref[rows(i), :] = _rmsnorm(x_ref[0, rows(i), :], gain).astype(BF16)

    def q_proj(i):
        for hd in heads:
            q_ref[rows(i), hd] = (_dot(h_ref[rows(i), :], wq_ref[:, hd]) * scale).astype(BF16)

    def scores(i):
        for hd in heads:
            s_ref[i % 2, :, hd] = _dot_nt(q_ref[rows(i), hd], kv_ref[0, :, hd])

    def softmax(i):
        for hd in heads:
            s = s_ref[i % 2, :, hd]
            e = jnp.exp(s - jnp.max(s, axis=-1, keepdims=True))
            p_ref[i % 2, :, hd] = (e * (1.0 / jnp.sum(e, axis=-1, keepdims=True))).astype(BF16)

    def pv(i):
        for hh, hd in enumerate(heads):
            v = kv_ref[0, :, D_MODEL + hh * XA_HEAD_DIM:D_MODEL + (hh + 1) * XA_HEAD_DIM]
            att_ref[rows(i), hd] = _dot(p_ref[i % 2, :, hd], v).astype(BF16)

    def out_proj(i):
        o_ref[0, rows(i), :] = x_ref[0, rows(i), :] + _dot(att_ref[rows(i), :], wo_ref[...])

    norm(0)
    if n_sub > 1:
        norm(1)
    q_proj(0)
    scores(0)
    for i in range(n_sub):
        if i + 1 < n_sub:
            q_proj(i + 1)
        softmax(i)
        pv(i)
        if i + 1 < n_sub:
            scores(i + 1)
        if i + 2 < n_sub:
            norm(i + 2)
        out_proj(i)


def _xattn(x, norm, w_q, kv, w_o):
    bsz, t, _ = x.shape
    tm = XA_ROWS
    est = (4 * tm * D_MODEL * 4 + w_q.size * 2 + w_o.size * 2 + 2 * N_MEM * 2 * D_MODEL * 2
           + 3 * tm * D_MODEL * 2 + 2 * XA_SUB * D_MODEL * (4 + 2))
    return pl.pallas_call(
        _xattn_kernel,
        grid=(bsz, t // tm),
        in_specs=[
            pl.BlockSpec((1, tm, D_MODEL), lambda b, i: (b, i, 0)),
            _resident((1, D_MODEL)),
            _resident((D_MODEL, D_MODEL)),
            pl.BlockSpec((1, N_MEM, 2 * D_MODEL), lambda b, i: (b, 0, 0)),
            _resident((D_MODEL, D_MODEL)),
        ],
        out_specs=pl.BlockSpec((1, tm, D_MODEL), lambda b, i: (b, i, 0)),
        out_shape=jax.ShapeDtypeStruct(x.shape, F32),
        scratch_shapes=[
            pltpu.VMEM((tm, D_MODEL), BF16),
            pltpu.VMEM((tm, D_MODEL), BF16),
            pltpu.VMEM((2, XA_SUB, D_MODEL), F32),
            pltpu.VMEM((2, XA_SUB, D_MODEL), BF16),
            pltpu.VMEM((tm, D_MODEL), BF16),
        ],
        compiler_params=pltpu.CompilerParams(
            dimension_semantics=("arbitrary", "arbitrary"), vmem_limit_bytes=_vmem_limit(est)),
        name="xattn",
    )(x, norm.reshape(1, D_MODEL), w_q, kv, w_o)


def kernel(x, mem, ffn1_norm, ffn1_w_gu, ffn1_w_down, mix_norm, w_in, lb_logits, hg_norm, conv_w,
           conv_norm, w_out, xa_norm, mem_norm, xa_wq, xa_wkv, xa_wo, ffn2_norm, ffn2_w_gu,
           ffn2_w_down, final_norm):
    bsz, t, d = x.shape
    assert d == D_MODEL and t % MIX_ROWS == 0 and (bsz * t) % FFN_ROWS == 0
    assert ffn1_norm.shape[0] == 1, "single-layer block"
    bf = lambda w: w[0].astype(BF16)

    x1 = _ffn(x.reshape(bsz * t, d), ffn1_norm[0], bf(ffn1_w_gu), bf(ffn1_w_down), final_norm,
              final_norm=False)
    x2 = _mix(x1.reshape(bsz, t, d), mix_norm[0], bf(w_in), lb_logits, hg_norm[0], conv_w[0],
              conv_norm[0], bf(w_out))
    kv = _kv(mem.reshape(bsz * N_MEM, d), mem_norm[0], bf(xa_wkv)).reshape(bsz, N_MEM, 2 * d)
    x3 = _xattn(x2, xa_norm[0], bf(xa_wq), kv, bf(xa_wo))
    y = _ffn(x3.reshape(bsz * t, d), ffn2_norm[0], bf(ffn2_w_gu), bf(ffn2_w_down), final_norm,
             final_norm=True)
    return y.reshape(bsz, t, d)
```

```python
import functools
import math

import jax
import jax.numpy as jnp
from jax import lax
from jax.experimental import pallas as pl
from jax.experimental.pallas import tpu as pltpu

D_MODEL = 1024
CHUNK = 64
SUB = 8
DIAG = 4
HG_WIDTH = 512
CONV_WIDTH = 512
HG_HEADS = 4
HG_DK = 128
CONV_K = 3
N_MEM = 256
XA_HEADS = 4
XA_HEAD_DIM = 256
D_FF = 2816
IN_COLS = 4 * HG_WIDTH + 3 * CONV_WIDTH
EPS = 1e-6

V7X_VMEM_BYTES = 64 * 1024 * 1024
MIB = 1024 * 1024

FFN_ROWS = 1024
FFN_SUB = 256
FFN_COLS = 256
MIX_ROWS = 1024
MACRO = 128
PIECE = 512
XA_ROWS = 1024
XA_SUB = 256
KV_ROWS = 256

BF16 = jnp.bfloat16
F32 = jnp.float32


def _vmem_limit(estimate_bytes):
    return int(min(estimate_bytes * 5 // 4 + 8 * MIB, V7X_VMEM_BYTES - 6 * MIB))


def _resident(shape):
    nd = len(shape)
    return pl.BlockSpec(shape, lambda *_: (0,) * nd, pipeline_mode=pl.Buffered(1))


def _rmsnorm(x, g):
    ms = jnp.mean(x * x, axis=-1, keepdims=True)
    return x * lax.rsqrt(ms + EPS) * g


def _silu(x):
    return x * (1.0 / (1.0 + jnp.exp(-x)))


def _dot(a, b):
    return jnp.dot(a, b, preferred_element_type=F32)


def _dot_nt(a, b):
    return lax.dot_general(a, b, (((1,), (1,)), ((), ())), preferred_element_type=F32)


def _dot_tn(a, b):
    return lax.dot_general(a, b, (((0,), (0,)), ((), ())), preferred_element_type=F32)


def _ffn_kernel(x_ref, g_ref, wgu_ref, wd_ref, fin_ref, o_ref, h_ref, act_ref, *, final_norm):
    n_sub = FFN_ROWS // FFN_SUB
    gain = g_ref[...]

    def rows(i):
        return slice(i * FFN_SUB, (i + 1) * FFN_SUB)

    def norm(i):
        h_ref[rows(i), :] = _rmsnorm(x_ref[rows(i), :], gain).astype(BF16)

    def gate_up(i):
        h = h_ref[rows(i), :]
        for c in range(D_FF // FFN_COLS):
            lo = c * FFN_COLS
            gate = _dot(h, wgu_ref[:, lo:lo + FFN_COLS])
            up = _dot(h, wgu_ref[:, D_FF + lo:D_FF + lo + FFN_COLS])
            act_ref[rows(i), lo:lo + FFN_COLS] = (_silu(gate) * up).astype(BF16)

    def down(i):
        y = x_ref[rows(i), :] + 0.5 * _dot(act_ref[rows(i), :], wd_ref[...])
        if final_norm:
            y = _rmsnorm(y, fin_ref[...])
        o_ref[rows(i), :] = y

    norm(0)
    for i in range(n_sub):
        if i + 1 < n_sub:
            norm(i + 1)
        gate_up(i)
        if i >= 1:
            down(i - 1)
    down(n_sub - 1)


def _ffn(x2d, norm, w_gu, w_down, fin, *, final_norm):
    n = x2d.shape[0]
    tm = FFN_ROWS
    est = (4 * tm * D_MODEL * 4 + w_gu.size * 2 + w_down.size * 2 + tm * D_FF * 2
           + tm * D_MODEL * 2 + 4 * FFN_SUB * FFN_COLS * 4)
    return pl.pallas_call(
        functools.partial(_ffn_kernel, final_norm=final_norm),
        grid=(n // tm,),
        in_specs=[
            pl.BlockSpec((tm, D_MODEL), lambda i: (i, 0)),
            _resident((1, D_MODEL)),
            _resident((D_MODEL, 2 * D_FF)),
            _resident((D_FF, D_MODEL)),
            _resident((1, D_MODEL)),
        ],
        out_specs=pl.BlockSpec((tm, D_MODEL), lambda i: (i, 0)),
        out_shape=jax.ShapeDtypeStruct((n, D_MODEL), F32),
        scratch_shapes=[pltpu.VMEM((tm, D_MODEL), BF16), pltpu.VMEM((tm, D_FF), BF16)],
        compiler_params=pltpu.CompilerParams(
            dimension_semantics=("arbitrary",), vmem_limit_bytes=_vmem_limit(est)),
        name="ffn_final" if final_norm else "ffn",
    )(x2d, norm.reshape(1, D_MODEL), w_gu, w_down, fin.reshape(1, D_MODEL))


def _chunk_cumsum(g):
    n = g.shape[-1]
    nb = CHUNK // SUB
    row = lax.broadcasted_iota(jnp.int32, (nb, SUB, n), 1)
    s = g.reshape(nb, SUB, n)
    for shift in (1, 2, 4):
        s = s + jnp.where(row >= shift, pltpu.roll(s, shift, 1), 0.0)
    carry = [jnp.zeros((1, 1, n), F32)]
    for i in range(1, nb):
        carry.append(carry[-1] + s[i - 1:i, SUB - 1:SUB, :])
    return (s + jnp.concatenate(carry, axis=0)).reshape(CHUNK, n)


def _hgrn2_masks():
    t = lax.broadcasted_iota(jnp.int32, (CHUNK, CHUNK), 0)
    s = lax.broadcasted_iota(jnp.int32, (CHUNK, CHUNK), 1)
    levels = []
    blk = CHUNK
    while blk > DIAG:
        half = blk // 2
        levels.append((t // blk == s // blk) & (t % blk >= half) & (s % blk < half))
        blk = half
    diag = [(s == t - d) & (t % DIAG >= d) for d in range(DIAG)]
    return levels, diag


def _mix_kernel(x_ref, nrm_ref, win_ref, lbl_ref, hgn_ref, cw_ref, cn_ref, wout_ref, o_ref,
                p_a, p_b, mix_a, mix_b, h_a, h_b, st_ref, ybuf_ref):
    n_macro = MIX_ROWS // MACRO

    @pl.when(pl.program_id(1) == 0)
    def _():
        st_ref[...] = jnp.zeros_like(st_ref)
        ybuf_ref[0:SUB, :] = jnp.zeros((SUB, CONV_WIDTH), F32)

    mix_b[...] = jnp.zeros((MACRO, D_MODEL), BF16)

    lbl = lbl_ref[...]
    lmax = jnp.max(lbl, axis=0, keepdims=True)
    lexp = jnp.exp(lbl - lmax)
    lb_all = lexp[0:1, :] / jnp.sum(lexp, axis=0, keepdims=True)

    level_masks, diag_masks = _hgrn2_masks()
    hgn = hgn_ref[...]
    nrm = nrm_ref[...]
    cw = cw_ref[...]
    cn = cn_ref[...]
    cb = 4 * HG_WIDTH

    def macro_rows(m):
        return pl.ds(pl.multiple_of(m * MACRO, MACRO), MACRO)

    def in_norm(m, h_ref):
        h_ref[...] = _rmsnorm(x_ref[0, macro_rows(m), :], nrm).astype(BF16)

    def in_proj_piece(h_ref, p_ref, c):
        cols = slice(c * PIECE, (c + 1) * PIECE)
        p_ref[:, cols] = _dot(h_ref[...], win_ref[:, cols])

    def out_proj_piece(m, mix_ref, c):
        rows = macro_rows(m)
        cols = slice(c * PIECE, (c + 1) * PIECE)
        o_ref[0, rows, cols] = x_ref[0, rows, cols] + _dot(mix_ref[...], wout_ref[:, cols])


    def hg_ab(p_ref, rows, hh):
        c0 = hh * HG_DK
        q = p_ref[rows, pl.ds(c0, HG_DK)]
        fz = p_ref[rows, pl.ds(HG_WIDTH + c0, HG_DK)]
        v = p_ref[rows, pl.ds(2 * HG_WIDTH + c0, HG_DK)]
        gt = p_ref[rows, pl.ds(3 * HG_WIDTH + c0, HG_DK)]
        lb = lb_all[:, c0:c0 + HG_DK]
        f = lb + (1.0 - lb) * (1.0 / (1.0 + jnp.exp(-fz)))
        k = 1.0 - f
        qs = _silu(q)
        b = _chunk_cumsum(jnp.log2(f))
        b_end = b[CHUNK - 1:CHUNK, :]
        u = dict(hh=hh, rows=rows, v_bf=v.astype(BF16), gate=_silu(gt))
        q_bf, k_bf = qs.astype(BF16), k.astype(BF16)
        ops = []
        blk = CHUNK
        for _ in level_masks:
            half = blk // 2
            nb = CHUNK // blk
            ref = b.reshape(nb, blk, HG_DK)[:, half - 1:half, :]
            ref = jnp.broadcast_to(ref, (nb, blk, HG_DK)).reshape(CHUNK, HG_DK)
            z = jnp.exp2(-jnp.abs(b - ref)).astype(BF16)
            ops.append((q_bf * z, k_bf * z))
            blk = half
        u["level_ops"] = ops
        diag = [jnp.sum(qs * k, axis=-1, keepdims=True)]
        f3 = f.reshape(CHUNK // SUB, SUB, HG_DK)
        decay = f
        for d in range(1, DIAG):
            f_d = pltpu.roll(f3, d, 1).reshape(CHUNK, HG_DK)
            diag.append(jnp.sum(qs * (1.0 - f_d) * decay, axis=-1, keepdims=True))
            decay = decay * f_d
        u["diag"] = diag
        u["qe"] = (qs * jnp.exp2(b)).astype(BF16)
        u["ke"] = (k * jnp.exp2(b_end - b)).astype(BF16)
        u["decay"] = jnp.exp2(b_end)
        return u

    def hg_mb(u):
        u["level_out"] = [_dot_nt(ql, kl) for ql, kl in u.pop("level_ops")]

    def hg_c(u):
        attn = jnp.zeros((CHUNK, CHUNK), F32)
        for lm, a in zip(level_masks, u.pop("level_out")):
            attn = jnp.where(lm, a, attn)
        for dm, rj in zip(diag_masks, u.pop("diag")):
            attn = jnp.where(dm, rj, attn)
        u["attn"] = attn.astype(BF16)

    def hg_mc(u):
        hh, v_bf = u["hh"], u.pop("v_bf")
        st = st_ref[hh]
        o = _dot(u.pop("attn"), v_bf)
        u["o"] = o + _dot_nt(u.pop("qe"), st.astype(BF16))
        st_ref[hh] = st * u.pop("decay") + _dot_tn(v_bf, u.pop("ke"))

    def hg_d(u, mix_ref):
        on = _rmsnorm(u.pop("o"), hgn) * u.pop("gate")
        mix_ref[u["rows"], pl.ds(u["hh"] * HG_DK, HG_DK)] = on.astype(BF16)

    def short_conv(p_ref, mix_ref, rows):
        y = p_ref[rows, cb + CONV_WIDTH:cb + 2 * CONV_WIDTH] * \
            p_ref[rows, cb + 2 * CONV_WIDTH:cb + 3 * CONV_WIDTH]
        ybuf_ref[SUB:SUB + CHUNK, :] = y
        y1 = ybuf_ref[SUB - 1:SUB - 1 + CHUNK, :]
        y2 = ybuf_ref[SUB - 2:SUB - 2 + CHUNK, :]
        conv = cw[0:1, :] * y2 + cw[1:2, :] * y1 + cw[2:3, :] * y
        ocv = _rmsnorm(p_ref[rows, cb:cb + CONV_WIDTH] * conv, cn)
        mix_ref[rows, HG_WIDTH:HG_WIDTH + CONV_WIDTH] = ocv.astype(BF16)
        ybuf_ref[0:SUB, :] = ybuf_ref[CHUNK:CHUNK + SUB, :]

    def each(phase, units, *args):
        for u in units:
            phase(u, *args)

    n_chunks = MACRO // CHUNK
    chunk_rows = [pl.ds(j * CHUNK, CHUNK) for j in range(n_chunks)]
    heads = range(HG_HEADS)
    last = n_macro - 1

    in_norm(0, h_b)
    for c in range(IN_COLS // PIECE):
        in_proj_piece(h_b, p_a, c)
    in_norm(1, h_a)

    def trip(j, carry):
        m = 2 * j
        big_a = [functools.partial(in_proj_piece, h_a, p_b, c) for c in range(IN_COLS // PIECE)]
        big_a += [functools.partial(out_proj_piece, jnp.maximum(m - 1, 0), mix_b, c)
                  for c in range(D_MODEL // PIECE)]
        big_b = [functools.partial(in_proj_piece, h_b, p_a, c) for c in range(IN_COLS // PIECE)]
        big_b += [functools.partial(out_proj_piece, m, mix_a, c) for c in range(D_MODEL // PIECE)]
        groups = [(p_a, mix_a, r) for r in chunk_rows] + [(p_b, mix_b, r) for r in chunk_rows]
        ng = len(groups)
        plan = {i: [] for i in range(-1, ng + 1)}
        for pieces, slots in ((big_a, range(-1, n_chunks + 1)), (big_b, range(n_chunks + 1, ng + 1))):
            for k, piece in enumerate(pieces):
                plan[slots[k * len(slots) // len(pieces)]].append(piece)

        def issue(slot):
            for piece in plan[slot]:
                piece()

        issue(-1)
        units = []
        for i in range(ng + 1):
            if i < ng:
                p_ref, _, rows = groups[i]
                units.append([hg_ab(p_ref, rows, hh) for hh in heads])
            if i >= 1:
                short_conv(*groups[i - 1])
                each(hg_c, units[i - 1])
            if i >= 2:
                each(hg_d, units[i - 2], groups[i - 2][1])
            if i == n_chunks - 1:
                in_norm(jnp.minimum(m + 2, last), h_b)
            if i == ng - 1:
                in_norm(jnp.minimum(m + 3, last), h_a)
            if i < ng:
                each(hg_mb, units[i])
            if i >= 1:
                each(hg_mc, units[i - 1])
            issue(i)
        each(hg_d, units[ng - 1], groups[ng - 1][1])
        return carry

    lax.fori_loop(0, n_macro // 2, trip, 0)
    for c in range(D_MODEL // PIECE):
        out_proj_piece(last, mix_b, c)


def _mix(x, norm, w_in, lb_logits, hg_norm, conv_w, conv_norm, w_out):
    bsz, t, _ = x.shape
    tt = MIX_ROWS
    est = (4 * tt * D_MODEL * 4 + w_in.size * 2 + w_out.size * 2 + 2 * MACRO * IN_COLS * 4
           + HG_HEADS * HG_DK * HG_DK * 4 + (CHUNK + SUB) * CONV_WIDTH * 4
           + 4 * MACRO * D_MODEL * 2)
    return pl.pallas_call(
        _mix_kernel,
        grid=(bsz, t // tt),
        in_specs=[
            pl.BlockSpec((1, tt, D_MODEL), lambda b, i: (b, i, 0)),
            _resident((1, D_MODEL)),
            _resident((D_MODEL, IN_COLS)),
            _resident(lb_logits.shape),
            _resident((1, HG_DK)),
            _resident((CONV_K, CONV_WIDTH)),
            _resident((1, CONV_WIDTH)),
            _resident((D_MODEL, D_MODEL)),
        ],
        out_specs=pl.BlockSpec((1, tt, D_MODEL), lambda b, i: (b, i, 0)),
        out_shape=jax.ShapeDtypeStruct(x.shape, F32),
        scratch_shapes=[
            pltpu.VMEM((MACRO, IN_COLS), F32),
            pltpu.VMEM((MACRO, IN_COLS), F32),
            pltpu.VMEM((MACRO, D_MODEL), BF16),
            pltpu.VMEM((MACRO, D_MODEL), BF16),
            pltpu.VMEM((MACRO, D_MODEL), BF16),
            pltpu.VMEM((MACRO, D_MODEL), BF16),
            pltpu.VMEM((HG_HEADS, HG_DK, HG_DK), F32),
            pltpu.VMEM((CHUNK + SUB, CONV_WIDTH), F32),
        ],
        compiler_params=pltpu.CompilerParams(
            dimension_semantics=("arbitrary", "arbitrary"), vmem_limit_bytes=_vmem_limit(est)),
        name="mix",
    )(x, norm.reshape(1, D_MODEL), w_in, lb_logits, hg_norm.reshape(1, HG_DK), conv_w,
      conv_norm.reshape(1, CONV_WIDTH), w_out)


def _kv_kernel(m_ref, g_ref, w_ref, o_ref):
    h = _rmsnorm(m_ref[...], g_ref[...]).astype(BF16)
    o_ref[...] = _dot(h, w_ref[...]).astype(BF16)


def _kv(mem2d, norm, w_kv):
    n = mem2d.shape[0]
    est = 2 * KV_ROWS * D_MODEL * 4 + w_kv.size * 2 + 2 * KV_ROWS * 2 * D_MODEL * 2
    return pl.pallas_call(
        _kv_kernel,
        grid=(n // KV_ROWS,),
        in_specs=[
            pl.BlockSpec((KV_ROWS, D_MODEL), lambda i: (i, 0)),
            _resident((1, D_MODEL)),
            _resident((D_MODEL, 2 * D_MODEL)),
        ],
        out_specs=pl.BlockSpec((KV_ROWS, 2 * D_MODEL), lambda i: (i, 0)),
        out_shape=jax.ShapeDtypeStruct((n, 2 * D_MODEL), BF16),
        compiler_params=pltpu.CompilerParams(
            dimension_semantics=("arbitrary",), vmem_limit_bytes=_vmem_limit(est)),
        name="kv",
    )(mem2d, norm.reshape(1, D_MODEL), w_kv)


def _xattn_kernel(x_ref, g_ref, wq_ref, kv_ref, wo_ref, o_ref, h_ref, q_ref, s_ref, p_ref,
                  att_ref):
    n_sub = XA_ROWS // XA_SUB
    scale = 1.0 / math.sqrt(XA_HEAD_DIM)
    gain = g_ref[...]
    heads = [slice(hh * XA_HEAD_DIM, (hh + 1) * XA_HEAD_DIM) for hh in range(XA_HEADS)]

    def rows(i):
        return slice(i * XA_SUB, (i + 1) * XA_SUB)

    def norm(i):
        h_ref[rows(i), :] = _rmsnorm(x_ref[0, rows(i), :], gain).astype(BF16)

    def q_proj(i):
        for hd in heads:
            q_ref[rows(i), hd] = (_dot(h_ref[rows(i), :], wq_ref[:, hd]) * scale).astype(BF16)

    def scores(i):
        for hd in heads:
            s_ref[i % 2, :, hd] = _dot_nt(q_ref[rows(i), hd], kv_ref[0, :, hd])

    def softmax(i):
        for hd in heads:
            s = s_ref[i % 2, :, hd]
            e = jnp.exp(s - jnp.max(s, axis=-1, keepdims=True))
            p_ref[i % 2, :, hd] = (e * (1.0 / jnp.sum(e, axis=-1, keepdims=True))).astype(BF16)

    def pv(i):
        for hh, hd in enumerate(heads):
            v = kv_ref[0, :, D_MODEL + hh * XA_HEAD_DIM:D_MODEL + (hh + 1) * XA_HEAD_DIM]
            att_ref[rows(i), hd] = _dot(p_ref[i % 2, :, hd], v).astype(BF16)

    def out_proj(i):
        o_ref[0, rows(i), :] = x_ref[0, rows(i), :] + _dot(att_ref[rows(i), :], wo_ref[...])

    norm(0)
    if n_sub > 1:
        norm(1)
    q_proj(0)
    scores(0)
    for i in range(n_sub):
        if i + 1 < n_sub:
            q_proj(i + 1)
        softmax(i)
        pv(i)
        if i + 1 < n_sub:
            scores(i + 1)
        if i + 2 < n_sub:
            norm(i + 2)
        out_proj(i)


def _xattn(x, norm, w_q, kv, w_o):
    bsz, t, _ = x.shape
    tm = XA_ROWS
    est = (4 * tm * D_MODEL * 4 + w_q.size * 2 + w_o.size * 2 + 2 * N_MEM * 2 * D_MODEL * 2
           + 3 * tm * D_MODEL * 2 + 2 * XA_SUB * D_MODEL * (4 + 2))
    return pl.pallas_call(
        _xattn_kernel,
        grid=(bsz, t // tm),
        in_specs=[
            pl.BlockSpec((1, tm, D_MODEL), lambda b, i: (b, i, 0)),
            _resident((1, D_MODEL)),
            _resident((D_MODEL, D_MODEL)),
            pl.BlockSpec((1, N_MEM, 2 * D_MODEL), lambda b, i: (b, 0, 0)),
            _resident((D_MODEL, D_MODEL)),
        ],
        out_specs=pl.BlockSpec((1, tm, D_MODEL), lambda b, i: (b, i, 0)),
        out_shape=jax.ShapeDtypeStruct(x.shape, F32),
        scratch_shapes=[
            pltpu.VMEM((tm, D_MODEL), BF16),
            pltpu.VMEM((tm, D_MODEL), BF16),
            pltpu.VMEM((2, XA_SUB, D_MODEL), F32),
            pltpu.VMEM((2, XA_SUB, D_MODEL), BF16),
            pltpu.VMEM((tm, D_MODEL), BF16),
        ],
        compiler_params=pltpu.CompilerParams(
            dimension_semantics=("arbitrary", "arbitrary"), vmem_limit_bytes=_vmem_limit(est)),
        name="xattn",
    )(x, norm.reshape(1, D_MODEL), w_q, kv, w_o)


def kernel(x, mem, ffn1_norm, ffn1_w_gu, ffn1_w_down, mix_norm, w_in, lb_logits, hg_norm, conv_w,
           conv_norm, w_out, xa_norm, mem_norm, xa_wq, xa_wkv, xa_wo, ffn2_norm, ffn2_w_gu,
           ffn2_w_down, final_norm):
    bsz, t, d = x.shape
    assert d == D_MODEL and t % MIX_ROWS == 0 and (bsz * t) % FFN_ROWS == 0
    assert ffn1_norm.shape[0] == 1, "single-layer block"
    bf = lambda w: w[0].astype(BF16)

    x1 = _ffn(x.reshape(bsz * t, d), ffn1_norm[0], bf(ffn1_w_gu), bf(ffn1_w_down), final_norm,
              final_norm=False)
    x2 = _mix(x1.reshape(bsz, t, d), mix_norm[0], bf(w_in), lb_logits, hg_norm[0], conv_w[0],
              conv_norm[0], bf(w_out))
    kv = _kv(mem.reshape(bsz * N_MEM, d), mem_norm[0], bf(xa_wkv)).reshape(bsz, N_MEM, 2 * d)
    x3 = _xattn(x2, xa_norm[0], bf(xa_wq), kv, bf(xa_wo))
    y = _ffn(x3.reshape(bsz * t, d), ffn2_norm[0], bf(ffn2_w_gu), bf(ffn2_w_down), final_norm,
             final_norm=True)
    return y.reshape(bsz, t, d)
```

```python
import functools
import math

import jax
import jax.numpy as jnp
from jax import lax
from jax.experimental import pallas as pl
from jax.experimental.pallas import tpu as pltpu

D_MODEL = 1024
CHUNK = 64
SUB = 8
DIAG = 4
HG_WIDTH = 512
CONV_WIDTH = 512
HG_HEADS = 4
HG_DK = 128
CONV_K = 3
N_MEM = 256
XA_HEADS = 4
XA_HEAD_DIM = 256
D_FF = 2816
IN_COLS = 4 * HG_WIDTH + 3 * CONV_WIDTH
EPS = 1e-6

V7X_VMEM_BYTES = 64 * 1024 * 1024
MIB = 1024 * 1024

FFN_ROWS = 1024
FFN_SUB = 512
FFN_COLS = 256
MIX_ROWS = 1024
MACRO = 128
PIECE = 512
XA_ROWS = 1024
XA_SUB = 256
KV_ROWS = 256

BF16 = jnp.bfloat16
F32 = jnp.float32
BF16_ROWS = 16


def _vmem_limit(estimate_bytes):
    return int(min(estimate_bytes * 5 // 4 + 8 * MIB, V7X_VMEM_BYTES - 6 * MIB))


def _resident(shape):
    nd = len(shape)
    return pl.BlockSpec(shape, lambda *_: (0,) * nd, pipeline_mode=pl.Buffered(1))


def _rmsnorm(x, g):
    ms = jnp.mean(x * x, axis=-1, keepdims=True)
    return x * lax.rsqrt(ms + EPS) * g


def _silu(x):
    return x * (1.0 / (1.0 + jnp.exp(-x)))


def _dot(a, b):
    return jnp.dot(a, b, preferred_element_type=F32)


def _dot_nt(a, b):
    return lax.dot_general(a, b, (((1,), (1,)), ((), ())), preferred_element_type=F32)


def _dot_tn(a, b):
    return lax.dot_general(a, b, (((0,), (0,)), ((), ())), preferred_element_type=F32)


def _ffn_kernel(x_ref, g_ref, wgu_ref, wd_ref, fin_ref, *refs, final_norm, n_cast):
    cast_in, o_ref, cast_out = refs[:n_cast], refs[n_cast], refs[n_cast + 1:2 * n_cast + 1]
    h_ref, act_ref = refs[2 * n_cast + 1:]
    n_sub = FFN_ROWS // FFN_SUB
    gain = g_ref[...]

    def rows(i):
        return slice(i * FFN_SUB, (i + 1) * FFN_SUB)

    def norm(i):
        h_ref[rows(i), :] = _rmsnorm(x_ref[rows(i), :], gain).astype(BF16)

    def gate_up(i):
        h = h_ref[rows(i), :]
        for c in range(D_FF // FFN_COLS):
            lo = c * FFN_COLS
            gate = _dot(h, wgu_ref[:, lo:lo + FFN_COLS])
            up = _dot(h, wgu_ref[:, D_FF + lo:D_FF + lo + FFN_COLS])
            act_ref[rows(i), lo:lo + FFN_COLS] = (_silu(gate) * up).astype(BF16)

    def down(i):
        y = x_ref[rows(i), :] + 0.5 * _dot(act_ref[rows(i), :], wd_ref[...])
        if final_norm:
            y = _rmsnorm(y, fin_ref[...])
        o_ref[rows(i), :] = y

    norm(0)
    for i in range(n_sub):
        if i + 1 < n_sub:
            norm(i + 1)
        gate_up(i)
        if i >= 1:
            down(i - 1)
    down(n_sub - 1)

    for src, dst in zip(cast_in, cast_out):
        dst[...] = src[...].astype(BF16)


def _ffn(x2d, norm, w_gu, w_down, fin, *, final_norm, cast=()):
    n = x2d.shape[0]
    tm = FFN_ROWS
    steps = n // tm
    slices = [w.shape[0] // steps for w in cast]
    assert all(s % BF16_ROWS == 0 and s * steps == w.shape[0] for s, w in zip(slices, cast))
    est = (4 * tm * D_MODEL * 4 + w_gu.size * 2 + w_down.size * 2 + tm * D_FF * 2
           + tm * D_MODEL * 2 + 4 * FFN_SUB * FFN_COLS * 4
           + 2 * sum(s * w.shape[1] for s, w in zip(slices, cast)) * (4 + 2))
    cast_specs = [pl.BlockSpec((s, w.shape[1]), lambda i: (i, 0)) for s, w in zip(slices, cast)]
    out = pl.pallas_call(
        functools.partial(_ffn_kernel, final_norm=final_norm, n_cast=len(cast)),
        grid=(steps,),
        in_specs=[
            pl.BlockSpec((tm, D_MODEL), lambda i: (i, 0)),
            _resident((1, D_MODEL)),
            _resident((D_MODEL, 2 * D_FF)),
            _resident((D_FF, D_MODEL)),
            _resident((1, D_MODEL)),
        ] + cast_specs,
        out_specs=[pl.BlockSpec((tm, D_MODEL), lambda i: (i, 0))] + cast_specs,
        out_shape=[jax.ShapeDtypeStruct((n, D_MODEL), F32)]
        + [jax.ShapeDtypeStruct(w.shape, BF16) for w in cast],
        scratch_shapes=[pltpu.VMEM((tm, D_MODEL), BF16), pltpu.VMEM((tm, D_FF), BF16)],
        compiler_params=pltpu.CompilerParams(
            dimension_semantics=("arbitrary",), vmem_limit_bytes=_vmem_limit(est)),
        name="ffn_final" if final_norm else "ffn",
    )(x2d, norm.reshape(1, D_MODEL), w_gu, w_down, fin.reshape(1, D_MODEL), *cast)
    return out[0], list(out[1:])


def _chunk_cumsum(g):
    n = g.shape[-1]
    nb = CHUNK // SUB
    row = lax.broadcasted_iota(jnp.int32, (nb, SUB, n), 1)
    s = g.reshape(nb, SUB, n)
    for shift in (1, 2, 4):
        s = s + jnp.where(row >= shift, pltpu.roll(s, shift, 1), 0.0)
    carry = [jnp.zeros((1, 1, n), F32)]
    for i in range(1, nb):
        carry.append(carry[-1] + s[i - 1:i, SUB - 1:SUB, :])
    return (s + jnp.concatenate(carry, axis=0)).reshape(CHUNK, n)


def _hgrn2_masks():
    t = lax.broadcasted_iota(jnp.int32, (CHUNK, CHUNK), 0)
    s = lax.broadcasted_iota(jnp.int32, (CHUNK, CHUNK), 1)
    levels = []
    blk = CHUNK
    while blk > DIAG:
        half = blk // 2
        levels.append((t // blk == s // blk) & (t % blk >= half) & (s % blk < half))
        blk = half
    diag = [(s == t - d) & (t % DIAG >= d) for d in range(DIAG)]
    return levels, diag


def _mix_kernel(x_ref, nrm_ref, win_ref, lbl_ref, hgn_ref, cw_ref, cn_ref, wout_ref, o_ref,
                p_a, p_b, mix_a, mix_b, h_a, h_b, st_ref, ybuf_ref):
    n_macro = MIX_ROWS // MACRO

    @pl.when(pl.program_id(1) == 0)
    def _():
        st_ref[...] = jnp.zeros_like(st_ref)
        ybuf_ref[0:SUB, :] = jnp.zeros((SUB, CONV_WIDTH), F32)

    mix_b[...] = jnp.zeros((MACRO, D_MODEL), BF16)

    lbl = lbl_ref[...]
    lmax = jnp.max(lbl, axis=0, keepdims=True)
    lexp = jnp.exp(lbl - lmax)
    lb_all = lexp[0:1, :] / jnp.sum(lexp, axis=0, keepdims=True)

    level_masks, diag_masks = _hgrn2_masks()
    hgn = hgn_ref[...]
    nrm = nrm_ref[...]
    cw = cw_ref[...]
    cn = cn_ref[...]
    cb = 4 * HG_WIDTH

    def macro_rows(m):
        return pl.ds(pl.multiple_of(m * MACRO, MACRO), MACRO)

    def in_norm(m, h_ref):
        h_ref[...] = _rmsnorm(x_ref[0, macro_rows(m), :], nrm).astype(BF16)

    def in_proj_piece(h_ref, p_ref, c):
        cols = slice(c * PIECE, (c + 1) * PIECE)
        p_ref[:, cols] = _dot(h_ref[...], win_ref[:, cols])

    def out_proj_piece(m, mix_ref, c):
        rows = macro_rows(m)
        cols = slice(c * PIECE, (c + 1) * PIECE)
        o_ref[0, rows, cols] = x_ref[0, rows, cols] + _dot(mix_ref[...], wout_ref[:, cols])


    def hg_ab(p_ref, rows, hh):
        c0 = hh * HG_DK
        q = p_ref[rows, pl.ds(c0, HG_DK)]
        fz = p_ref[rows, pl.ds(HG_WIDTH + c0, HG_DK)]
        v = p_ref[rows, pl.ds(2 * HG_WIDTH + c0, HG_DK)]
        gt = p_ref[rows, pl.ds(3 * HG_WIDTH + c0, HG_DK)]
        lb = lb_all[:, c0:c0 + HG_DK]
        f = lb + (1.0 - lb) * (1.0 / (1.0 + jnp.exp(-fz)))
        k = 1.0 - f
        qs = _silu(q)
        b = _chunk_cumsum(jnp.log2(f))
        b_end = b[CHUNK - 1:CHUNK, :]
        u = dict(hh=hh, rows=rows, v_bf=v.astype(BF16), gate=_silu(gt))
        q_bf, k_bf = qs.astype(BF16), k.astype(BF16)
        ops = []
        blk = CHUNK
        for _ in level_masks:
            half = blk // 2
            nb = CHUNK // blk
            ref = b.reshape(nb, blk, HG_DK)[:, half - 1:half, :]
            ref = jnp.broadcast_to(ref, (nb, blk, HG_DK)).reshape(CHUNK, HG_DK)
            z = jnp.exp2(-jnp.abs(b - ref)).astype(BF16)
            ops.append((q_bf * z, k_bf * z))
            blk = half
        u["level_ops"] = ops
        diag = [jnp.sum(qs * k, axis=-1, keepdims=True)]
        f3 = f.reshape(CHUNK // SUB, SUB, HG_DK)
        decay = f
        for d in range(1, DIAG):
            f_d = pltpu.roll(f3, d, 1).reshape(CHUNK, HG_DK)
            diag.append(jnp.sum(qs * (1.0 - f_d) * decay, axis=-1, keepdims=True))
            decay = decay * f_d
        u["diag"] = diag
        u["qe"] = (qs * jnp.exp2(b)).astype(BF16)
        u["ke"] = (k * jnp.exp2(b_end - b)).astype(BF16)
        u["decay"] = jnp.exp2(b_end)
        return u

    def hg_mb(u):
        u["level_out"] = [_dot_nt(ql, kl) for ql, kl in u.pop("level_ops")]

    def hg_c(u):
        attn = jnp.zeros((CHUNK, CHUNK), F32)
        for lm, a in zip(level_masks, u.pop("level_out")):
            attn = jnp.where(lm, a, attn)
        for dm, rj in zip(diag_masks, u.pop("diag")):
            attn = jnp.where(dm, rj, attn)
        u["attn"] = attn.astype(BF16)

    def hg_mc(u):
        hh, v_bf = u["hh"], u.pop("v_bf")
        st = st_ref[hh]
        o = _dot(u.pop("attn"), v_bf)
        u["o"] = o + _dot_nt(u.pop("qe"), st.astype(BF16))
        st_ref[hh] = st * u.pop("decay") + _dot_tn(v_bf, u.pop("ke"))

    def hg_d(u, mix_ref):
        on = _rmsnorm(u.pop("o"), hgn) * u.pop("gate")
        mix_ref[u["rows"], pl.ds(u["hh"] * HG_DK, HG_DK)] = on.astype(BF16)

    def short_conv(p_ref, mix_ref, rows):
        y = p_ref[rows, cb + CONV_WIDTH:cb + 2 * CONV_WIDTH] * \
            p_ref[rows, cb + 2 * CONV_WIDTH:cb + 3 * CONV_WIDTH]
        ybuf_ref[SUB:SUB + CHUNK, :] = y
        y1 = ybuf_ref[SUB - 1:SUB - 1 + CHUNK, :]
        y2 = ybuf_ref[SUB - 2:SUB - 2 + CHUNK, :]
        conv = cw[0:1, :] * y2 + cw[1:2, :] * y1 + cw[2:3, :] * y
        ocv = _rmsnorm(p_ref[rows, cb:cb + CONV_WIDTH] * conv, cn)
        mix_ref[rows, HG_WIDTH:HG_WIDTH + CONV_WIDTH] = ocv.astype(BF16)
        ybuf_ref[0:SUB, :] = ybuf_ref[CHUNK:CHUNK + SUB, :]

    def each(phase, units, *args):
        for u in units:
            phase(u, *args)

    n_chunks = MACRO // CHUNK
    chunk_rows = [pl.ds(j * CHUNK, CHUNK) for j in range(n_chunks)]
    heads = range(HG_HEADS)
    last = n_macro - 1

    in_norm(0, h_b)
    for c in range(IN_COLS // PIECE):
        in_proj_piece(h_b, p_a, c)
    in_norm(1, h_a)

    def trip(j, carry):
        m = 2 * j
        big_a = [functools.partial(in_proj_piece, h_a, p_b, c) for c in range(IN_COLS // PIECE)]
        big_a += [functools.partial(out_proj_piece, jnp.maximum(m - 1, 0), mix_b, c)
                  for c in range(D_MODEL // PIECE)]
        big_b = [functools.partial(in_proj_piece, h_b, p_a, c) for c in range(IN_COLS // PIECE)]
        big_b += [functools.partial(out_proj_piece, m, mix_a, c) for c in range(D_MODEL // PIECE)]
        groups = [(p_a, mix_a, r) for r in chunk_rows] + [(p_b, mix_b, r) for r in chunk_rows]
        ng = len(groups)
        plan = {i: [] for i in range(-1, ng + 1)}
        for pieces, slots in ((big_a, range(-1, n_chunks + 1)), (big_b, range(n_chunks + 1, ng + 1))):
            for k, piece in enumerate(pieces):
                plan[slots[k * len(slots) // len(pieces)]].append(piece)

        def issue(slot):
            for piece in plan[slot]:
                piece()

        issue(-1)
        units = []
        for i in range(ng + 1):
            if i < ng:
                p_ref, _, rows = groups[i]
                units.append([hg_ab(p_ref, rows, hh) for hh in heads])
            if i >= 1:
                short_conv(*groups[i - 1])
                each(hg_c, units[i - 1])
            if i >= 2:
                each(hg_d, units[i - 2], groups[i - 2][1])
            if i == n_chunks - 1:
                in_norm(jnp.minimum(m + 2, last), h_b)
            if i == ng - 1:
                in_norm(jnp.minimum(m + 3, last), h_a)
            if i < ng:
                each(hg_mb, units[i])
            if i >= 1:
                each(hg_mc, units[i - 1])
            issue(i)
        each(hg_d, units[ng - 1], groups[ng - 1][1])
        return carry

    lax.fori_loop(0, n_macro // 2, trip, 0)
    for c in range(D_MODEL // PIECE):
        out_proj_piece(last, mix_b, c)


def _mix(x, norm, w_in, lb_logits, hg_norm, conv_w, conv_norm, w_out):
    bsz, t, _ = x.shape
    tt = MIX_ROWS
    est = (4 * tt * D_MODEL * 4 + w_in.size * 2 + w_out.size * 2 + 2 * MACRO * IN_COLS * 4
           + HG_HEADS * HG_DK * HG_DK * 4 + (CHUNK + SUB) * CONV_WIDTH * 4
           + 4 * MACRO * D_MODEL * 2)
    return pl.pallas_call(
        _mix_kernel,
        grid=(bsz, t // tt),
        in_specs=[
            pl.BlockSpec((1, tt, D_MODEL), lambda b, i: (b, i, 0)),
            _resident((1, D_MODEL)),
            _resident((D_MODEL, IN_COLS)),
            _resident(lb_logits.shape),
            _resident((1, HG_DK)),
            _resident((CONV_K, CONV_WIDTH)),
            _resident((1, CONV_WIDTH)),
            _resident((D_MODEL, D_MODEL)),
        ],
        out_specs=pl.BlockSpec((1, tt, D_MODEL), lambda b, i: (b, i, 0)),
        out_shape=jax.ShapeDtypeStruct(x.shape, F32),
        scratch_shapes=[
            pltpu.VMEM((MACRO, IN_COLS), F32),
            pltpu.VMEM((MACRO, IN_COLS), F32),
            pltpu.VMEM((MACRO, D_MODEL), BF16),
            pltpu.VMEM((MACRO, D_MODEL), BF16),
            pltpu.VMEM((MACRO, D_MODEL), BF16),
            pltpu.VMEM((MACRO, D_MODEL), BF16),
            pltpu.VMEM((HG_HEADS, HG_DK, HG_DK), F32),
            pltpu.VMEM((CHUNK + SUB, CONV_WIDTH), F32),
        ],
        compiler_params=pltpu.CompilerParams(
            dimension_semantics=("arbitrary", "arbitrary"), vmem_limit_bytes=_vmem_limit(est)),
        name="mix",
    )(x, norm.reshape(1, D_MODEL), w_in, lb_logits, hg_norm.reshape(1, HG_DK), conv_w,
      conv_norm.reshape(1, CONV_WIDTH), w_out)


def _kv_kernel(m_ref, g_ref, w_ref, o_ref):
    h = _rmsnorm(m_ref[...], g_ref[...]).astype(BF16)
    o_ref[...] = _dot(h, w_ref[...]).astype(BF16)


def _kv(mem2d, norm, w_kv):
    n = mem2d.shape[0]
    est = 2 * KV_ROWS * D_MODEL * 4 + w_kv.size * 2 + 2 * KV_ROWS * 2 * D_MODEL * 2
    return pl.pallas_call(
        _kv_kernel,
        grid=(n // KV_ROWS,),
        in_specs=[
            pl.BlockSpec((KV_ROWS, D_MODEL), lambda i: (i, 0)),
            _resident((1, D_MODEL)),
            _resident((D_MODEL, 2 * D_MODEL)),
        ],
        out_specs=pl.BlockSpec((KV_ROWS, 2 * D_MODEL), lambda i: (i, 0)),
        out_shape=jax.ShapeDtypeStruct((n, 2 * D_MODEL), BF16),
        compiler_params=pltpu.CompilerParams(
            dimension_semantics=("arbitrary",), vmem_limit_bytes=_vmem_limit(est)),
        name="kv",
    )(mem2d, norm.reshape(1, D_MODEL), w_kv)


def _xattn_kernel(x_ref, g_ref, wq_ref, kv_ref, wo_ref, o_ref, h_ref, q_ref, s_ref, p_ref,
                  att_ref):
    n_sub = XA_ROWS // XA_SUB
    scale = 1.0 / math.sqrt(XA_HEAD_DIM)
    gain = g_ref[...]
    heads = [slice(hh * XA_HEAD_DIM, (hh + 1) * XA_HEAD_DIM) for hh in range(XA_HEADS)]

    def rows(i):
        return slice(i * XA_SUB, (i + 1) * XA_SUB)

    def norm(i):
        h_ref[rows(i), :] = _rmsnorm(x_ref[0, rows(i), :], gain).astype(BF16)

    def q_proj(i):
        for hd in heads:
            q_ref[rows(i), hd] = (_dot(h_ref[rows(i), :], wq_ref[:, hd]) * scale).astype(BF16)

    def scores(i):
        for hd in heads:
            s_ref[i % 2, :, hd] = _dot_nt(q_ref[rows(i), hd], kv_ref[0, :, hd])

    def softmax(i):
        for hd in heads:
            s = s_ref[i % 2, :, hd]
            e = jnp.exp(s - jnp.max(s, axis=-1, keepdims=True))
            p_ref[i % 2, :, hd] = (e * (1.0 / jnp.sum(e, axis=-1, keepdims=True))).astype(BF16)

    def pv(i):
        for hh, hd in enumerate(heads):
            v = kv_ref[0, :, D_MODEL + hh * XA_HEAD_DIM:D_MODEL + (hh + 1) * XA_HEAD_DIM]
            att_ref[rows(i), hd] = _dot(p_ref[i % 2, :, hd], v).astype(BF16)

    def out_proj(i):
        o_ref[0, rows(i), :] = x_ref[0, rows(i), :] + _dot(att_ref[rows(i), :], wo_ref[...])

    norm(0)
    if n_sub > 1:
        norm(1)
    q_proj(0)
    scores(0)
    for i in range(n_sub):
        if i + 1 < n_sub:
            q_proj(i + 1)
        softmax(i)
        pv(i)
        if i + 1 < n_sub:
            scores(i + 1)
        if i + 2 < n_sub:
            norm(i + 2)
        out_proj(i)


def _xattn(x, norm, w_q, kv, w_o):
    bsz, t, _ = x.shape
    tm = XA_ROWS
    est = (4 * tm * D_MODEL * 4 + w_q.size * 2 + w_o.size * 2 + 2 * N_MEM * 2 * D_MODEL * 2
           + 3 * tm * D_MODEL * 2 + 2 * XA_SUB * D_MODEL * (4 + 2))
    return pl.pallas_call(
        _xattn_kernel,
        grid=(bsz, t // tm),
        in_specs=[
            pl.BlockSpec((1, tm, D_MODEL), lambda b, i: (b, i, 0)),
            _resident((1, D_MODEL)),
            _resident((D_MODEL, D_MODEL)),
            pl.BlockSpec((1, N_MEM, 2 * D_MODEL), lambda b, i: (b, 0, 0)),
            _resident((D_MODEL, D_MODEL)),
        ],
        out_specs=pl.BlockSpec((1, tm, D_MODEL), lambda b, i: (b, i, 0)),
        out_shape=jax.ShapeDtypeStruct(x.shape, F32),
        scratch_shapes=[
            pltpu.VMEM((tm, D_MODEL), BF16),
            pltpu.VMEM((tm, D_MODEL), BF16),
            pltpu.VMEM((2, XA_SUB, D_MODEL), F32),
            pltpu.VMEM((2, XA_SUB, D_MODEL), BF16),
            pltpu.VMEM((tm, D_MODEL), BF16),
        ],
        compiler_params=pltpu.CompilerParams(
            dimension_semantics=("arbitrary", "arbitrary"), vmem_limit_bytes=_vmem_limit(est)),
        name="xattn",
    )(x, norm.reshape(1, D_MODEL), w_q, kv, w_o)


def kernel(x, mem, ffn1_norm, ffn1_w_gu, ffn1_w_down, mix_norm, w_in, lb_logits, hg_norm, conv_w,
           conv_norm, w_out, xa_norm, mem_norm, xa_wq, xa_wkv, xa_wo, ffn2_norm, ffn2_w_gu,
           ffn2_w_down, final_norm):
    bsz, t, d = x.shape
    assert d == D_MODEL and t % MIX_ROWS == 0 and (bsz * t) % FFN_ROWS == 0
    assert ffn1_norm.shape[0] == 1, "single-layer block"
    later = [w_in[0], w_out[0], xa_wq[0], xa_wkv[0], xa_wo[0], ffn2_w_gu[0], ffn2_w_down[0]]
    x1, later_bf = _ffn(x.reshape(bsz * t, d), ffn1_norm[0], ffn1_w_gu[0].astype(BF16),
                        ffn1_w_down[0].astype(BF16), final_norm, final_norm=False, cast=later)
    w_in_bf, w_out_bf, wq_bf, wkv_bf, wo_bf, w_gu2_bf, w_down2_bf = later_bf
    x2 = _mix(x1.reshape(bsz, t, d), mix_norm[0], w_in_bf, lb_logits, hg_norm[0], conv_w[0],
              conv_norm[0], w_out_bf)
    kv = _kv(mem.reshape(bsz * N_MEM, d), mem_norm[0], wkv_bf).reshape(bsz, N_MEM, 2 * d)
    x3 = _xattn(x2, xa_norm[0], wq_bf, kv, wo_bf)
    y, _ = _ffn(x3.reshape(bsz * t, d), ffn2_norm[0], w_gu2_bf, w_down2_bf, final_norm,
                final_norm=True)
    return y.reshape(bsz, t, d)
```

```python
import functools
import math

import jax
import jax.numpy as jnp
from jax import lax
from jax.experimental import pallas as pl
from jax.experimental.pallas import tpu as pltpu

D_MODEL = 1024
CHUNK = 64
SUB = 8
DIAG = 4
HG_WIDTH = 512
CONV_WIDTH = 512
HG_HEADS = 4
HG_DK = 128
CONV_K = 3
N_MEM = 256
XA_HEADS = 4
XA_HEAD_DIM = 256
D_FF = 2816
IN_COLS = 4 * HG_WIDTH + 3 * CONV_WIDTH
EPS = 1e-6

V7X_VMEM_BYTES = 64 * 1024 * 1024
MIB = 1024 * 1024

FFN_ROWS = 1024
FFN_SUB = 512
FFN_COLS = 256
MIX_ROWS = 1024
MACRO = 128
PIECE = 512
XA_ROWS = 1024
XA_SUB = 256
KV_ROWS = 256

BF16 = jnp.bfloat16
F32 = jnp.float32
BF16_ROWS = 16


def _vmem_limit(estimate_bytes):
    return int(min(estimate_bytes * 5 // 4 + 8 * MIB, V7X_VMEM_BYTES - 6 * MIB))


def _resident(shape):
    nd = len(shape)
    return pl.BlockSpec(shape, lambda *_: (0,) * nd, pipeline_mode=pl.Buffered(1))


def _rmsnorm(x, g):
    ms = jnp.mean(x * x, axis=-1, keepdims=True)
    return x * lax.rsqrt(ms + EPS) * g


def _silu(x):
    return x * (1.0 / (1.0 + jnp.exp(-x)))


def _dot(a, b):
    return jnp.dot(a, b, preferred_element_type=F32)


def _dot_nt(a, b):
    return lax.dot_general(a, b, (((1,), (1,)), ((), ())), preferred_element_type=F32)


def _dot_tn(a, b):
    return lax.dot_general(a, b, (((0,), (0,)), ((), ())), preferred_element_type=F32)


def _ffn_kernel(x_ref, g_ref, wgu_ref, wd_ref, fin_ref, *refs, final_norm, n_cast):
    cast_in, o_ref, cast_out = refs[:n_cast], refs[n_cast], refs[n_cast + 1:2 * n_cast + 1]
    h_ref, act_ref = refs[2 * n_cast + 1:]
    n_sub = FFN_ROWS // FFN_SUB
    gain = g_ref[...]

    def rows(i):
        return slice(i * FFN_SUB, (i + 1) * FFN_SUB)

    def norm(i):
        h_ref[rows(i), :] = _rmsnorm(x_ref[rows(i), :], gain).astype(BF16)

    def gate_up(i):
        h = h_ref[rows(i), :]
        for c in range(D_FF // FFN_COLS):
            lo = c * FFN_COLS
            gate = _dot(h, wgu_ref[:, lo:lo + FFN_COLS])
            up = _dot(h, wgu_ref[:, D_FF + lo:D_FF + lo + FFN_COLS])
            act_ref[rows(i), lo:lo + FFN_COLS] = (_silu(gate) * up).astype(BF16)

    def down(i):
        y = x_ref[rows(i), :] + 0.5 * _dot(act_ref[rows(i), :], wd_ref[...])
        if final_norm:
            y = _rmsnorm(y, fin_ref[...])
        o_ref[rows(i), :] = y

    norm(0)
    for i in range(n_sub):
        if i + 1 < n_sub:
            norm(i + 1)
        gate_up(i)
        if i >= 1:
            down(i - 1)
    down(n_sub - 1)

    for src, dst in zip(cast_in, cast_out):
        dst[...] = src[...].astype(BF16)


def _ffn(x2d, norm, w_gu, w_down, fin, *, final_norm, cast=()):
    n = x2d.shape[0]
    tm = FFN_ROWS
    steps = n // tm
    slices = [w.shape[0] // steps for w in cast]
    assert all(s % BF16_ROWS == 0 and s * steps == w.shape[0] for s, w in zip(slices, cast))
    est = (4 * tm * D_MODEL * 4 + w_gu.size * 2 + w_down.size * 2 + tm * D_FF * 2
           + tm * D_MODEL * 2 + 4 * FFN_SUB * FFN_COLS * 4
           + 2 * sum(s * w.shape[1] for s, w in zip(slices, cast)) * (4 + 2))
    cast_specs = [pl.BlockSpec((s, w.shape[1]), lambda i: (i, 0)) for s, w in zip(slices, cast)]
    out = pl.pallas_call(
        functools.partial(_ffn_kernel, final_norm=final_norm, n_cast=len(cast)),
        grid=(steps,),
        in_specs=[
            pl.BlockSpec((tm, D_MODEL), lambda i: (i, 0)),
            _resident((1, D_MODEL)),
            _resident((D_MODEL, 2 * D_FF)),
            _resident((D_FF, D_MODEL)),
            _resident((1, D_MODEL)),
        ] + cast_specs,
        out_specs=[pl.BlockSpec((tm, D_MODEL), lambda i: (i, 0))] + cast_specs,
        out_shape=[jax.ShapeDtypeStruct((n, D_MODEL), F32)]
        + [jax.ShapeDtypeStruct(w.shape, BF16) for w in cast],
        scratch_shapes=[pltpu.VMEM((tm, D_MODEL), BF16), pltpu.VMEM((tm, D_FF), BF16)],
        compiler_params=pltpu.CompilerParams(
            dimension_semantics=("arbitrary",), vmem_limit_bytes=_vmem_limit(est)),
        name="ffn_final" if final_norm else "ffn",
    )(x2d, norm.reshape(1, D_MODEL), w_gu, w_down, fin.reshape(1, D_MODEL), *cast)
    return out[0], list(out[1:])


def _chunk_cumsum(g):
    n = g.shape[-1]
    nb = CHUNK // SUB
    row = lax.broadcasted_iota(jnp.int32, (nb, SUB, n), 1)
    s = g.reshape(nb, SUB, n)
    for shift in (1, 2, 4):
        s = s + jnp.where(row >= shift, pltpu.roll(s, shift, 1), 0.0)
    carry = [jnp.zeros((1, 1, n), F32)]
    for i in range(1, nb):
        carry.append(carry[-1] + s[i - 1:i, SUB - 1:SUB, :])
    return (s + jnp.concatenate(carry, axis=0)).reshape(CHUNK, n)


def _hgrn2_masks():
    t = lax.broadcasted_iota(jnp.int32, (CHUNK, CHUNK), 0)
    s = lax.broadcasted_iota(jnp.int32, (CHUNK, CHUNK), 1)
    levels = []
    blk = CHUNK
    while blk > DIAG:
        half = blk // 2
        levels.append((t // blk == s // blk) & (t % blk >= half) & (s % blk < half))
        blk = half
    diag = [(s == t - d) & (t % DIAG >= d) for d in range(DIAG)]
    return levels, diag


def _mix_kernel(x_ref, nrm_ref, win_ref, lbl_ref, hgn_ref, cw_ref, cn_ref, wout_ref, o_ref,
                p_a, p_b, mix_a, mix_b, h_a, h_b, st_ref, ybuf_ref):
    n_macro = MIX_ROWS // MACRO

    @pl.when(pl.program_id(1) == 0)
    def _():
        st_ref[...] = jnp.zeros_like(st_ref)
        ybuf_ref[0:SUB, :] = jnp.zeros((SUB, CONV_WIDTH), F32)

    mix_b[...] = jnp.zeros((MACRO, D_MODEL), BF16)

    lbl = lbl_ref[...]
    lmax = jnp.max(lbl, axis=0, keepdims=True)
    lexp = jnp.exp(lbl - lmax)
    lb_all = lexp[0:1, :] / jnp.sum(lexp, axis=0, keepdims=True)

    level_masks, diag_masks = _hgrn2_masks()
    hgn = hgn_ref[...]
    nrm = nrm_ref[...]
    cw = cw_ref[...]
    cn = cn_ref[...]
    cb = 4 * HG_WIDTH

    def macro_rows(m):
        return pl.ds(pl.multiple_of(m * MACRO, MACRO), MACRO)

    def in_norm(m, h_ref):
        h_ref[...] = _rmsnorm(x_ref[0, macro_rows(m), :], nrm).astype(BF16)

    def in_proj_piece(h_ref, p_ref, c):
        cols = slice(c * PIECE, (c + 1) * PIECE)
        p_ref[:, cols] = _dot(h_ref[...], win_ref[:, cols])

    def out_proj_piece(m, mix_ref, c):
        rows = macro_rows(m)
        cols = slice(c * PIECE, (c + 1) * PIECE)
        o_ref[0, rows, cols] = x_ref[0, rows, cols] + _dot(mix_ref[...], wout_ref[:, cols])


    def hg_ab(p_ref, rows, hh):
        c0 = hh * HG_DK
        q = p_ref[rows, pl.ds(c0, HG_DK)]
        fz = p_ref[rows, pl.ds(HG_WIDTH + c0, HG_DK)]
        v = p_ref[rows, pl.ds(2 * HG_WIDTH + c0, HG_DK)]
        gt = p_ref[rows, pl.ds(3 * HG_WIDTH + c0, HG_DK)]
        lb = lb_all[:, c0:c0 + HG_DK]
        f = lb + (1.0 - lb) * (1.0 / (1.0 + jnp.exp(-fz)))
        k = 1.0 - f
        qs = _silu(q)
        b = _chunk_cumsum(jnp.log2(f))
        b_end = b[CHUNK - 1:CHUNK, :]
        u = dict(hh=hh, rows=rows, v_bf=v.astype(BF16), gate=_silu(gt))
        q_bf, k_bf = qs.astype(BF16), k.astype(BF16)
        ops = []
        blk = CHUNK
        for _ in level_masks:
            half = blk // 2
            nb = CHUNK // blk
            ref = b.reshape(nb, blk, HG_DK)[:, half - 1:half, :]
            ref = jnp.broadcast_to(ref, (nb, blk, HG_DK)).reshape(CHUNK, HG_DK)
            z = jnp.exp2(-jnp.abs(b - ref)).astype(BF16)
            ops.append((q_bf * z, k_bf * z))
            blk = half
        u["level_ops"] = ops
        attn = jnp.where(diag_masks[0], jnp.sum(qs * k, axis=-1, keepdims=True), 0.0)
        f3 = f.reshape(CHUNK // SUB, SUB, HG_DK)
        decay = f
        for d in range(1, DIAG):
            f_d = pltpu.roll(f3, d, 1).reshape(CHUNK, HG_DK)
            r_d = jnp.sum(qs * (1.0 - f_d) * decay, axis=-1, keepdims=True)
            attn = jnp.where(diag_masks[d], r_d, attn)
            decay = decay * f_d
        u["attn_diag"] = attn
        u["qe"] = (qs * jnp.exp2(b)).astype(BF16)
        u["ke"] = (k * jnp.exp2(b_end - b)).astype(BF16)
        u["decay"] = jnp.exp2(b_end)
        return u

    def hg_mb(u):
        u["level_out"] = [_dot_nt(ql, kl) for ql, kl in u.pop("level_ops")]

    def hg_c(u):
        attn = u.pop("attn_diag")
        for lm, a in zip(level_masks, u.pop("level_out")):
            attn = jnp.where(lm, a, attn)
        u["attn"] = attn.astype(BF16)

    def hg_mc(u):
        hh, v_bf = u["hh"], u.pop("v_bf")
        st = st_ref[hh]
        o = _dot(u.pop("attn"), v_bf)
        u["o"] = o + _dot(u.pop("qe"), st.astype(BF16))
        decay_col = jnp.transpose(jnp.broadcast_to(u.pop("decay"), (HG_DK, HG_DK)))
        st_ref[hh] = st * decay_col + _dot_tn(u.pop("ke"), v_bf)

    def hg_d(u, mix_ref):
        on = _rmsnorm(u.pop("o"), hgn) * u.pop("gate")
        mix_ref[u["rows"], pl.ds(u["hh"] * HG_DK, HG_DK)] = on.astype(BF16)

    def short_conv(p_ref, mix_ref, rows):
        y = p_ref[rows, cb + CONV_WIDTH:cb + 2 * CONV_WIDTH] * \
            p_ref[rows, cb + 2 * CONV_WIDTH:cb + 3 * CONV_WIDTH]
        ybuf_ref[SUB:SUB + CHUNK, :] = y
        y1 = ybuf_ref[SUB - 1:SUB - 1 + CHUNK, :]
        y2 = ybuf_ref[SUB - 2:SUB - 2 + CHUNK, :]
        conv = cw[0:1, :] * y2 + cw[1:2, :] * y1 + cw[2:3, :] * y
        ocv = _rmsnorm(p_ref[rows, cb:cb + CONV_WIDTH] * conv, cn)
        mix_ref[rows, HG_WIDTH:HG_WIDTH + CONV_WIDTH] = ocv.astype(BF16)
        ybuf_ref[0:SUB, :] = ybuf_ref[CHUNK:CHUNK + SUB, :]

    def each(phase, units, *args):
        for u in units:
            phase(u, *args)

    n_chunks = MACRO // CHUNK
    chunk_rows = [pl.ds(j * CHUNK, CHUNK) for j in range(n_chunks)]
    heads = range(HG_HEADS)
    last = n_macro - 1

    in_norm(0, h_b)
    for c in range(IN_COLS // PIECE):
        in_proj_piece(h_b, p_a, c)
    in_norm(1, h_a)

    def trip(j, carry):
        m = 2 * j
        big_a = [functools.partial(in_proj_piece, h_a, p_b, c) for c in range(IN_COLS // PIECE)]
        big_a += [functools.partial(out_proj_piece, jnp.maximum(m - 1, 0), mix_b, c)
                  for c in range(D_MODEL // PIECE)]
        big_b = [functools.partial(in_proj_piece, h_b, p_a, c) for c in range(IN_COLS // PIECE)]
        big_b += [functools.partial(out_proj_piece, m, mix_a, c) for c in range(D_MODEL // PIECE)]
        groups = [(p_a, mix_a, r) for r in chunk_rows] + [(p_b, mix_b, r) for r in chunk_rows]
        ng = len(groups)
        plan = {i: [] for i in range(-1, ng + 1)}
        for pieces, slots in ((big_a, range(-1, n_chunks + 1)), (big_b, range(n_chunks + 1, ng + 1))):
            for k, piece in enumerate(pieces):
                plan[slots[k * len(slots) // len(pieces)]].append(piece)

        def issue(slot):
            for piece in plan[slot]:
                piece()

        issue(-1)
        units = []
        for i in range(ng + 1):
            if i < ng:
                p_ref, _, rows = groups[i]
                units.append([hg_ab(p_ref, rows, hh) for hh in heads])
            if i >= 1:
                short_conv(*groups[i - 1])
                each(hg_c, units[i - 1])
            if i >= 2:
                each(hg_d, units[i - 2], groups[i - 2][1])
            if i == n_chunks - 1:
                in_norm(jnp.minimum(m + 2, last), h_b)
            if i == ng - 1:
                in_norm(jnp.minimum(m + 3, last), h_a)
            if i < ng:
                each(hg_mb, units[i])
            if i >= 1:
                each(hg_mc, units[i - 1])
            issue(i)
        each(hg_d, units[ng - 1], groups[ng - 1][1])
        return carry

    lax.fori_loop(0, n_macro // 2, trip, 0)
    for c in range(D_MODEL // PIECE):
        out_proj_piece(last, mix_b, c)


def _mix(x, norm, w_in, lb_logits, hg_norm, conv_w, conv_norm, w_out):
    bsz, t, _ = x.shape
    tt = MIX_ROWS
    est = (4 * tt * D_MODEL * 4 + w_in.size * 2 + w_out.size * 2 + 2 * MACRO * IN_COLS * 4
           + HG_HEADS * HG_DK * HG_DK * 4 + (CHUNK + SUB) * CONV_WIDTH * 4
           + 4 * MACRO * D_MODEL * 2)
    return pl.pallas_call(
        _mix_kernel,
        grid=(bsz, t // tt),
        in_specs=[
            pl.BlockSpec((1, tt, D_MODEL), lambda b, i: (b, i, 0)),
            _resident((1, D_MODEL)),
            _resident((D_MODEL, IN_COLS)),
            _resident(lb_logits.shape),
            _resident((1, HG_DK)),
            _resident((CONV_K, CONV_WIDTH)),
            _resident((1, CONV_WIDTH)),
            _resident((D_MODEL, D_MODEL)),
        ],
        out_specs=pl.BlockSpec((1, tt, D_MODEL), lambda b, i: (b, i, 0)),
        out_shape=jax.ShapeDtypeStruct(x.shape, F32),
        scratch_shapes=[
            pltpu.VMEM((MACRO, IN_COLS), F32),
            pltpu.VMEM((MACRO, IN_COLS), F32),
            pltpu.VMEM((MACRO, D_MODEL), BF16),
            pltpu.VMEM((MACRO, D_MODEL), BF16),
            pltpu.VMEM((MACRO, D_MODEL), BF16),
            pltpu.VMEM((MACRO, D_MODEL), BF16),
            pltpu.VMEM((HG_HEADS, HG_DK, HG_DK), F32),
            pltpu.VMEM((CHUNK + SUB, CONV_WIDTH), F32),
        ],
        compiler_params=pltpu.CompilerParams(
            dimension_semantics=("arbitrary", "arbitrary"), vmem_limit_bytes=_vmem_limit(est)),
        name="mix",
    )(x, norm.reshape(1, D_MODEL), w_in, lb_logits, hg_norm.reshape(1, HG_DK), conv_w,
      conv_norm.reshape(1, CONV_WIDTH), w_out)


def _kv_kernel(m_ref, g_ref, w_ref, o_ref):
    h = _rmsnorm(m_ref[...], g_ref[...]).astype(BF16)
    o_ref[...] = _dot(h, w_ref[...]).astype(BF16)


def _kv(mem2d, norm, w_kv):
    n = mem2d.shape[0]
    est = 2 * KV_ROWS * D_MODEL * 4 + w_kv.size * 2 + 2 * KV_ROWS * 2 * D_MODEL * 2
    return pl.pallas_call(
        _kv_kernel,
        grid=(n // KV_ROWS,),
        in_specs=[
            pl.BlockSpec((KV_ROWS, D_MODEL), lambda i: (i, 0)),
            _resident((1, D_MODEL)),
            _resident((D_MODEL, 2 * D_MODEL)),
        ],
        out_specs=pl.BlockSpec((KV_ROWS, 2 * D_MODEL), lambda i: (i, 0)),
        out_shape=jax.ShapeDtypeStruct((n, 2 * D_MODEL), BF16),
        compiler_params=pltpu.CompilerParams(
            dimension_semantics=("arbitrary",), vmem_limit_bytes=_vmem_limit(est)),
        name="kv",
    )(mem2d, norm.reshape(1, D_MODEL), w_kv)


def _xattn_kernel(x_ref, g_ref, wq_ref, kv_ref, wo_ref, o_ref, h_ref, q_ref, s_ref, p_ref,
                  att_ref):
    n_sub = XA_ROWS // XA_SUB
    scale = 1.0 / math.sqrt(XA_HEAD_DIM)
    gain = g_ref[...]
    heads = [slice(hh * XA_HEAD_DIM, (hh + 1) * XA_HEAD_DIM) for hh in range(XA_HEADS)]

    def rows(i):
        return slice(i * XA_SUB, (i + 1) * XA_SUB)

    def norm(i):
        h_ref[rows(i), :] = _rmsnorm(x_ref[0, rows(i), :], gain).astype(BF16)

    def q_proj(i):
        for hd in heads:
            q_ref[rows(i), hd] = (_dot(h_ref[rows(i), :], wq_ref[:, hd]) * scale).astype(BF16)

    def scores(i):
        for hd in heads:
            s_ref[i % 2, :, hd] = _dot_nt(q_ref[rows(i), hd], kv_ref[0, :, hd])

    def softmax(i):
        for hd in heads:
            s = s_ref[i % 2, :, hd]
            e = jnp.exp(s - jnp.max(s, axis=-1, keepdims=True))
            p_ref[i % 2, :, hd] = (e * (1.0 / jnp.sum(e, axis=-1, keepdims=True))).astype(BF16)

    def pv(i):
        for hh, hd in enumerate(heads):
            v = kv_ref[0, :, D_MODEL + hh * XA_HEAD_DIM:D_MODEL + (hh + 1) * XA_HEAD_DIM]
            att_ref[rows(i), hd] = _dot(p_ref[i % 2, :, hd], v).astype(BF16)

    def out_proj(i):
        o_ref[0, rows(i), :] = x_ref[0, rows(i), :] + _dot(att_ref[rows(i), :], wo_ref[...])

    norm(0)
    if n_sub > 1:
        norm(1)
    q_proj(0)
    scores(0)
    for i in range(n_sub):
        if i + 1 < n_sub:
            q_proj(i + 1)
        softmax(i)
        pv(i)
        if i + 1 < n_sub:
            scores(i + 1)
        if i + 2 < n_sub:
            norm(i + 2)
        out_proj(i)


def _xattn(x, norm, w_q, kv, w_o):
    bsz, t, _ = x.shape
    tm = XA_ROWS
    est = (4 * tm * D_MODEL * 4 + w_q.size * 2 + w_o.size * 2 + 2 * N_MEM * 2 * D_MODEL * 2
           + 3 * tm * D_MODEL * 2 + 2 * XA_SUB * D_MODEL * (4 + 2))
    return pl.pallas_call(
        _xattn_kernel,
        grid=(bsz, t // tm),
        in_specs=[
            pl.BlockSpec((1, tm, D_MODEL), lambda b, i: (b, i, 0)),
            _resident((1, D_MODEL)),
            _resident((D_MODEL, D_MODEL)),
            pl.BlockSpec((1, N_MEM, 2 * D_MODEL), lambda b, i: (b, 0, 0)),
            _resident((D_MODEL, D_MODEL)),
        ],
        out_specs=pl.BlockSpec((1, tm, D_MODEL), lambda b, i: (b, i, 0)),
        out_shape=jax.ShapeDtypeStruct(x.shape, F32),
        scratch_shapes=[
            pltpu.VMEM((tm, D_MODEL), BF16),
            pltpu.VMEM((tm, D_MODEL), BF16),
            pltpu.VMEM((2, XA_SUB, D_MODEL), F32),
            pltpu.VMEM((2, XA_SUB, D_MODEL), BF16),
            pltpu.VMEM((tm, D_MODEL), BF16),
        ],
        compiler_params=pltpu.CompilerParams(
            dimension_semantics=("arbitrary", "arbitrary"), vmem_limit_bytes=_vmem_limit(est)),
        name="xattn",
    )(x, norm.reshape(1, D_MODEL), w_q, kv, w_o)


def kernel(x, mem, ffn1_norm, ffn1_w_gu, ffn1_w_down, mix_norm, w_in, lb_logits, hg_norm, conv_w,
           conv_norm, w_out, xa_norm, mem_norm, xa_wq, xa_wkv, xa_wo, ffn2_norm, ffn2_w_gu,
           ffn2_w_down, final_norm):
    bsz, t, d = x.shape
    assert d == D_MODEL and t % MIX_ROWS == 0 and (bsz * t) % FFN_ROWS == 0
    assert ffn1_norm.shape[0] == 1, "single-layer block"
    later = [w_in[0], w_out[0], xa_wq[0], xa_wkv[0], xa_wo[0], ffn2_w_gu[0], ffn2_w_down[0]]
    x1, later_bf = _ffn(x.reshape(bsz * t, d), ffn1_norm[0], ffn1_w_gu[0].astype(BF16),
                        ffn1_w_down[0].astype(BF16), final_norm, final_norm=False, cast=later)
    w_in_bf, w_out_bf, wq_bf, wkv_bf, wo_bf, w_gu2_bf, w_down2_bf = later_bf
    x2 = _mix(x1.reshape(bsz, t, d), mix_norm[0], w_in_bf, lb_logits, hg_norm[0], conv_w[0],
              conv_norm[0], w_out_bf)
    kv = _kv(mem.reshape(bsz * N_MEM, d), mem_norm[0], wkv_bf).reshape(bsz, N_MEM, 2 * d)
    x3 = _xattn(x2, xa_norm[0], wq_bf, kv, wo_bf)
    y, _ = _ffn(x3.reshape(bsz * t, d), ffn2_norm[0], w_gu2_bf, w_down2_bf, final_norm,
                final_norm=True)
    return y.reshape(bsz, t, d)
```

```python
import functools
import math

import jax
import jax.numpy as jnp
from jax import lax
from jax.experimental import pallas as pl
from jax.experimental.pallas import tpu as pltpu

D_MODEL = 1024
CHUNK = 64
SUB = 8
DIAG = 4
HG_WIDTH = 512
CONV_WIDTH = 512
HG_HEADS = 4
HG_DK = 128
CONV_K = 3
N_MEM = 256
XA_HEADS = 4
XA_HEAD_DIM = 256
D_FF = 2816
IN_COLS = 4 * HG_WIDTH + 3 * CONV_WIDTH
EPS = 1e-6

V7X_VMEM_BYTES = 64 * 1024 * 1024
MIB = 1024 * 1024

FFN_ROWS = 1024
FFN_SUB = 512
FFN_COLS = 256
MIX_ROWS = 2048
MACRO = 128
PIECE = 512
XA_ROWS = 2048
XA_SUB = 256
KV_ROWS = 256

BF16 = jnp.bfloat16
F32 = jnp.float32
BF16_ROWS = 16


def _vmem_limit(estimate_bytes):
    return int(min(estimate_bytes * 5 // 4 + 8 * MIB, V7X_VMEM_BYTES - 6 * MIB))


def _resident(shape):
    nd = len(shape)
    return pl.BlockSpec(shape, lambda *_: (0,) * nd, pipeline_mode=pl.Buffered(1))


def _rmsnorm(x, g):
    ms = jnp.mean(x * x, axis=-1, keepdims=True)
    return x * lax.rsqrt(ms + EPS) * g


def _silu(x):
    return x * (1.0 / (1.0 + jnp.exp(-x)))


def _dot(a, b):
    return jnp.dot(a, b, preferred_element_type=F32)


def _dot_nt(a, b):
    return lax.dot_general(a, b, (((1,), (1,)), ((), ())), preferred_element_type=F32)


def _dot_tn(a, b):
    return lax.dot_general(a, b, (((0,), (0,)), ((), ())), preferred_element_type=F32)


def _ffn_kernel(x_ref, g_ref, wgu_ref, wd_ref, fin_ref, *refs, final_norm, n_cast):
    cast_in, o_ref, cast_out = refs[:n_cast], refs[n_cast], refs[n_cast + 1:2 * n_cast + 1]
    h_ref, act_ref = refs[2 * n_cast + 1:]
    n_sub = FFN_ROWS // FFN_SUB
    gain = g_ref[...]

    def rows(i):
        return slice(i * FFN_SUB, (i + 1) * FFN_SUB)

    def norm(i):
        h_ref[rows(i), :] = _rmsnorm(x_ref[rows(i), :], gain).astype(BF16)

    def gate_up(i):
        h = h_ref[rows(i), :]
        for c in range(D_FF // FFN_COLS):
            lo = c * FFN_COLS
            gate = _dot(h, wgu_ref[:, lo:lo + FFN_COLS])
            up = _dot(h, wgu_ref[:, D_FF + lo:D_FF + lo + FFN_COLS])
            act_ref[rows(i), lo:lo + FFN_COLS] = (_silu(gate) * up).astype(BF16)

    def down(i):
        y = x_ref[rows(i), :] + 0.5 * _dot(act_ref[rows(i), :], wd_ref[...])
        if final_norm:
            y = _rmsnorm(y, fin_ref[...])
        o_ref[rows(i), :] = y

    norm(0)
    for i in range(n_sub):
        if i + 1 < n_sub:
            norm(i + 1)
        gate_up(i)
        if i >= 1:
            down(i - 1)
    down(n_sub - 1)

    for src, dst in zip(cast_in, cast_out):
        dst[...] = src[...].astype(BF16)


def _ffn(x2d, norm, w_gu, w_down, fin, *, final_norm, cast=()):
    n = x2d.shape[0]
    tm = FFN_ROWS
    steps = n // tm
    slices = [w.shape[0] // steps for w in cast]
    assert all(s % BF16_ROWS == 0 and s * steps == w.shape[0] for s, w in zip(slices, cast))
    est = (4 * tm * D_MODEL * 4 + w_gu.size * 2 + w_down.size * 2 + tm * D_FF * 2
           + tm * D_MODEL * 2 + 4 * FFN_SUB * FFN_COLS * 4
           + 2 * sum(s * w.shape[1] for s, w in zip(slices, cast)) * (4 + 2))
    cast_specs = [pl.BlockSpec((s, w.shape[1]), lambda i: (i, 0)) for s, w in zip(slices, cast)]
    out = pl.pallas_call(
        functools.partial(_ffn_kernel, final_norm=final_norm, n_cast=len(cast)),
        grid=(steps,),
        in_specs=[
            pl.BlockSpec((tm, D_MODEL), lambda i: (i, 0)),
            _resident((1, D_MODEL)),
            _resident((D_MODEL, 2 * D_FF)),
            _resident((D_FF, D_MODEL)),
            _resident((1, D_MODEL)),
        ] + cast_specs,
        out_specs=[pl.BlockSpec((tm, D_MODEL), lambda i: (i, 0))] + cast_specs,
        out_shape=[jax.ShapeDtypeStruct((n, D_MODEL), F32)]
        + [jax.ShapeDtypeStruct(w.shape, BF16) for w in cast],
        scratch_shapes=[pltpu.VMEM((tm, D_MODEL), BF16), pltpu.VMEM((tm, D_FF), BF16)],
        compiler_params=pltpu.CompilerParams(
            dimension_semantics=("arbitrary",), vmem_limit_bytes=_vmem_limit(est)),
        name="ffn_final" if final_norm else "ffn",
    )(x2d, norm.reshape(1, D_MODEL), w_gu, w_down, fin.reshape(1, D_MODEL), *cast)
    return out[0], list(out[1:])


def _chunk_cumsum(g):
    n = g.shape[-1]
    nb = CHUNK // SUB
    row = lax.broadcasted_iota(jnp.int32, (nb, SUB, n), 1)
    s = g.reshape(nb, SUB, n)
    for shift in (1, 2, 4):
        s = s + jnp.where(row >= shift, pltpu.roll(s, shift, 1), 0.0)
    carry = [jnp.zeros((1, 1, n), F32)]
    for i in range(1, nb):
        carry.append(carry[-1] + s[i - 1:i, SUB - 1:SUB, :])
    return (s + jnp.concatenate(carry, axis=0)).reshape(CHUNK, n)


def _hgrn2_masks():
    t = lax.broadcasted_iota(jnp.int32, (CHUNK, CHUNK), 0)
    s = lax.broadcasted_iota(jnp.int32, (CHUNK, CHUNK), 1)
    levels = []
    blk = CHUNK
    while blk > DIAG:
        half = blk // 2
        levels.append((t // blk == s // blk) & (t % blk >= half) & (s % blk < half))
        blk = half
    diag = [(s == t - d) & (t % DIAG >= d) for d in range(DIAG)]
    return levels, diag


def _mix_kernel(x_ref, nrm_ref, win_ref, lbl_ref, hgn_ref, cw_ref, cn_ref, wout_ref, o_ref,
                p_a, p_b, mix_a, mix_b, h_a, h_b, st_ref, ybuf_ref):
    n_macro = MIX_ROWS // MACRO

    @pl.when(pl.program_id(1) == 0)
    def _():
        st_ref[...] = jnp.zeros_like(st_ref)
        ybuf_ref[0:SUB, :] = jnp.zeros((SUB, CONV_WIDTH), F32)

    mix_b[...] = jnp.zeros((MACRO, D_MODEL), BF16)

    lbl = lbl_ref[...]
    lmax = jnp.max(lbl, axis=0, keepdims=True)
    lexp = jnp.exp(lbl - lmax)
    lb_all = lexp[0:1, :] / jnp.sum(lexp, axis=0, keepdims=True)

    level_masks, diag_masks = _hgrn2_masks()
    hgn = hgn_ref[...]
    nrm = nrm_ref[...]
    cw = cw_ref[...]
    cn = cn_ref[...]
    cb = 4 * HG_WIDTH

    def macro_rows(m):
        return pl.ds(pl.multiple_of(m * MACRO, MACRO), MACRO)

    def in_norm(m, h_ref):
        h_ref[...] = _rmsnorm(x_ref[0, macro_rows(m), :], nrm).astype(BF16)

    def in_proj_piece(h_ref, p_ref, c):
        cols = slice(c * PIECE, (c + 1) * PIECE)
        p_ref[:, cols] = _dot(h_ref[...], win_ref[:, cols])

    def out_proj_piece(m, mix_ref, c):
        rows = macro_rows(m)
        cols = slice(c * PIECE, (c + 1) * PIECE)
        o_ref[0, rows, cols] = x_ref[0, rows, cols] + _dot(mix_ref[...], wout_ref[:, cols])


    def hg_ab(p_ref, rows, hh):
        c0 = hh * HG_DK
        q = p_ref[rows, pl.ds(c0, HG_DK)]
        fz = p_ref[rows, pl.ds(HG_WIDTH + c0, HG_DK)]
        v = p_ref[rows, pl.ds(2 * HG_WIDTH + c0, HG_DK)]
        gt = p_ref[rows, pl.ds(3 * HG_WIDTH + c0, HG_DK)]
        lb = lb_all[:, c0:c0 + HG_DK]
        f = lb + (1.0 - lb) * (1.0 / (1.0 + jnp.exp(-fz)))
        k = 1.0 - f
        qs = _silu(q)
        b = _chunk_cumsum(jnp.log2(f))
        b_end = b[CHUNK - 1:CHUNK, :]
        u = dict(hh=hh, rows=rows, v_bf=v.astype(BF16), gate=_silu(gt))
        q_bf, k_bf = qs.astype(BF16), k.astype(BF16)
        ops = []
        blk = CHUNK
        for _ in level_masks:
            half = blk // 2
            nb = CHUNK // blk
            ref = b.reshape(nb, blk, HG_DK)[:, half - 1:half, :]
            ref = jnp.broadcast_to(ref, (nb, blk, HG_DK)).reshape(CHUNK, HG_DK)
            z = jnp.exp2(-jnp.abs(b - ref)).astype(BF16)
            ops.append((q_bf * z, k_bf * z))
            blk = half
        u["level_ops"] = ops
        attn = jnp.where(diag_masks[0], jnp.sum(qs * k, axis=-1, keepdims=True), 0.0)
        f3 = f.reshape(CHUNK // SUB, SUB, HG_DK)
        decay = f
        for d in range(1, DIAG):
            f_d = pltpu.roll(f3, d, 1).reshape(CHUNK, HG_DK)
            r_d = jnp.sum(qs * (1.0 - f_d) * decay, axis=-1, keepdims=True)
            attn = jnp.where(diag_masks[d], r_d, attn)
            decay = decay * f_d
        u["attn_diag"] = attn
        u["qe"] = (qs * jnp.exp2(b)).astype(BF16)
        u["ke"] = (k * jnp.exp2(b_end - b)).astype(BF16)
        u["decay"] = jnp.exp2(b_end)
        return u

    def hg_mb(u):
        u["level_out"] = [_dot_nt(ql, kl) for ql, kl in u.pop("level_ops")]

    def hg_c(u):
        attn = u.pop("attn_diag")
        for lm, a in zip(level_masks, u.pop("level_out")):
            attn = jnp.where(lm, a, attn)
        u["attn"] = attn.astype(BF16)

    def hg_mc(u):
        hh, v_bf = u["hh"], u.pop("v_bf")
        st = st_ref[hh]
        o = _dot(u.pop("attn"), v_bf)
        u["o"] = o + _dot(u.pop("qe"), st.astype(BF16))
        decay_col = jnp.transpose(jnp.broadcast_to(u.pop("decay"), (HG_DK, HG_DK)))
        st_ref[hh] = st * decay_col + _dot_tn(u.pop("ke"), v_bf)

    def hg_d(u, mix_ref):
        on = _rmsnorm(u.pop("o"), hgn) * u.pop("gate")
        mix_ref[u["rows"], pl.ds(u["hh"] * HG_DK, HG_DK)] = on.astype(BF16)

    def short_conv(p_ref, mix_ref, rows):
        y = p_ref[rows, cb + CONV_WIDTH:cb + 2 * CONV_WIDTH] * \
            p_ref[rows, cb + 2 * CONV_WIDTH:cb + 3 * CONV_WIDTH]
        ybuf_ref[SUB:SUB + CHUNK, :] = y
        y1 = ybuf_ref[SUB - 1:SUB - 1 + CHUNK, :]
        y2 = ybuf_ref[SUB - 2:SUB - 2 + CHUNK, :]
        conv = cw[0:1, :] * y2 + cw[1:2, :] * y1 + cw[2:3, :] * y
        ocv = _rmsnorm(p_ref[rows, cb:cb + CONV_WIDTH] * conv, cn)
        mix_ref[rows, HG_WIDTH:HG_WIDTH + CONV_WIDTH] = ocv.astype(BF16)
        ybuf_ref[0:SUB, :] = ybuf_ref[CHUNK:CHUNK + SUB, :]

    def each(phase, units, *args):
        for u in units:
            phase(u, *args)

    n_chunks = MACRO // CHUNK
    chunk_rows = [pl.ds(j * CHUNK, CHUNK) for j in range(n_chunks)]
    heads = range(HG_HEADS)
    last = n_macro - 1

    in_norm(0, h_b)
    for c in range(IN_COLS // PIECE):
        in_proj_piece(h_b, p_a, c)
    in_norm(1, h_a)

    def trip(j, carry):
        m = 2 * j
        big_a = [functools.partial(in_proj_piece, h_a, p_b, c) for c in range(IN_COLS // PIECE)]
        big_a += [functools.partial(out_proj_piece, jnp.maximum(m - 1, 0), mix_b, c)
                  for c in range(D_MODEL // PIECE)]
        big_b = [functools.partial(in_proj_piece, h_b, p_a, c) for c in range(IN_COLS // PIECE)]
        big_b += [functools.partial(out_proj_piece, m, mix_a, c) for c in range(D_MODEL // PIECE)]
        groups = [(p_a, mix_a, r) for r in chunk_rows] + [(p_b, mix_b, r) for r in chunk_rows]
        ng = len(groups)
        plan = {i: [] for i in range(-1, ng + 1)}
        for pieces, slots in ((big_a, range(-1, n_chunks + 1)), (big_b, range(n_chunks + 1, ng + 1))):
            for k, piece in enumerate(pieces):
                plan[slots[k * len(slots) // len(pieces)]].append(piece)

        def issue(slot):
            for piece in plan[slot]:
                piece()

        issue(-1)
        units = []
        for i in range(ng + 1):
            if i < ng:
                p_ref, _, rows = groups[i]
                units.append([hg_ab(p_ref, rows, hh) for hh in heads])
            if i >= 1:
                short_conv(*groups[i - 1])
                each(hg_c, units[i - 1])
            if i >= 2:
                each(hg_d, units[i - 2], groups[i - 2][1])
            if i == n_chunks - 1:
                in_norm(jnp.minimum(m + 2, last), h_b)
            if i == ng - 1:
                in_norm(jnp.minimum(m + 3, last), h_a)
            if i < ng:
                each(hg_mb, units[i])
            if i >= 1:
                each(hg_mc, units[i - 1])
            issue(i)
        each(hg_d, units[ng - 1], groups[ng - 1][1])
        return carry

    lax.fori_loop(0, n_macro // 2, trip, 0)
    for c in range(D_MODEL // PIECE):
        out_proj_piece(last, mix_b, c)


def _mix(x, norm, w_in, lb_logits, hg_norm, conv_w, conv_norm, w_out):
    bsz, t, _ = x.shape
    tt = MIX_ROWS
    est = (4 * tt * D_MODEL * 4 + w_in.size * 2 + w_out.size * 2 + 2 * MACRO * IN_COLS * 4
           + HG_HEADS * HG_DK * HG_DK * 4 + (CHUNK + SUB) * CONV_WIDTH * 4
           + 4 * MACRO * D_MODEL * 2)
    return pl.pallas_call(
        _mix_kernel,
        grid=(bsz, t // tt),
        in_specs=[
            pl.BlockSpec((1, tt, D_MODEL), lambda b, i: (b, i, 0)),
            _resident((1, D_MODEL)),
            _resident((D_MODEL, IN_COLS)),
            _resident(lb_logits.shape),
            _resident((1, HG_DK)),
            _resident((CONV_K, CONV_WIDTH)),
            _resident((1, CONV_WIDTH)),
            _resident((D_MODEL, D_MODEL)),
        ],
        out_specs=pl.BlockSpec((1, tt, D_MODEL), lambda b, i: (b, i, 0)),
        out_shape=jax.ShapeDtypeStruct(x.shape, F32),
        scratch_shapes=[
            pltpu.VMEM((MACRO, IN_COLS), F32),
            pltpu.VMEM((MACRO, IN_COLS), F32),
            pltpu.VMEM((MACRO, D_MODEL), BF16),
            pltpu.VMEM((MACRO, D_MODEL), BF16),
            pltpu.VMEM((MACRO, D_MODEL), BF16),
            pltpu.VMEM((MACRO, D_MODEL), BF16),
            pltpu.VMEM((HG_HEADS, HG_DK, HG_DK), F32),
            pltpu.VMEM((CHUNK + SUB, CONV_WIDTH), F32),
        ],
        compiler_params=pltpu.CompilerParams(
            dimension_semantics=("arbitrary", "arbitrary"), vmem_limit_bytes=_vmem_limit(est)),
        name="mix",
    )(x, norm.reshape(1, D_MODEL), w_in, lb_logits, hg_norm.reshape(1, HG_DK), conv_w,
      conv_norm.reshape(1, CONV_WIDTH), w_out)


def _kv_kernel(m_ref, g_ref, w_ref, o_ref):
    h = _rmsnorm(m_ref[...], g_ref[...]).astype(BF16)
    o_ref[...] = _dot(h, w_ref[...]).astype(BF16)


def _kv(mem2d, norm, w_kv):
    n = mem2d.shape[0]
    est = 2 * KV_ROWS * D_MODEL * 4 + w_kv.size * 2 + 2 * KV_ROWS * 2 * D_MODEL * 2
    return pl.pallas_call(
        _kv_kernel,
        grid=(n // KV_ROWS,),
        in_specs=[
            pl.BlockSpec((KV_ROWS, D_MODEL), lambda i: (i, 0)),
            _resident((1, D_MODEL)),
            _resident((D_MODEL, 2 * D_MODEL)),
        ],
        out_specs=pl.BlockSpec((KV_ROWS, 2 * D_MODEL), lambda i: (i, 0)),
        out_shape=jax.ShapeDtypeStruct((n, 2 * D_MODEL), BF16),
        compiler_params=pltpu.CompilerParams(
            dimension_semantics=("arbitrary",), vmem_limit_bytes=_vmem_limit(est)),
        name="kv",
    )(mem2d, norm.reshape(1, D_MODEL), w_kv)


def _xattn_kernel(x_ref, g_ref, wq_ref, kv_ref, wo_ref, o_ref, h_ref, q_ref, s_ref, p_ref,
                  att_ref):
    n_sub = XA_ROWS // XA_SUB
    scale = 1.0 / math.sqrt(XA_HEAD_DIM)
    gain = g_ref[...]
    heads = [slice(hh * XA_HEAD_DIM, (hh + 1) * XA_HEAD_DIM) for hh in range(XA_HEADS)]

    def rows(i):
        return slice(i * XA_SUB, (i + 1) * XA_SUB)

    def norm(i):
        h_ref[i % 2] = _rmsnorm(x_ref[0, rows(i), :], gain).astype(BF16)

    def q_proj(i):
        for hd in heads:
            q_ref[i % 2, :, hd] = (_dot(h_ref[i % 2], wq_ref[:, hd]) * scale).astype(BF16)

    def scores(i):
        for hd in heads:
            s_ref[i % 2, :, hd] = _dot_nt(q_ref[i % 2, :, hd], kv_ref[0, :, hd])

    def softmax(i):
        for hd in heads:
            s = s_ref[i % 2, :, hd]
            e = jnp.exp(s - jnp.max(s, axis=-1, keepdims=True))
            p_ref[i % 2, :, hd] = (e * (1.0 / jnp.sum(e, axis=-1, keepdims=True))).astype(BF16)

    def pv(i):
        for hh, hd in enumerate(heads):
            v = kv_ref[0, :, D_MODEL + hh * XA_HEAD_DIM:D_MODEL + (hh + 1) * XA_HEAD_DIM]
            att_ref[i % 2, :, hd] = _dot(p_ref[i % 2, :, hd], v).astype(BF16)

    def out_proj(i):
        o_ref[0, rows(i), :] = x_ref[0, rows(i), :] + _dot(att_ref[i % 2], wo_ref[...])

    norm(0)
    if n_sub > 1:
        norm(1)
    q_proj(0)
    scores(0)
    for i in range(n_sub):
        if i + 1 < n_sub:
            q_proj(i + 1)
        softmax(i)
        pv(i)
        if i + 1 < n_sub:
            scores(i + 1)
        if i + 2 < n_sub:
            norm(i + 2)
        out_proj(i)


def _xattn(x, norm, w_q, kv, w_o):
    bsz, t, _ = x.shape
    tm = XA_ROWS
    est = (4 * tm * D_MODEL * 4 + w_q.size * 2 + w_o.size * 2 + 2 * N_MEM * 2 * D_MODEL * 2
           + 2 * XA_SUB * D_MODEL * (4 + 4 * 2))
    return pl.pallas_call(
        _xattn_kernel,
        grid=(bsz, t // tm),
        in_specs=[
            pl.BlockSpec((1, tm, D_MODEL), lambda b, i: (b, i, 0)),
            _resident((1, D_MODEL)),
            _resident((D_MODEL, D_MODEL)),
            pl.BlockSpec((1, N_MEM, 2 * D_MODEL), lambda b, i: (b, 0, 0)),
            _resident((D_MODEL, D_MODEL)),
        ],
        out_specs=pl.BlockSpec((1, tm, D_MODEL), lambda b, i: (b, i, 0)),
        out_shape=jax.ShapeDtypeStruct(x.shape, F32),
        scratch_shapes=[
            pltpu.VMEM((2, XA_SUB, D_MODEL), BF16),
            pltpu.VMEM((2, XA_SUB, D_MODEL), BF16),
            pltpu.VMEM((2, XA_SUB, D_MODEL), F32),
            pltpu.VMEM((2, XA_SUB, D_MODEL), BF16),
            pltpu.VMEM((2, XA_SUB, D_MODEL), BF16),
        ],
        compiler_params=pltpu.CompilerParams(
            dimension_semantics=("arbitrary", "arbitrary"), vmem_limit_bytes=_vmem_limit(est)),
        name="xattn",
    )(x, norm.reshape(1, D_MODEL), w_q, kv, w_o)


def kernel(x, mem, ffn1_norm, ffn1_w_gu, ffn1_w_down, mix_norm, w_in, lb_logits, hg_norm, conv_w,
           conv_norm, w_out, xa_norm, mem_norm, xa_wq, xa_wkv, xa_wo, ffn2_norm, ffn2_w_gu,
           ffn2_w_down, final_norm):
    bsz, t, d = x.shape
    assert d == D_MODEL and t % MIX_ROWS == 0 and (bsz * t) % FFN_ROWS == 0
    assert ffn1_norm.shape[0] == 1, "single-layer block"
    later = [w_in[0], w_out[0], xa_wq[0], xa_wkv[0], xa_wo[0], ffn2_w_gu[0], ffn2_w_down[0]]
    x1, later_bf = _ffn(x.reshape(bsz * t, d), ffn1_norm[0], ffn1_w_gu[0].astype(BF16),
                        ffn1_w_down[0].astype(BF16), final_norm, final_norm=False, cast=later)
    w_in_bf, w_out_bf, wq_bf, wkv_bf, wo_bf, w_gu2_bf, w_down2_bf = later_bf
    x2 = _mix(x1.reshape(bsz, t, d), mix_norm[0], w_in_bf, lb_logits, hg_norm[0], conv_w[0],
              conv_norm[0], w_out_bf)
    kv = _kv(mem.reshape(bsz * N_MEM, d), mem_norm[0], wkv_bf).reshape(bsz, N_MEM, 2 * d)
    x3 = _xattn(x2, xa_norm[0], wq_bf, kv, wo_bf)
    y, _ = _ffn(x3.reshape(bsz * t, d), ffn2_norm[0], w_gu2_bf, w_down2_bf, final_norm,
                final_norm=True)
    return y.reshape(bsz, t, d)
```

```python
import functools
import math

import jax
import jax.numpy as jnp
from jax import lax
from jax.experimental import pallas as pl
from jax.experimental.pallas import tpu as pltpu

D_MODEL = 1024
CHUNK = 64
SUB = 8
DIAG = 4
HG_WIDTH = 512
CONV_WIDTH = 512
HG_HEADS = 4
HG_DK = 128
CONV_K = 3
N_MEM = 256
XA_HEADS = 4
XA_HEAD_DIM = 256
D_FF = 2816
IN_COLS = 4 * HG_WIDTH + 3 * CONV_WIDTH
EPS = 1e-6

V7X_VMEM_BYTES = 64 * 1024 * 1024
MIB = 1024 * 1024

FFN_ROWS = 1024
FFN_SUB = 1024
FFN_COLS = 256
MIX_ROWS = 2048
MACRO = 128
PIECE = 512
XA_ROWS = 2048
XA_SUB = 256
KV_ROWS = 256

BF16 = jnp.bfloat16
F32 = jnp.float32
BF16_ROWS = 16


def _vmem_limit(estimate_bytes):
    return int(min(estimate_bytes * 5 // 4 + 8 * MIB, V7X_VMEM_BYTES - 6 * MIB))


def _resident(shape):
    nd = len(shape)
    return pl.BlockSpec(shape, lambda *_: (0,) * nd, pipeline_mode=pl.Buffered(1))


def _rmsnorm(x, g):
    ms = jnp.mean(x * x, axis=-1, keepdims=True)
    return x * lax.rsqrt(ms + EPS) * g


def _silu(x):
    return x * (1.0 / (1.0 + jnp.exp(-x)))


def _dot(a, b):
    return jnp.dot(a, b, preferred_element_type=F32)


def _dot_nt(a, b):
    return lax.dot_general(a, b, (((1,), (1,)), ((), ())), preferred_element_type=F32)


def _dot_tn(a, b):
    return lax.dot_general(a, b, (((0,), (0,)), ((), ())), preferred_element_type=F32)


def _ffn_kernel(x_ref, g_ref, wgu_ref, wd_ref, fin_ref, *refs, final_norm, n_cast):
    cast_in, o_ref, cast_out = refs[:n_cast], refs[n_cast], refs[n_cast + 1:2 * n_cast + 1]
    h_ref, act_ref = refs[2 * n_cast + 1:]
    n_sub = FFN_ROWS // FFN_SUB
    gain = g_ref[...]

    def rows(i):
        return slice(i * FFN_SUB, (i + 1) * FFN_SUB)

    def norm(i):
        h_ref[rows(i), :] = _rmsnorm(x_ref[rows(i), :], gain).astype(BF16)

    def gate_up(i):
        h = h_ref[rows(i), :]
        for c in range(D_FF // FFN_COLS):
            lo = c * FFN_COLS
            gate = _dot(h, wgu_ref[:, lo:lo + FFN_COLS])
            up = _dot(h, wgu_ref[:, D_FF + lo:D_FF + lo + FFN_COLS])
            act_ref[rows(i), lo:lo + FFN_COLS] = (_silu(gate) * up).astype(BF16)

    def down(i):
        y = x_ref[rows(i), :] + 0.5 * _dot(act_ref[rows(i), :], wd_ref[...])
        if final_norm:
            y = _rmsnorm(y, fin_ref[...])
        o_ref[rows(i), :] = y

    norm(0)
    for i in range(n_sub):
        if i + 1 < n_sub:
            norm(i + 1)
        gate_up(i)
        if i >= 1:
            down(i - 1)
    down(n_sub - 1)

    for src, dst in zip(cast_in, cast_out):
        dst[...] = src[...].astype(BF16)


def _ffn(x2d, norm, w_gu, w_down, fin, *, final_norm, cast=()):
    n = x2d.shape[0]
    tm = FFN_ROWS
    steps = n // tm
    slices = [w.shape[0] // steps for w in cast]
    assert all(s % BF16_ROWS == 0 and s * steps == w.shape[0] for s, w in zip(slices, cast))
    est = (4 * tm * D_MODEL * 4 + w_gu.size * 2 + w_down.size * 2 + tm * D_FF * 2
           + tm * D_MODEL * 2 + 4 * FFN_SUB * FFN_COLS * 4
           + 2 * sum(s * w.shape[1] for s, w in zip(slices, cast)) * (4 + 2))
    cast_specs = [pl.BlockSpec((s, w.shape[1]), lambda i: (i, 0)) for s, w in zip(slices, cast)]
    out = pl.pallas_call(
        functools.partial(_ffn_kernel, final_norm=final_norm, n_cast=len(cast)),
        grid=(steps,),
        in_specs=[
            pl.BlockSpec((tm, D_MODEL), lambda i: (i, 0)),
            _resident((1, D_MODEL)),
            _resident((D_MODEL, 2 * D_FF)),
            _resident((D_FF, D_MODEL)),
            _resident((1, D_MODEL)),
        ] + cast_specs,
        out_specs=[pl.BlockSpec((tm, D_MODEL), lambda i: (i, 0))] + cast_specs,
        out_shape=[jax.ShapeDtypeStruct((n, D_MODEL), F32)]
        + [jax.ShapeDtypeStruct(w.shape, BF16) for w in cast],
        scratch_shapes=[pltpu.VMEM((tm, D_MODEL), BF16), pltpu.VMEM((tm, D_FF), BF16)],
        compiler_params=pltpu.CompilerParams(
            dimension_semantics=("arbitrary",), vmem_limit_bytes=_vmem_limit(est)),
        name="ffn_final" if final_norm else "ffn",
    )(x2d, norm.reshape(1, D_MODEL), w_gu, w_down, fin.reshape(1, D_MODEL), *cast)
    return out[0], list(out[1:])


def _chunk_cumsum(g):
    n = g.shape[-1]
    nb = CHUNK // SUB
    row = lax.broadcasted_iota(jnp.int32, (nb, SUB, n), 1)
    s = g.reshape(nb, SUB, n)
    for shift in (1, 2, 4):
        s = s + jnp.where(row >= shift, pltpu.roll(s, shift, 1), 0.0)
    carry = [jnp.zeros((1, 1, n), F32)]
    for i in range(1, nb):
        carry.append(carry[-1] + s[i - 1:i, SUB - 1:SUB, :])
    return (s + jnp.concatenate(carry, axis=0)).reshape(CHUNK, n)


def _hgrn2_masks():
    t = lax.broadcasted_iota(jnp.int32, (CHUNK, CHUNK), 0)
    s = lax.broadcasted_iota(jnp.int32, (CHUNK, CHUNK), 1)
    levels = []
    blk = CHUNK
    while blk > DIAG:
        half = blk // 2
        levels.append((t // blk == s // blk) & (t % blk >= half) & (s % blk < half))
        blk = half
    diag = [(s == t - d) & (t % DIAG >= d) for d in range(DIAG)]
    return levels, diag


def _mix_kernel(x_ref, nrm_ref, win_ref, lbl_ref, hgn_ref, cw_ref, cn_ref, wout_ref, o_ref,
                p_a, p_b, mix_a, mix_b, h_a, h_b, st_ref, ybuf_ref):
    n_macro = MIX_ROWS // MACRO

    @pl.when(pl.program_id(1) == 0)
    def _():
        st_ref[...] = jnp.zeros_like(st_ref)
        ybuf_ref[0:SUB, :] = jnp.zeros((SUB, CONV_WIDTH), F32)

    mix_b[...] = jnp.zeros((MACRO, D_MODEL), BF16)

    lbl = lbl_ref[...]
    lmax = jnp.max(lbl, axis=0, keepdims=True)
    lexp = jnp.exp(lbl - lmax)
    lb_all = lexp[0:1, :] / jnp.sum(lexp, axis=0, keepdims=True)

    level_masks, diag_masks = _hgrn2_masks()
    hgn = hgn_ref[...]
    nrm = nrm_ref[...]
    cw = cw_ref[...]
    cn = cn_ref[...]
    cb = 4 * HG_WIDTH

    def macro_rows(m):
        return pl.ds(pl.multiple_of(m * MACRO, MACRO), MACRO)

    def in_norm(m, h_ref):
        h_ref[...] = _rmsnorm(x_ref[0, macro_rows(m), :], nrm).astype(BF16)

    def in_proj_piece(h_ref, p_ref, c):
        cols = slice(c * PIECE, (c + 1) * PIECE)
        p_ref[:, cols] = _dot(h_ref[...], win_ref[:, cols])

    def out_proj_piece(m, mix_ref, c):
        rows = macro_rows(m)
        cols = slice(c * PIECE, (c + 1) * PIECE)
        o_ref[0, rows, cols] = x_ref[0, rows, cols] + _dot(mix_ref[...], wout_ref[:, cols])


    def hg_ab(p_ref, rows, hh):
        c0 = hh * HG_DK
        q = p_ref[rows, pl.ds(c0, HG_DK)]
        fz = p_ref[rows, pl.ds(HG_WIDTH + c0, HG_DK)]
        v = p_ref[rows, pl.ds(2 * HG_WIDTH + c0, HG_DK)]
        lb = lb_all[:, c0:c0 + HG_DK]
        f = lb + (1.0 - lb) * (1.0 / (1.0 + jnp.exp(-fz)))
        k = 1.0 - f
        qs = _silu(q)
        b = _chunk_cumsum(jnp.log2(f))
        b_end = b[CHUNK - 1:CHUNK, :]
        u = dict(hh=hh, rows=rows, p_ref=p_ref, v_bf=v.astype(BF16))
        q_bf, k_bf = qs.astype(BF16), k.astype(BF16)
        ops = []
        blk = CHUNK
        for _ in level_masks:
            half = blk // 2
            nb = CHUNK // blk
            ref = b.reshape(nb, blk, HG_DK)[:, half - 1:half, :]
            ref = jnp.broadcast_to(ref, (nb, blk, HG_DK)).reshape(CHUNK, HG_DK)
            z = jnp.exp2(-jnp.abs(b - ref)).astype(BF16)
            ops.append((q_bf * z, k_bf * z))
            blk = half
        u["level_ops"] = ops
        attn = jnp.where(diag_masks[0], jnp.sum(qs * k, axis=-1, keepdims=True), 0.0)
        f3 = f.reshape(CHUNK // SUB, SUB, HG_DK)
        decay = f
        for d in range(1, DIAG):
            f_d = pltpu.roll(f3, d, 1).reshape(CHUNK, HG_DK)
            r_d = jnp.sum(qs * (1.0 - f_d) * decay, axis=-1, keepdims=True)
            attn = jnp.where(diag_masks[d], r_d, attn)
            decay = decay * f_d
        u["attn_diag"] = attn
        u["qe"] = (qs * jnp.exp2(b)).astype(BF16)
        u["ke"] = (k * jnp.exp2(b_end - b)).astype(BF16)
        u["decay"] = jnp.exp2(b_end)
        return u

    def hg_mb(u):
        u["level_out"] = [_dot_nt(ql, kl) for ql, kl in u.pop("level_ops")]

    def hg_c(u):
        attn = u.pop("attn_diag")
        for lm, a in zip(level_masks, u.pop("level_out")):
            attn = jnp.where(lm, a, attn)
        u["attn"] = attn.astype(BF16)

    def hg_mc(u):
        hh, v_bf = u["hh"], u.pop("v_bf")
        st = st_ref[hh]
        o = _dot(u.pop("attn"), v_bf)
        u["o"] = o + _dot(u.pop("qe"), st.astype(BF16))
        decay_col = jnp.transpose(jnp.broadcast_to(u.pop("decay"), (HG_DK, HG_DK)))
        st_ref[hh] = st * decay_col + _dot_tn(u.pop("ke"), v_bf)

    def hg_d(u, mix_ref):
        c0 = u["hh"] * HG_DK
        gt = u["p_ref"][u["rows"], pl.ds(3 * HG_WIDTH + c0, HG_DK)]
        on = _rmsnorm(u.pop("o"), hgn) * _silu(gt)
        mix_ref[u["rows"], pl.ds(c0, HG_DK)] = on.astype(BF16)

    def short_conv(p_ref, mix_ref, rows):
        y = p_ref[rows, cb + CONV_WIDTH:cb + 2 * CONV_WIDTH] * \
            p_ref[rows, cb + 2 * CONV_WIDTH:cb + 3 * CONV_WIDTH]
        ybuf_ref[SUB:SUB + CHUNK, :] = y
        y1 = ybuf_ref[SUB - 1:SUB - 1 + CHUNK, :]
        y2 = ybuf_ref[SUB - 2:SUB - 2 + CHUNK, :]
        conv = cw[0:1, :] * y2 + cw[1:2, :] * y1 + cw[2:3, :] * y
        ocv = _rmsnorm(p_ref[rows, cb:cb + CONV_WIDTH] * conv, cn)
        mix_ref[rows, HG_WIDTH:HG_WIDTH + CONV_WIDTH] = ocv.astype(BF16)
        ybuf_ref[0:SUB, :] = ybuf_ref[CHUNK:CHUNK + SUB, :]

    def each(phase, units, *args):
        for u in units:
            phase(u, *args)

    n_chunks = MACRO // CHUNK
    chunk_rows = [pl.ds(j * CHUNK, CHUNK) for j in range(n_chunks)]
    heads = range(HG_HEADS)
    last = n_macro - 1

    in_norm(0, h_b)
    for c in range(IN_COLS // PIECE):
        in_proj_piece(h_b, p_a, c)
    in_norm(1, h_a)

    def trip(j, carry):
        m = 2 * j
        big_a = [functools.partial(in_proj_piece, h_a, p_b, c) for c in range(IN_COLS // PIECE)]
        big_a += [functools.partial(out_proj_piece, jnp.maximum(m - 1, 0), mix_b, c)
                  for c in range(D_MODEL // PIECE)]
        big_b = [functools.partial(in_proj_piece, h_b, p_a, c) for c in range(IN_COLS // PIECE)]
        big_b += [functools.partial(out_proj_piece, m, mix_a, c) for c in range(D_MODEL // PIECE)]
        groups = [(p_a, mix_a, r) for r in chunk_rows] + [(p_b, mix_b, r) for r in chunk_rows]
        ng = len(groups)
        plan = {i: [] for i in range(-1, ng + 1)}
        for pieces, slots in ((big_a, range(-1, n_chunks + 1)), (big_b, range(n_chunks + 1, ng + 1))):
            for k, piece in enumerate(pieces):
                plan[slots[k * len(slots) // len(pieces)]].append(piece)

        def issue(slot):
            for piece in plan[slot]:
                piece()

        issue(-1)
        units = []
        for i in range(ng + 1):
            if i < ng:
                p_ref, _, rows = groups[i]
                units.append([hg_ab(p_ref, rows, hh) for hh in heads])
            if i >= 1:
                short_conv(*groups[i - 1])
                each(hg_c, units[i - 1])
            if i >= 2:
                each(hg_d, units[i - 2], groups[i - 2][1])
            if i == n_chunks - 1:
                in_norm(jnp.minimum(m + 2, last), h_b)
            if i == ng - 1:
                in_norm(jnp.minimum(m + 3, last), h_a)
            if i < ng:
                each(hg_mb, units[i])
            if i >= 1:
                each(hg_mc, units[i - 1])
            issue(i)
        each(hg_d, units[ng - 1], groups[ng - 1][1])
        return carry

    lax.fori_loop(0, n_macro // 2, trip, 0)
    for c in range(D_MODEL // PIECE):
        out_proj_piece(last, mix_b, c)


def _mix(x, norm, w_in, lb_logits, hg_norm, conv_w, conv_norm, w_out):
    bsz, t, _ = x.shape
    tt = MIX_ROWS
    est = (4 * tt * D_MODEL * 4 + w_in.size * 2 + w_out.size * 2 + 2 * MACRO * IN_COLS * 4
           + HG_HEADS * HG_DK * HG_DK * 4 + (CHUNK + SUB) * CONV_WIDTH * 4
           + 4 * MACRO * D_MODEL * 2)
    return pl.pallas_call(
        _mix_kernel,
        grid=(bsz, t // tt),
        in_specs=[
            pl.BlockSpec((1, tt, D_MODEL), lambda b, i: (b, i, 0)),
            _resident((1, D_MODEL)),
            _resident((D_MODEL, IN_COLS)),
            _resident(lb_logits.shape),
            _resident((1, HG_DK)),
            _resident((CONV_K, CONV_WIDTH)),
            _resident((1, CONV_WIDTH)),
            _resident((D_MODEL, D_MODEL)),
        ],
        out_specs=pl.BlockSpec((1, tt, D_MODEL), lambda b, i: (b, i, 0)),
        out_shape=jax.ShapeDtypeStruct(x.shape, F32),
        scratch_shapes=[
            pltpu.VMEM((MACRO, IN_COLS), F32),
            pltpu.VMEM((MACRO, IN_COLS), F32),
            pltpu.VMEM((MACRO, D_MODEL), BF16),
            pltpu.VMEM((MACRO, D_MODEL), BF16),
            pltpu.VMEM((MACRO, D_MODEL), BF16),
            pltpu.VMEM((MACRO, D_MODEL), BF16),
            pltpu.VMEM((HG_HEADS, HG_DK, HG_DK), F32),
            pltpu.VMEM((CHUNK + SUB, CONV_WIDTH), F32),
        ],
        compiler_params=pltpu.CompilerParams(
            dimension_semantics=("arbitrary", "arbitrary"), vmem_limit_bytes=_vmem_limit(est)),
        name="mix",
    )(x, norm.reshape(1, D_MODEL), w_in, lb_logits, hg_norm.reshape(1, HG_DK), conv_w,
      conv_norm.reshape(1, CONV_WIDTH), w_out)


def _kv_kernel(m_ref, g_ref, w_ref, o_ref):
    h = _rmsnorm(m_ref[...], g_ref[...]).astype(BF16)
    o_ref[...] = _dot(h, w_ref[...]).astype(BF16)


def _kv(mem2d, norm, w_kv):
    n = mem2d.shape[0]
    est = 2 * KV_ROWS * D_MODEL * 4 + w_kv.size * 2 + 2 * KV_ROWS * 2 * D_MODEL * 2
    return pl.pallas_call(
        _kv_kernel,
        grid=(n // KV_ROWS,),
        in_specs=[
            pl.BlockSpec((KV_ROWS, D_MODEL), lambda i: (i, 0)),
            _resident((1, D_MODEL)),
            _resident((D_MODEL, 2 * D_MODEL)),
        ],
        out_specs=pl.BlockSpec((KV_ROWS, 2 * D_MODEL), lambda i: (i, 0)),
        out_shape=jax.ShapeDtypeStruct((n, 2 * D_MODEL), BF16),
        compiler_params=pltpu.CompilerParams(
            dimension_semantics=("arbitrary",), vmem_limit_bytes=_vmem_limit(est)),
        name="kv",
    )(mem2d, norm.reshape(1, D_MODEL), w_kv)


def _xattn_kernel(x_ref, g_ref, wq_ref, kv_ref, wo_ref, o_ref, h_ref, q_ref, s_ref, p_ref,
                  att_ref):
    n_sub = XA_ROWS // XA_SUB
    scale = 1.0 / math.sqrt(XA_HEAD_DIM)
    gain = g_ref[...]
    heads = [slice(hh * XA_HEAD_DIM, (hh + 1) * XA_HEAD_DIM) for hh in range(XA_HEADS)]

    def rows(i):
        return slice(i * XA_SUB, (i + 1) * XA_SUB)

    def norm(i):
        h_ref[i % 2] = _rmsnorm(x_ref[0, rows(i), :], gain).astype(BF16)

    def q_proj(i):
        for hd in heads:
            q_ref[i % 2, :, hd] = (_dot(h_ref[i % 2], wq_ref[:, hd]) * scale).astype(BF16)

    def scores(i):
        for hd in heads:
            s_ref[i % 2, :, hd] = _dot_nt(q_ref[i % 2, :, hd], kv_ref[0, :, hd])

    def softmax(i):
        for hd in heads:
            s = s_ref[i % 2, :, hd]
            e = jnp.exp(s - jnp.max(s, axis=-1, keepdims=True))
            p_ref[i % 2, :, hd] = (e * (1.0 / jnp.sum(e, axis=-1, keepdims=True))).astype(BF16)

    def pv(i):
        for hh, hd in enumerate(heads):
            v = kv_ref[0, :, D_MODEL + hh * XA_HEAD_DIM:D_MODEL + (hh + 1) * XA_HEAD_DIM]
            att_ref[i % 2, :, hd] = _dot(p_ref[i % 2, :, hd], v).astype(BF16)

    def out_proj(i):
        o_ref[0, rows(i), :] = x_ref[0, rows(i), :] + _dot(att_ref[i % 2], wo_ref[...])

    norm(0)
    if n_sub > 1:
        norm(1)
    q_proj(0)
    scores(0)
    for i in range(n_sub):
        if i + 1 < n_sub:
            q_proj(i + 1)
        softmax(i)
        pv(i)
        if i + 1 < n_sub:
            scores(i + 1)
        if i + 2 < n_sub:
            norm(i + 2)
        out_proj(i)


def _xattn(x, norm, w_q, kv, w_o):
    bsz, t, _ = x.shape
    tm = XA_ROWS
    est = (4 * tm * D_MODEL * 4 + w_q.size * 2 + w_o.size * 2 + 2 * N_MEM * 2 * D_MODEL * 2
           + 2 * XA_SUB * D_MODEL * (4 + 4 * 2))
    return pl.pallas_call(
        _xattn_kernel,
        grid=(bsz, t // tm),
        in_specs=[
            pl.BlockSpec((1, tm, D_MODEL), lambda b, i: (b, i, 0)),
            _resident((1, D_MODEL)),
            _resident((D_MODEL, D_MODEL)),
            pl.BlockSpec((1, N_MEM, 2 * D_MODEL), lambda b, i: (b, 0, 0)),
            _resident((D_MODEL, D_MODEL)),
        ],
        out_specs=pl.BlockSpec((1, tm, D_MODEL), lambda b, i: (b, i, 0)),
        out_shape=jax.ShapeDtypeStruct(x.shape, F32),
        scratch_shapes=[
            pltpu.VMEM((2, XA_SUB, D_MODEL), BF16),
            pltpu.VMEM((2, XA_SUB, D_MODEL), BF16),
            pltpu.VMEM((2, XA_SUB, D_MODEL), F32),
            pltpu.VMEM((2, XA_SUB, D_MODEL), BF16),
            pltpu.VMEM((2, XA_SUB, D_MODEL), BF16),
        ],
        compiler_params=pltpu.CompilerParams(
            dimension_semantics=("arbitrary", "arbitrary"), vmem_limit_bytes=_vmem_limit(est)),
        name="xattn",
    )(x, norm.reshape(1, D_MODEL), w_q, kv, w_o)


def kernel(x, mem, ffn1_norm, ffn1_w_gu, ffn1_w_down, mix_norm, w_in, lb_logits, hg_norm, conv_w,
           conv_norm, w_out, xa_norm, mem_norm, xa_wq, xa_wkv, xa_wo, ffn2_norm, ffn2_w_gu,
           ffn2_w_down, final_norm):
    bsz, t, d = x.shape
    assert d == D_MODEL and t % MIX_ROWS == 0 and (bsz * t) % FFN_ROWS == 0
    assert ffn1_norm.shape[0] == 1, "single-layer block"
    later = [w_in[0], w_out[0], xa_wq[0], xa_wkv[0], xa_wo[0], ffn2_w_gu[0], ffn2_w_down[0]]
    x1, later_bf = _ffn(x.reshape(bsz * t, d), ffn1_norm[0], ffn1_w_gu[0].astype(BF16),
                        ffn1_w_down[0].astype(BF16), final_norm, final_norm=False, cast=later)
    w_in_bf, w_out_bf, wq_bf, wkv_bf, wo_bf, w_gu2_bf, w_down2_bf = later_bf
    x2 = _mix(x1.reshape(bsz, t, d), mix_norm[0], w_in_bf, lb_logits, hg_norm[0], conv_w[0],
              conv_norm[0], w_out_bf)
    kv = _kv(mem.reshape(bsz * N_MEM, d), mem_norm[0], wkv_bf).reshape(bsz, N_MEM, 2 * d)
    x3 = _xattn(x2, xa_norm[0], wq_bf, kv, wo_bf)
    y, _ = _ffn(x3.reshape(bsz * t, d), ffn2_norm[0], w_gu2_bf, w_down2_bf, final_norm,
                final_norm=True)
    return y.reshape(bsz, t, d)
```

```python
import functools
import math

import jax
import jax.numpy as jnp
from jax import lax
from jax.experimental import pallas as pl
from jax.experimental.pallas import tpu as pltpu

D_MODEL = 1024
CHUNK = 64
SUB = 8
DIAG = 4
HG_WIDTH = 512
CONV_WIDTH = 512
HG_HEADS = 4
HG_DK = 128
CONV_K = 3
N_MEM = 256
XA_HEADS = 4
XA_HEAD_DIM = 256
D_FF = 2816
IN_COLS = 4 * HG_WIDTH + 3 * CONV_WIDTH
EPS = 1e-6

V7X_VMEM_BYTES = 64 * 1024 * 1024
MIB = 1024 * 1024

FFN_ROWS = 1024
FFN_SUB = 1024
FFN_COLS = 256
MIX_ROWS = 2048
MACRO = 128
PIECE = 512
XA_ROWS = 2048
XA_SUB = 256
KV_ROWS = 256

BF16 = jnp.bfloat16
F32 = jnp.float32
BF16_ROWS = 16


def _vmem_limit(estimate_bytes):
    return int(min(estimate_bytes * 5 // 4 + 8 * MIB, V7X_VMEM_BYTES - 6 * MIB))


def _resident(shape):
    nd = len(shape)
    return pl.BlockSpec(shape, lambda *_: (0,) * nd, pipeline_mode=pl.Buffered(1))


def _rmsnorm(x, g):
    ms = jnp.mean(x * x, axis=-1, keepdims=True)
    return x * lax.rsqrt(ms + EPS) * g


def _silu(x):
    return x * (1.0 / (1.0 + jnp.exp(-x)))


def _dot(a, b):
    return jnp.dot(a, b, preferred_element_type=F32)


def _dot_nt(a, b):
    return lax.dot_general(a, b, (((1,), (1,)), ((), ())), preferred_element_type=F32)


def _dot_tn(a, b):
    return lax.dot_general(a, b, (((0,), (0,)), ((), ())), preferred_element_type=F32)


def _ffn_kernel(x_ref, g_ref, wgu_ref, wd_ref, fin_ref, *refs, final_norm, n_cast):
    cast_in, o_ref, cast_out = refs[:n_cast], refs[n_cast], refs[n_cast + 1:2 * n_cast + 1]
    h_ref, act_ref = refs[2 * n_cast + 1:]
    n_sub = FFN_ROWS // FFN_SUB
    gain = g_ref[...]

    def rows(i):
        return slice(i * FFN_SUB, (i + 1) * FFN_SUB)

    def norm(i):
        h_ref[rows(i), :] = _rmsnorm(x_ref[rows(i), :], gain).astype(BF16)

    def gate_up(i):
        h = h_ref[rows(i), :]
        for c in range(D_FF // FFN_COLS):
            lo = c * FFN_COLS
            gate = _dot(h, wgu_ref[:, lo:lo + FFN_COLS])
            up = _dot(h, wgu_ref[:, D_FF + lo:D_FF + lo + FFN_COLS])
            act_ref[rows(i), lo:lo + FFN_COLS] = (_silu(gate) * up).astype(BF16)

    def down(i):
        y = x_ref[rows(i), :] + 0.5 * _dot(act_ref[rows(i), :], wd_ref[...])
        if final_norm:
            y = _rmsnorm(y, fin_ref[...])
        o_ref[rows(i), :] = y

    norm(0)
    for i in range(n_sub):
        if i + 1 < n_sub:
            norm(i + 1)
        gate_up(i)
        if i >= 1:
            down(i - 1)
    down(n_sub - 1)

    for src, dst in zip(cast_in, cast_out):
        dst[...] = src[...].astype(BF16)


def _ffn(x2d, norm, w_gu, w_down, fin, *, final_norm, cast=()):
    n = x2d.shape[0]
    tm = FFN_ROWS
    steps = n // tm
    slices = [w.shape[0] // steps for w in cast]
    assert all(s % BF16_ROWS == 0 and s * steps == w.shape[0] for s, w in zip(slices, cast))
    est = (4 * tm * D_MODEL * 4 + w_gu.size * 2 + w_down.size * 2 + tm * D_FF * 2
           + tm * D_MODEL * 2 + 4 * FFN_SUB * FFN_COLS * 4
           + 2 * sum(s * w.shape[1] for s, w in zip(slices, cast)) * (4 + 2))
    cast_specs = [pl.BlockSpec((s, w.shape[1]), lambda i: (i, 0)) for s, w in zip(slices, cast)]
    out = pl.pallas_call(
        functools.partial(_ffn_kernel, final_norm=final_norm, n_cast=len(cast)),
        grid=(steps,),
        in_specs=[
            pl.BlockSpec((tm, D_MODEL), lambda i: (i, 0)),
            _resident((1, D_MODEL)),
            _resident((D_MODEL, 2 * D_FF)),
            _resident((D_FF, D_MODEL)),
            _resident((1, D_MODEL)),
        ] + cast_specs,
        out_specs=[pl.BlockSpec((tm, D_MODEL), lambda i: (i, 0))] + cast_specs,
        out_shape=[jax.ShapeDtypeStruct((n, D_MODEL), F32)]
        + [jax.ShapeDtypeStruct(w.shape, BF16) for w in cast],
        scratch_shapes=[pltpu.VMEM((tm, D_MODEL), BF16), pltpu.VMEM((tm, D_FF), BF16)],
        compiler_params=pltpu.CompilerParams(
            dimension_semantics=("arbitrary",), vmem_limit_bytes=_vmem_limit(est)),
        name="ffn_final" if final_norm else "ffn",
    )(x2d, norm.reshape(1, D_MODEL), w_gu, w_down, fin.reshape(1, D_MODEL), *cast)
    return out[0], list(out[1:])


def _chunk_cumsum(g):
    n = g.shape[-1]
    nb = CHUNK // SUB
    row = lax.broadcasted_iota(jnp.int32, (nb, SUB, n), 1)
    s = g.reshape(nb, SUB, n)
    for shift in (1, 2, 4):
        s = s + jnp.where(row >= shift, pltpu.roll(s, shift, 1), 0.0)
    carry = [jnp.zeros((1, 1, n), F32)]
    for i in range(1, nb):
        carry.append(carry[-1] + s[i - 1:i, SUB - 1:SUB, :])
    return (s + jnp.concatenate(carry, axis=0)).reshape(CHUNK, n)


def _hgrn2_masks():
    t = lax.broadcasted_iota(jnp.int32, (CHUNK, CHUNK), 0)
    s = lax.broadcasted_iota(jnp.int32, (CHUNK, CHUNK), 1)
    levels = []
    blk = CHUNK
    while blk > DIAG:
        half = blk // 2
        levels.append((t // blk == s // blk) & (t % blk >= half) & (s % blk < half))
        blk = half
    diag = [(s == t - d) & (t % DIAG >= d) for d in range(DIAG)]
    return levels, diag


def _mix_kernel(x_ref, xn_ref, nrm_ref, win_ref, lbl_ref, hgn_ref, cw_ref, cn_ref, wout_ref, o_ref,
                p_a, p_b, mix_a, mix_b, h_a, h_b, st_ref, ybuf_ref):
    n_macro = MIX_ROWS // MACRO

    @pl.when(pl.program_id(1) == 0)
    def _():
        st_ref[...] = jnp.zeros_like(st_ref)
        ybuf_ref[0:SUB, :] = jnp.zeros((SUB, CONV_WIDTH), F32)

    mix_b[...] = jnp.zeros((MACRO, D_MODEL), BF16)

    lbl = lbl_ref[...]
    lmax = jnp.max(lbl, axis=0, keepdims=True)
    lexp = jnp.exp(lbl - lmax)
    lb_all = lexp[0:1, :] / jnp.sum(lexp, axis=0, keepdims=True)

    level_masks, diag_masks = _hgrn2_masks()
    hgn = hgn_ref[...]
    nrm = nrm_ref[...]
    cw = cw_ref[...]
    cn = cn_ref[...]
    cb = 4 * HG_WIDTH

    def macro_rows(m):
        return pl.ds(pl.multiple_of(m * MACRO, MACRO), MACRO)

    def in_norm(m, h_ref):
        h_ref[...] = _rmsnorm(x_ref[0, macro_rows(m), :], nrm).astype(BF16)

    def in_norm_or_next(m, h_ref):
        rows = x_ref[0, macro_rows(jnp.minimum(m, n_macro - 1)), :]
        rows = jnp.where(m >= n_macro, xn_ref[0], rows)
        h_ref[...] = _rmsnorm(rows, nrm).astype(BF16)

    def in_proj_piece(h_ref, p_ref, c):
        cols = slice(c * PIECE, (c + 1) * PIECE)
        p_ref[:, cols] = _dot(h_ref[...], win_ref[:, cols])

    def out_proj_piece(m, mix_ref, c):
        rows = macro_rows(m)
        cols = slice(c * PIECE, (c + 1) * PIECE)
        o_ref[0, rows, cols] = x_ref[0, rows, cols] + _dot(mix_ref[...], wout_ref[:, cols])


    def hg_ab(p_ref, rows, hh):
        c0 = hh * HG_DK
        q = p_ref[rows, pl.ds(c0, HG_DK)]
        fz = p_ref[rows, pl.ds(HG_WIDTH + c0, HG_DK)]
        v = p_ref[rows, pl.ds(2 * HG_WIDTH + c0, HG_DK)]
        lb = lb_all[:, c0:c0 + HG_DK]
        f = lb + (1.0 - lb) * (1.0 / (1.0 + jnp.exp(-fz)))
        k = 1.0 - f
        qs = _silu(q)
        b = _chunk_cumsum(jnp.log2(f))
        b_end = b[CHUNK - 1:CHUNK, :]
        u = dict(hh=hh, rows=rows, p_ref=p_ref, v_bf=v.astype(BF16))
        q_bf, k_bf = qs.astype(BF16), k.astype(BF16)
        ops = []
        blk = CHUNK
        for _ in level_masks:
            half = blk // 2
            nb = CHUNK // blk
            ref = b.reshape(nb, blk, HG_DK)[:, half - 1:half, :]
            ref = jnp.broadcast_to(ref, (nb, blk, HG_DK)).reshape(CHUNK, HG_DK)
            z = jnp.exp2(-jnp.abs(b - ref)).astype(BF16)
            ops.append((q_bf * z, k_bf * z))
            blk = half
        u["level_ops"] = ops
        attn = jnp.where(diag_masks[0], jnp.sum(qs * k, axis=-1, keepdims=True), 0.0)
        f3 = f.reshape(CHUNK // SUB, SUB, HG_DK)
        decay = f
        for d in range(1, DIAG):
            f_d = pltpu.roll(f3, d, 1).reshape(CHUNK, HG_DK)
            r_d = jnp.sum(qs * (1.0 - f_d) * decay, axis=-1, keepdims=True)
            attn = jnp.where(diag_masks[d], r_d, attn)
            decay = decay * f_d
        u["attn_diag"] = attn
        u["qe"] = (qs * jnp.exp2(b)).astype(BF16)
        u["ke"] = (k * jnp.exp2(b_end - b)).astype(BF16)
        u["decay"] = jnp.exp2(b_end)
        return u

    def hg_mb(u):
        u["level_out"] = [_dot_nt(ql, kl) for ql, kl in u.pop("level_ops")]

    def hg_c(u):
        attn = u.pop("attn_diag")
        for lm, a in zip(level_masks, u.pop("level_out")):
            attn = jnp.where(lm, a, attn)
        u["attn"] = attn.astype(BF16)

    def hg_mc(u):
        hh, v_bf = u["hh"], u.pop("v_bf")
        st = st_ref[hh]
        o = _dot(u.pop("attn"), v_bf)
        u["o"] = o + _dot(u.pop("qe"), st.astype(BF16))
        decay_col = jnp.transpose(jnp.broadcast_to(u.pop("decay"), (HG_DK, HG_DK)))
        st_ref[hh] = st * decay_col + _dot_tn(u.pop("ke"), v_bf)

    def hg_d(u, mix_ref):
        c0 = u["hh"] * HG_DK
        gt = u["p_ref"][u["rows"], pl.ds(3 * HG_WIDTH + c0, HG_DK)]
        on = _rmsnorm(u.pop("o"), hgn) * _silu(gt)
        mix_ref[u["rows"], pl.ds(c0, HG_DK)] = on.astype(BF16)

    def short_conv(p_ref, mix_ref, rows):
        y = p_ref[rows, cb + CONV_WIDTH:cb + 2 * CONV_WIDTH] * \
            p_ref[rows, cb + 2 * CONV_WIDTH:cb + 3 * CONV_WIDTH]
        ybuf_ref[SUB:SUB + CHUNK, :] = y
        y1 = ybuf_ref[SUB - 1:SUB - 1 + CHUNK, :]
        y2 = ybuf_ref[SUB - 2:SUB - 2 + CHUNK, :]
        conv = cw[0:1, :] * y2 + cw[1:2, :] * y1 + cw[2:3, :] * y
        ocv = _rmsnorm(p_ref[rows, cb:cb + CONV_WIDTH] * conv, cn)
        mix_ref[rows, HG_WIDTH:HG_WIDTH + CONV_WIDTH] = ocv.astype(BF16)
        ybuf_ref[0:SUB, :] = ybuf_ref[CHUNK:CHUNK + SUB, :]

    def each(phase, units, *args):
        for u in units:
            phase(u, *args)

    n_chunks = MACRO // CHUNK
    chunk_rows = [pl.ds(j * CHUNK, CHUNK) for j in range(n_chunks)]
    heads = range(HG_HEADS)
    last = n_macro - 1

    @pl.when((pl.program_id(0) == 0) & (pl.program_id(1) == 0))
    def _():
        in_norm(0, h_b)
        for c in range(IN_COLS // PIECE):
            in_proj_piece(h_b, p_a, c)

    in_norm(1, h_a)

    def trip(j, carry):
        m = 2 * j
        big_a = [functools.partial(in_proj_piece, h_a, p_b, c) for c in range(IN_COLS // PIECE)]
        big_a += [functools.partial(out_proj_piece, jnp.maximum(m - 1, 0), mix_b, c)
                  for c in range(D_MODEL // PIECE)]
        big_b = [functools.partial(in_proj_piece, h_b, p_a, c) for c in range(IN_COLS // PIECE)]
        big_b += [functools.partial(out_proj_piece, m, mix_a, c) for c in range(D_MODEL // PIECE)]
        groups = [(p_a, mix_a, r) for r in chunk_rows] + [(p_b, mix_b, r) for r in chunk_rows]
        ng = len(groups)
        plan = {i: [] for i in range(-1, ng + 1)}
        for pieces, slots in ((big_a, range(-1, n_chunks + 1)), (big_b, range(n_chunks + 1, ng + 1))):
            for k, piece in enumerate(pieces):
                plan[slots[k * len(slots) // len(pieces)]].append(piece)

        def issue(slot):
            for piece in plan[slot]:
                piece()

        issue(-1)
        units = []
        for i in range(ng + 1):
            if i < ng:
                p_ref, _, rows = groups[i]
                units.append([hg_ab(p_ref, rows, hh) for hh in heads])
            if i >= 1:
                short_conv(*groups[i - 1])
                each(hg_c, units[i - 1])
            if i >= 2:
                each(hg_d, units[i - 2], groups[i - 2][1])
            if i == n_chunks - 1:
                in_norm_or_next(m + 2, h_b)
            if i == ng - 1:
                in_norm(jnp.minimum(m + 3, last), h_a)
            if i < ng:
                each(hg_mb, units[i])
            if i >= 1:
                each(hg_mc, units[i - 1])
            issue(i)
        each(hg_d, units[ng - 1], groups[ng - 1][1])
        return carry

    lax.fori_loop(0, n_macro // 2, trip, 0)
    for c in range(D_MODEL // PIECE):
        out_proj_piece(last, mix_b, c)


def _mix(x, norm, w_in, lb_logits, hg_norm, conv_w, conv_norm, w_out):
    bsz, t, _ = x.shape
    tt = MIX_ROWS
    tiles = t // tt

    def next_group(b, i):
        step = jnp.minimum(b * tiles + i + 1, bsz * tiles - 1)
        return step // tiles, (step % tiles) * (tt // MACRO), 0

    est = (4 * tt * D_MODEL * 4 + 2 * MACRO * D_MODEL * 4 + w_in.size * 2 + w_out.size * 2 + 2 * MACRO * IN_COLS * 4
           + HG_HEADS * HG_DK * HG_DK * 4 + (CHUNK + SUB) * CONV_WIDTH * 4
           + 4 * MACRO * D_MODEL * 2)
    return pl.pallas_call(
        _mix_kernel,
        grid=(bsz, t // tt),
        in_specs=[
            pl.BlockSpec((1, tt, D_MODEL), lambda b, i: (b, i, 0)),
            pl.BlockSpec((1, MACRO, D_MODEL), next_group),
            _resident((1, D_MODEL)),
            _resident((D_MODEL, IN_COLS)),
            _resident(lb_logits.shape),
            _resident((1, HG_DK)),
            _resident((CONV_K, CONV_WIDTH)),
            _resident((1, CONV_WIDTH)),
            _resident((D_MODEL, D_MODEL)),
        ],
        out_specs=pl.BlockSpec((1, tt, D_MODEL), lambda b, i: (b, i, 0)),
        out_shape=jax.ShapeDtypeStruct(x.shape, F32),
        scratch_shapes=[
            pltpu.VMEM((MACRO, IN_COLS), F32),
            pltpu.VMEM((MACRO, IN_COLS), F32),
            pltpu.VMEM((MACRO, D_MODEL), BF16),
            pltpu.VMEM((MACRO, D_MODEL), BF16),
            pltpu.VMEM((MACRO, D_MODEL), BF16),
            pltpu.VMEM((MACRO, D_MODEL), BF16),
            pltpu.VMEM((HG_HEADS, HG_DK, HG_DK), F32),
            pltpu.VMEM((CHUNK + SUB, CONV_WIDTH), F32),
        ],
        compiler_params=pltpu.CompilerParams(
            dimension_semantics=("arbitrary", "arbitrary"), vmem_limit_bytes=_vmem_limit(est)),
        name="mix",
    )(x, x, norm.reshape(1, D_MODEL), w_in, lb_logits, hg_norm.reshape(1, HG_DK), conv_w,
      conv_norm.reshape(1, CONV_WIDTH), w_out)


def _kv_kernel(m_ref, g_ref, w_ref, o_ref):
    h = _rmsnorm(m_ref[...], g_ref[...]).astype(BF16)
    o_ref[...] = _dot(h, w_ref[...]).astype(BF16)


def _kv(mem2d, norm, w_kv):
    n = mem2d.shape[0]
    est = 2 * KV_ROWS * D_MODEL * 4 + w_kv.size * 2 + 2 * KV_ROWS * 2 * D_MODEL * 2
    return pl.pallas_call(
        _kv_kernel,
        grid=(n // KV_ROWS,),
        in_specs=[
            pl.BlockSpec((KV_ROWS, D_MODEL), lambda i: (i, 0)),
            _resident((1, D_MODEL)),
            _resident((D_MODEL, 2 * D_MODEL)),
        ],
        out_specs=pl.BlockSpec((KV_ROWS, 2 * D_MODEL), lambda i: (i, 0)),
        out_shape=jax.ShapeDtypeStruct((n, 2 * D_MODEL), BF16),
        compiler_params=pltpu.CompilerParams(
            dimension_semantics=("arbitrary",), vmem_limit_bytes=_vmem_limit(est)),
        name="kv",
    )(mem2d, norm.reshape(1, D_MODEL), w_kv)


def _xattn_kernel(x_ref, g_ref, wq_ref, kv_ref, wo_ref, o_ref, h_ref, q_ref, s_ref, p_ref,
                  att_ref):
    n_sub = XA_ROWS // XA_SUB
    scale = 1.0 / math.sqrt(XA_HEAD_DIM)
    gain = g_ref[...]
    heads = [slice(hh * XA_HEAD_DIM, (hh + 1) * XA_HEAD_DIM) for hh in range(XA_HEADS)]

    def rows(i):
        return slice(i * XA_SUB, (i + 1) * XA_SUB)

    def norm(i):
        h_ref[i % 2] = _rmsnorm(x_ref[0, rows(i), :], gain).astype(BF16)

    def q_proj(i):
        for hd in heads:
            q_ref[i % 2, :, hd] = (_dot(h_ref[i % 2], wq_ref[:, hd]) * scale).astype(BF16)

    def scores(i):
        for hd in heads:
            s_ref[i % 2, :, hd] = _dot_nt(q_ref[i % 2, :, hd], kv_ref[0, :, hd])

    def softmax(i):
        for hd in heads:
            s = s_ref[i % 2, :, hd]
            e = jnp.exp(s - jnp.max(s, axis=-1, keepdims=True))
            p_ref[i % 2, :, hd] = (e * (1.0 / jnp.sum(e, axis=-1, keepdims=True))).astype(BF16)

    def pv(i):
        for hh, hd in enumerate(heads):
            v = kv_ref[0, :, D_MODEL + hh * XA_HEAD_DIM:D_MODEL + (hh + 1) * XA_HEAD_DIM]
            att_ref[i % 2, :, hd] = _dot(p_ref[i % 2, :, hd], v).astype(BF16)

    def out_proj(i):
        o_ref[0, rows(i), :] = x_ref[0, rows(i), :] + _dot(att_ref[i % 2], wo_ref[...])

    norm(0)
    if n_sub > 1:
        norm(1)
    q_proj(0)
    scores(0)
    for i in range(n_sub):
        if i + 1 < n_sub:
            q_proj(i + 1)
        softmax(i)
        pv(i)
        if i + 1 < n_sub:
            scores(i + 1)
        if i + 2 < n_sub:
            norm(i + 2)
        out_proj(i)


def _xattn(x, norm, w_q, kv, w_o):
    bsz, t, _ = x.shape
    tm = XA_ROWS
    est = (4 * tm * D_MODEL * 4 + w_q.size * 2 + w_o.size * 2 + 2 * N_MEM * 2 * D_MODEL * 2
           + 2 * XA_SUB * D_MODEL * (4 + 4 * 2))
    return pl.pallas_call(
        _xattn_kernel,
        grid=(bsz, t // tm),
        in_specs=[
            pl.BlockSpec((1, tm, D_MODEL), lambda b, i: (b, i, 0)),
            _resident((1, D_MODEL)),
            _resident((D_MODEL, D_MODEL)),
            pl.BlockSpec((1, N_MEM, 2 * D_MODEL), lambda b, i: (b, 0, 0)),
            _resident((D_MODEL, D_MODEL)),
        ],
        out_specs=pl.BlockSpec((1, tm, D_MODEL), lambda b, i: (b, i, 0)),
        out_shape=jax.ShapeDtypeStruct(x.shape, F32),
        scratch_shapes=[
            pltpu.VMEM((2, XA_SUB, D_MODEL), BF16),
            pltpu.VMEM((2, XA_SUB, D_MODEL), BF16),
            pltpu.VMEM((2, XA_SUB, D_MODEL), F32),
            pltpu.VMEM((2, XA_SUB, D_MODEL), BF16),
            pltpu.VMEM((2, XA_SUB, D_MODEL), BF16),
        ],
        compiler_params=pltpu.CompilerParams(
            dimension_semantics=("arbitrary", "arbitrary"), vmem_limit_bytes=_vmem_limit(est)),
        name="xattn",
    )(x, norm.reshape(1, D_MODEL), w_q, kv, w_o)


def kernel(x, mem, ffn1_norm, ffn1_w_gu, ffn1_w_down, mix_norm, w_in, lb_logits, hg_norm, conv_w,
           conv_norm, w_out, xa_norm, mem_norm, xa_wq, xa_wkv, xa_wo, ffn2_norm, ffn2_w_gu,
           ffn2_w_down, final_norm):
    bsz, t, d = x.shape
    assert d == D_MODEL and t % MIX_ROWS == 0 and (bsz * t) % FFN_ROWS == 0
    assert ffn1_norm.shape[0] == 1, "single-layer block"
    later = [w_in[0], w_out[0], xa_wq[0], xa_wkv[0], xa_wo[0], ffn2_w_gu[0], ffn2_w_down[0]]
    x1, later_bf = _ffn(x.reshape(bsz * t, d), ffn1_norm[0], ffn1_w_gu[0].astype(BF16),
                        ffn1_w_down[0].astype(BF16), final_norm, final_norm=False, cast=later)
    w_in_bf, w_out_bf, wq_bf, wkv_bf, wo_bf, w_gu2_bf, w_down2_bf = later_bf
    x2 = _mix(x1.reshape(bsz, t, d), mix_norm[0], w_in_bf, lb_logits, hg_norm[0], conv_w[0],
              conv_norm[0], w_out_bf)
    kv = _kv(mem.reshape(bsz * N_MEM, d), mem_norm[0], wkv_bf).reshape(bsz, N_MEM, 2 * d)
    x3 = _xattn(x2, xa_norm[0], wq_bf, kv, wo_bf)
    y, _ = _ffn(x3.reshape(bsz * t, d), ffn2_norm[0], w_gu2_bf, w_down2_bf, final_norm,
                final_norm=True)
    return y.reshape(bsz, t, d)
```

```python
import functools
import math

import jax
import jax.numpy as jnp
from jax import lax
from jax.experimental import pallas as pl
from jax.experimental.pallas import tpu as pltpu

D_MODEL = 1024
CHUNK = 64
SUB = 8
DIAG = 4
HG_WIDTH = 512
CONV_WIDTH = 512
HG_HEADS = 4
HG_DK = 128
CONV_K = 3
N_MEM = 256
XA_HEADS = 4
XA_HEAD_DIM = 256
D_FF = 2816
IN_COLS = 4 * HG_WIDTH + 3 * CONV_WIDTH
EPS = 1e-6

V7X_VMEM_BYTES = 64 * 1024 * 1024
MIB = 1024 * 1024

FFN_ROWS = 1024
FFN_SUB = 1024
FFN_COLS = 256
MIX_ROWS = 2048
MACRO = 128
PIECE = 512
GROUP_HEADS = 2
XA_ROWS = 2048
XA_SUB = 256
KV_ROWS = 256

BF16 = jnp.bfloat16
F32 = jnp.float32
BF16_ROWS = 16


def _vmem_limit(estimate_bytes):
    return int(min(estimate_bytes * 5 // 4 + 8 * MIB, V7X_VMEM_BYTES - 6 * MIB))


def _resident(shape):
    nd = len(shape)
    return pl.BlockSpec(shape, lambda *_: (0,) * nd, pipeline_mode=pl.Buffered(1))


def _rmsnorm(x, g):
    ms = jnp.mean(x * x, axis=-1, keepdims=True)
    return x * lax.rsqrt(ms + EPS) * g


def _silu(x):
    return x * (1.0 / (1.0 + jnp.exp(-x)))


def _dot(a, b):
    return jnp.dot(a, b, preferred_element_type=F32)


def _dot_nt(a, b):
    return lax.dot_general(a, b, (((1,), (1,)), ((), ())), preferred_element_type=F32)


def _dot_tn(a, b):
    return lax.dot_general(a, b, (((0,), (0,)), ((), ())), preferred_element_type=F32)


def _ffn_kernel(x_ref, g_ref, wgu_ref, wd_ref, fin_ref, *refs, final_norm, n_cast):
    cast_in, o_ref, cast_out = refs[:n_cast], refs[n_cast], refs[n_cast + 1:2 * n_cast + 1]
    h_ref, act_ref = refs[2 * n_cast + 1:]
    n_sub = FFN_ROWS // FFN_SUB
    gain = g_ref[...]

    def rows(i):
        return slice(i * FFN_SUB, (i + 1) * FFN_SUB)

    def norm(i):
        h_ref[rows(i), :] = _rmsnorm(x_ref[rows(i), :], gain).astype(BF16)

    def gate_up(i):
        h = h_ref[rows(i), :]
        for c in range(D_FF // FFN_COLS):
            lo = c * FFN_COLS
            gate = _dot(h, wgu_ref[:, lo:lo + FFN_COLS])
            up = _dot(h, wgu_ref[:, D_FF + lo:D_FF + lo + FFN_COLS])
            act_ref[rows(i), lo:lo + FFN_COLS] = (_silu(gate) * up).astype(BF16)

    def down(i):
        y = x_ref[rows(i), :] + 0.5 * _dot(act_ref[rows(i), :], wd_ref[...])
        if final_norm:
            y = _rmsnorm(y, fin_ref[...])
        o_ref[rows(i), :] = y

    norm(0)
    for i in range(n_sub):
        if i + 1 < n_sub:
            norm(i + 1)
        gate_up(i)
        if i >= 1:
            down(i - 1)
    down(n_sub - 1)

    for src, dst in zip(cast_in, cast_out):
        dst[...] = src[...].astype(BF16)


def _ffn(x2d, norm, w_gu, w_down, fin, *, final_norm, cast=()):
    n = x2d.shape[0]
    tm = FFN_ROWS
    steps = n // tm
    slices = [w.shape[0] // steps for w in cast]
    assert all(s % BF16_ROWS == 0 and s * steps == w.shape[0] for s, w in zip(slices, cast))
    est = (4 * tm * D_MODEL * 4 + w_gu.size * 2 + w_down.size * 2 + tm * D_FF * 2
           + tm * D_MODEL * 2 + 4 * FFN_SUB * FFN_COLS * 4
           + 2 * sum(s * w.shape[1] for s, w in zip(slices, cast)) * (4 + 2))
    cast_specs = [pl.BlockSpec((s, w.shape[1]), lambda i: (i, 0)) for s, w in zip(slices, cast)]
    out = pl.pallas_call(
        functools.partial(_ffn_kernel, final_norm=final_norm, n_cast=len(cast)),
        grid=(steps,),
        in_specs=[
            pl.BlockSpec((tm, D_MODEL), lambda i: (i, 0)),
            _resident((1, D_MODEL)),
            _resident((D_MODEL, 2 * D_FF)),
            _resident((D_FF, D_MODEL)),
            _resident((1, D_MODEL)),
        ] + cast_specs,
        out_specs=[pl.BlockSpec((tm, D_MODEL), lambda i: (i, 0))] + cast_specs,
        out_shape=[jax.ShapeDtypeStruct((n, D_MODEL), F32)]
        + [jax.ShapeDtypeStruct(w.shape, BF16) for w in cast],
        scratch_shapes=[pltpu.VMEM((tm, D_MODEL), BF16), pltpu.VMEM((tm, D_FF), BF16)],
        compiler_params=pltpu.CompilerParams(
            dimension_semantics=("arbitrary",), vmem_limit_bytes=_vmem_limit(est)),
        name="ffn_final" if final_norm else "ffn",
    )(x2d, norm.reshape(1, D_MODEL), w_gu, w_down, fin.reshape(1, D_MODEL), *cast)
    return out[0], list(out[1:])


def _chunk_cumsum(g):
    n = g.shape[-1]
    nb = CHUNK // SUB
    row = lax.broadcasted_iota(jnp.int32, (nb, SUB, n), 1)
    s = g.reshape(nb, SUB, n)
    for shift in (1, 2, 4):
        s = s + jnp.where(row >= shift, pltpu.roll(s, shift, 1), 0.0)
    carry = [jnp.zeros((1, 1, n), F32)]
    for i in range(1, nb):
        carry.append(carry[-1] + s[i - 1:i, SUB - 1:SUB, :])
    return (s + jnp.concatenate(carry, axis=0)).reshape(CHUNK, n)


def _hgrn2_masks():
    t = lax.broadcasted_iota(jnp.int32, (CHUNK, CHUNK), 0)
    s = lax.broadcasted_iota(jnp.int32, (CHUNK, CHUNK), 1)
    levels = []
    blk = CHUNK
    while blk > DIAG:
        half = blk // 2
        levels.append((t // blk == s // blk) & (t % blk >= half) & (s % blk < half))
        blk = half
    diag = [(s == t - d) & (t % DIAG >= d) for d in range(DIAG)]
    return levels, diag


def _mix_kernel(x_ref, xn_ref, nrm_ref, win_ref, lbl_ref, hgn_ref, cw_ref, cn_ref, wout_ref, o_ref,
                p_a, p_b, mix_a, mix_b, h_a, h_b, st_ref, ybuf_ref):
    n_macro = MIX_ROWS // MACRO

    @pl.when(pl.program_id(1) == 0)
    def _():
        st_ref[...] = jnp.zeros_like(st_ref)
        ybuf_ref[0:SUB, :] = jnp.zeros((SUB, CONV_WIDTH), F32)

    mix_b[...] = jnp.zeros((MACRO, D_MODEL), BF16)

    lbl = lbl_ref[...]
    lmax = jnp.max(lbl, axis=0, keepdims=True)
    lexp = jnp.exp(lbl - lmax)
    lb_all = lexp[0:1, :] / jnp.sum(lexp, axis=0, keepdims=True)

    level_masks, diag_masks = _hgrn2_masks()
    hgn = hgn_ref[...]
    nrm = nrm_ref[...]
    cw = cw_ref[...]
    cn = cn_ref[...]
    cb = 4 * HG_WIDTH

    def macro_rows(m):
        return pl.ds(pl.multiple_of(m * MACRO, MACRO), MACRO)

    def in_norm(m, h_ref):
        h_ref[...] = _rmsnorm(x_ref[0, macro_rows(m), :], nrm).astype(BF16)

    def in_norm_or_next(m, h_ref):
        rows = x_ref[0, macro_rows(jnp.minimum(m, n_macro - 1)), :]
        rows = jnp.where(m >= n_macro, xn_ref[0], rows)
        h_ref[...] = _rmsnorm(rows, nrm).astype(BF16)

    def in_proj_piece(h_ref, p_ref, c):
        cols = slice(c * PIECE, (c + 1) * PIECE)
        p_ref[:, cols] = _dot(h_ref[...], win_ref[:, cols])

    def out_proj_piece(m, mix_ref, c):
        rows = macro_rows(m)
        cols = slice(c * PIECE, (c + 1) * PIECE)
        o_ref[0, rows, cols] = x_ref[0, rows, cols] + _dot(mix_ref[...], wout_ref[:, cols])


    def hg_ab(p_ref, rows, hh):
        c0 = hh * HG_DK
        q = p_ref[rows, pl.ds(c0, HG_DK)]
        fz = p_ref[rows, pl.ds(HG_WIDTH + c0, HG_DK)]
        v = p_ref[rows, pl.ds(2 * HG_WIDTH + c0, HG_DK)]
        lb = lb_all[:, c0:c0 + HG_DK]
        f = lb + (1.0 - lb) * (1.0 / (1.0 + jnp.exp(-fz)))
        k = 1.0 - f
        qs = _silu(q)
        b = _chunk_cumsum(jnp.log2(f))
        b_end = b[CHUNK - 1:CHUNK, :]
        u = dict(hh=hh, rows=rows, p_ref=p_ref, v_bf=v.astype(BF16))
        q_bf, k_bf = qs.astype(BF16), k.astype(BF16)
        ops = []
        blk = CHUNK
        for _ in level_masks:
            half = blk // 2
            nb = CHUNK // blk
            ref = b.reshape(nb, blk, HG_DK)[:, half - 1:half, :]
            ref = jnp.broadcast_to(ref, (nb, blk, HG_DK)).reshape(CHUNK, HG_DK)
            z = jnp.exp2(-jnp.abs(b - ref)).astype(BF16)
            ops.append((q_bf * z, k_bf * z))
            blk = half
        u["level_ops"] = ops
        attn = jnp.where(diag_masks[0], jnp.sum(qs * k, axis=-1, keepdims=True), 0.0)
        f3 = f.reshape(CHUNK // SUB, SUB, HG_DK)
        decay = f
        for d in range(1, DIAG):
            f_d = pltpu.roll(f3, d, 1).reshape(CHUNK, HG_DK)
            r_d = jnp.sum(qs * (1.0 - f_d) * decay, axis=-1, keepdims=True)
            attn = jnp.where(diag_masks[d], r_d, attn)
            decay = decay * f_d
        u["attn_diag"] = attn
        u["qe"] = (qs * jnp.exp2(b)).astype(BF16)
        u["ke"] = (k * jnp.exp2(b_end - b)).astype(BF16)
        u["decay"] = jnp.exp2(b_end)
        return u

    def hg_mb(u):
        u["level_out"] = [_dot_nt(ql, kl) for ql, kl in u.pop("level_ops")]

    def hg_c(u):
        attn = u.pop("attn_diag")
        for lm, a in zip(level_masks, u.pop("level_out")):
            attn = jnp.where(lm, a, attn)
        u["attn"] = attn.astype(BF16)

    def hg_mc(u):
        hh, v_bf = u["hh"], u.pop("v_bf")
        st = st_ref[hh]
        o = _dot(u.pop("attn"), v_bf)
        u["o"] = o + _dot(u.pop("qe"), st.astype(BF16))
        decay_col = jnp.transpose(jnp.broadcast_to(u.pop("decay"), (HG_DK, HG_DK)))
        st_ref[hh] = st * decay_col + _dot_tn(u.pop("ke"), v_bf)

    def hg_d(u, mix_ref):
        c0 = u["hh"] * HG_DK
        gt = u["p_ref"][u["rows"], pl.ds(3 * HG_WIDTH + c0, HG_DK)]
        on = _rmsnorm(u.pop("o"), hgn) * _silu(gt)
        mix_ref[u["rows"], pl.ds(c0, HG_DK)] = on.astype(BF16)

    def short_conv(p_ref, mix_ref, rows):
        y = p_ref[rows, cb + CONV_WIDTH:cb + 2 * CONV_WIDTH] * \
            p_ref[rows, cb + 2 * CONV_WIDTH:cb + 3 * CONV_WIDTH]
        ybuf_ref[SUB:SUB + CHUNK, :] = y
        y1 = ybuf_ref[SUB - 1:SUB - 1 + CHUNK, :]
        y2 = ybuf_ref[SUB - 2:SUB - 2 + CHUNK, :]
        conv = cw[0:1, :] * y2 + cw[1:2, :] * y1 + cw[2:3, :] * y
        ocv = _rmsnorm(p_ref[rows, cb:cb + CONV_WIDTH] * conv, cn)
        mix_ref[rows, HG_WIDTH:HG_WIDTH + CONV_WIDTH] = ocv.astype(BF16)
        ybuf_ref[0:SUB, :] = ybuf_ref[CHUNK:CHUNK + SUB, :]

    def each(phase, units, *args):
        for u in units:
            phase(u, *args)

    n_chunks = MACRO // CHUNK
    chunk_rows = [pl.ds(j * CHUNK, CHUNK) for j in range(n_chunks)]
    head_groups = [tuple(range(h, h + GROUP_HEADS)) for h in range(0, HG_HEADS, GROUP_HEADS)]
    last = n_macro - 1

    @pl.when((pl.program_id(0) == 0) & (pl.program_id(1) == 0))
    def _():
        in_norm(0, h_b)
        for c in range(IN_COLS // PIECE):
            in_proj_piece(h_b, p_a, c)

    in_norm(1, h_a)

    def trip(j, carry):
        m = 2 * j
        big_a = [functools.partial(in_proj_piece, h_a, p_b, c) for c in range(IN_COLS // PIECE)]
        big_a += [functools.partial(out_proj_piece, jnp.maximum(m - 1, 0), mix_b, c)
                  for c in range(D_MODEL // PIECE)]
        big_b = [functools.partial(in_proj_piece, h_b, p_a, c) for c in range(IN_COLS // PIECE)]
        big_b += [functools.partial(out_proj_piece, m, mix_a, c) for c in range(D_MODEL // PIECE)]
        groups = [(p_ref, mix_ref, r, hs, hs[0] == 0)
                  for p_ref, mix_ref in ((p_a, mix_a), (p_b, mix_b)) for r in chunk_rows
                  for hs in head_groups]
        ng = len(groups)
        na = ng // 2
        plan = {i: [] for i in range(-1, ng + 1)}
        for pieces, slots in ((big_a, range(-1, na + 1)), (big_b, range(na + 1, ng + 1))):
            for k, piece in enumerate(pieces):
                plan[slots[k * len(slots) // len(pieces)]].append(piece)

        def issue(slot):
            for piece in plan[slot]:
                piece()

        issue(-1)
        units = []
        for i in range(ng + 1):
            if i < ng:
                p_ref, _, rows, hs, _ = groups[i]
                units.append([hg_ab(p_ref, rows, hh) for hh in hs])
                each(hg_mb, units[i])
            if i >= 1:
                p_ref, mix_ref, rows, _, with_conv = groups[i - 1]
                if with_conv:
                    short_conv(p_ref, mix_ref, rows)
                each(hg_c, units[i - 1])
                each(hg_mc, units[i - 1])
            if i >= 2:
                each(hg_d, units[i - 2], groups[i - 2][1])
            if i == na - 1:
                in_norm_or_next(m + 2, h_b)
            if i == ng - 1:
                in_norm(jnp.minimum(m + 3, last), h_a)
            issue(i)
        each(hg_d, units[ng - 1], groups[ng - 1][1])
        return carry

    lax.fori_loop(0, n_macro // 2, trip, 0)
    for c in range(D_MODEL // PIECE):
        out_proj_piece(last, mix_b, c)


def _mix(x, norm, w_in, lb_logits, hg_norm, conv_w, conv_norm, w_out):
    bsz, t, _ = x.shape
    tt = MIX_ROWS
    tiles = t // tt

    def next_group(b, i):
        step = jnp.minimum(b * tiles + i + 1, bsz * tiles - 1)
        return step // tiles, (step % tiles) * (tt // MACRO), 0

    est = (4 * tt * D_MODEL * 4 + 2 * MACRO * D_MODEL * 4 + w_in.size * 2 + w_out.size * 2 + 2 * MACRO * IN_COLS * 4
           + HG_HEADS * HG_DK * HG_DK * 4 + (CHUNK + SUB) * CONV_WIDTH * 4
           + 4 * MACRO * D_MODEL * 2)
    return pl.pallas_call(
        _mix_kernel,
        grid=(bsz, t // tt),
        in_specs=[
            pl.BlockSpec((1, tt, D_MODEL), lambda b, i: (b, i, 0)),
            pl.BlockSpec((1, MACRO, D_MODEL), next_group),
            _resident((1, D_MODEL)),
            _resident((D_MODEL, IN_COLS)),
            _resident(lb_logits.shape),
            _resident((1, HG_DK)),
            _resident((CONV_K, CONV_WIDTH)),
            _resident((1, CONV_WIDTH)),
            _resident((D_MODEL, D_MODEL)),
        ],
        out_specs=pl.BlockSpec((1, tt, D_MODEL), lambda b, i: (b, i, 0)),
        out_shape=jax.ShapeDtypeStruct(x.shape, F32),
        scratch_shapes=[
            pltpu.VMEM((MACRO, IN_COLS), F32),
            pltpu.VMEM((MACRO, IN_COLS), F32),
            pltpu.VMEM((MACRO, D_MODEL), BF16),
            pltpu.VMEM((MACRO, D_MODEL), BF16),
            pltpu.VMEM((MACRO, D_MODEL), BF16),
            pltpu.VMEM((MACRO, D_MODEL), BF16),
            pltpu.VMEM((HG_HEADS, HG_DK, HG_DK), F32),
            pltpu.VMEM((CHUNK + SUB, CONV_WIDTH), F32),
        ],
        compiler_params=pltpu.CompilerParams(
            dimension_semantics=("arbitrary", "arbitrary"), vmem_limit_bytes=_vmem_limit(est)),
        name="mix",
    )(x, x, norm.reshape(1, D_MODEL), w_in, lb_logits, hg_norm.reshape(1, HG_DK), conv_w,
      conv_norm.reshape(1, CONV_WIDTH), w_out)


def _kv_kernel(m_ref, g_ref, w_ref, o_ref):
    h = _rmsnorm(m_ref[...], g_ref[...]).astype(BF16)
    o_ref[...] = _dot(h, w_ref[...]).astype(BF16)


def _kv(mem2d, norm, w_kv):
    n = mem2d.shape[0]
    est = 2 * KV_ROWS * D_MODEL * 4 + w_kv.size * 2 + 2 * KV_ROWS * 2 * D_MODEL * 2
    return pl.pallas_call(
        _kv_kernel,
        grid=(n // KV_ROWS,),
        in_specs=[
            pl.BlockSpec((KV_ROWS, D_MODEL), lambda i: (i, 0)),
            _resident((1, D_MODEL)),
            _resident((D_MODEL, 2 * D_MODEL)),
        ],
        out_specs=pl.BlockSpec((KV_ROWS, 2 * D_MODEL), lambda i: (i, 0)),
        out_shape=jax.ShapeDtypeStruct((n, 2 * D_MODEL), BF16),
        compiler_params=pltpu.CompilerParams(
            dimension_semantics=("arbitrary",), vmem_limit_bytes=_vmem_limit(est)),
        name="kv",
    )(mem2d, norm.reshape(1, D_MODEL), w_kv)


def _xattn_kernel(x_ref, g_ref, wq_ref, kv_ref, wo_ref, o_ref, h_ref, q_ref, s_ref, p_ref,
                  att_ref):
    n_sub = XA_ROWS // XA_SUB
    scale = 1.0 / math.sqrt(XA_HEAD_DIM)
    gain = g_ref[...]
    heads = [slice(hh * XA_HEAD_DIM, (hh + 1) * XA_HEAD_DIM) for hh in range(XA_HEADS)]

    def rows(i):
        return slice(i * XA_SUB, (i + 1) * XA_SUB)

    def norm(i):
        h_ref[i % 2] = _rmsnorm(x_ref[0, rows(i), :], gain).astype(BF16)

    def q_proj(i):
        for hd in heads:
            q_ref[i % 2, :, hd] = (_dot(h_ref[i % 2], wq_ref[:, hd]) * scale).astype(BF16)

    def scores(i):
        for hd in heads:
            s_ref[i % 2, :, hd] = _dot_nt(q_ref[i % 2, :, hd], kv_ref[0, :, hd])

    def softmax(i):
        for hd in heads:
            s = s_ref[i % 2, :, hd]
            e = jnp.exp(s - jnp.max(s, axis=-1, keepdims=True))
            p_ref[i % 2, :, hd] = (e * (1.0 / jnp.sum(e, axis=-1, keepdims=True))).astype(BF16)

    def pv(i):
        for hh, hd in enumerate(heads):
            v = kv_ref[0, :, D_MODEL + hh * XA_HEAD_DIM:D_MODEL + (hh + 1) * XA_HEAD_DIM]
            att_ref[i % 2, :, hd] = _dot(p_ref[i % 2, :, hd], v).astype(BF16)

    def out_proj(i):
        o_ref[0, rows(i), :] = x_ref[0, rows(i), :] + _dot(att_ref[i % 2], wo_ref[...])

    norm(0)
    if n_sub > 1:
        norm(1)
    q_proj(0)
    scores(0)
    for i in range(n_sub):
        if i + 1 < n_sub:
            q_proj(i + 1)
        softmax(i)
        pv(i)
        if i + 1 < n_sub:
            scores(i + 1)
        if i + 2 < n_sub:
            norm(i + 2)
        out_proj(i)


def _xattn(x, norm, w_q, kv, w_o):
    bsz, t, _ = x.shape
    tm = XA_ROWS
    est = (4 * tm * D_MODEL * 4 + w_q.size * 2 + w_o.size * 2 + 2 * N_MEM * 2 * D_MODEL * 2
           + 2 * XA_SUB * D_MODEL * (4 + 4 * 2))
    return pl.pallas_call(
        _xattn_kernel,
        grid=(bsz, t // tm),
        in_specs=[
            pl.BlockSpec((1, tm, D_MODEL), lambda b, i: (b, i, 0)),
            _resident((1, D_MODEL)),
            _resident((D_MODEL, D_MODEL)),
            pl.BlockSpec((1, N_MEM, 2 * D_MODEL), lambda b, i: (b, 0, 0)),
            _resident((D_MODEL, D_MODEL)),
        ],
        out_specs=pl.BlockSpec((1, tm, D_MODEL), lambda b, i: (b, i, 0)),
        out_shape=jax.ShapeDtypeStruct(x.shape, F32),
        scratch_shapes=[
            pltpu.VMEM((2, XA_SUB, D_MODEL), BF16),
            pltpu.VMEM((2, XA_SUB, D_MODEL), BF16),
            pltpu.VMEM((2, XA_SUB, D_MODEL), F32),
            pltpu.VMEM((2, XA_SUB, D_MODEL), BF16),
            pltpu.VMEM((2, XA_SUB, D_MODEL), BF16),
        ],
        compiler_params=pltpu.CompilerParams(
            dimension_semantics=("arbitrary", "arbitrary"), vmem_limit_bytes=_vmem_limit(est)),
        name="xattn",
    )(x, norm.reshape(1, D_MODEL), w_q, kv, w_o)


def kernel(x, mem, ffn1_norm, ffn1_w_gu, ffn1_w_down, mix_norm, w_in, lb_logits, hg_norm, conv_w,
           conv_norm, w_out, xa_norm, mem_norm, xa_wq, xa_wkv, xa_wo, ffn2_norm, ffn2_w_gu,
           ffn2_w_down, final_norm):
    bsz, t, d = x.shape
    assert d == D_MODEL and t % MIX_ROWS == 0 and (bsz * t) % FFN_ROWS == 0
    assert ffn1_norm.shape[0] == 1, "single-layer block"
    later = [w_in[0], w_out[0], xa_wq[0], xa_wkv[0], xa_wo[0], ffn2_w_gu[0], ffn2_w_down[0]]
    x1, later_bf = _ffn(x.reshape(bsz * t, d), ffn1_norm[0], ffn1_w_gu[0].astype(BF16),
                        ffn1_w_down[0].astype(BF16), final_norm, final_norm=False, cast=later)
    w_in_bf, w_out_bf, wq_bf, wkv_bf, wo_bf, w_gu2_bf, w_down2_bf = later_bf
    x2 = _mix(x1.reshape(bsz, t, d), mix_norm[0], w_in_bf, lb_logits, hg_norm[0], conv_w[0],
              conv_norm[0], w_out_bf)
    kv = _kv(mem.reshape(bsz * N_MEM, d), mem_norm[0], wkv_bf).reshape(bsz, N_MEM, 2 * d)
    x3 = _xattn(x2, xa_norm[0], wq_bf, kv, wo_bf)
    y, _ = _ffn(x3.reshape(bsz * t, d), ffn2_norm[0], w_gu2_bf, w_down2_bf, final_norm,
                final_norm=True)
    return y.reshape(bsz, t, d)
```

```python
import functools
import math

import jax
import jax.numpy as jnp
from jax import lax
from jax.experimental import pallas as pl
from jax.experimental.pallas import tpu as pltpu

D_MODEL = 1024
CHUNK = 64
SUB = 8
DIAG = 4
HG_WIDTH = 512
CONV_WIDTH = 512
HG_HEADS = 4
HG_DK = 128
CONV_K = 3
N_MEM = 256
XA_HEADS = 4
XA_HEAD_DIM = 256
D_FF = 2816
IN_COLS = 4 * HG_WIDTH + 3 * CONV_WIDTH
EPS = 1e-6

V7X_VMEM_BYTES = 64 * 1024 * 1024
MIB = 1024 * 1024

FFN_ROWS = 1024
FFN_SUB = 1024
FFN_COLS = 256
MIX_ROWS = 2048
MACRO = 128
PIECE = 512
GROUP_HEADS = 4
XA_ROWS = 2048
XA_SUB = 256
KV_ROWS = 256

BF16 = jnp.bfloat16
F32 = jnp.float32
BF16_ROWS = 16


def _vmem_limit(estimate_bytes):
    return int(min(estimate_bytes * 5 // 4 + 8 * MIB, V7X_VMEM_BYTES - 6 * MIB))


def _resident(shape):
    nd = len(shape)
    return pl.BlockSpec(shape, lambda *_: (0,) * nd, pipeline_mode=pl.Buffered(1))


def _rmsnorm(x, g):
    ms = jnp.mean(x * x, axis=-1, keepdims=True)
    return x * lax.rsqrt(ms + EPS) * g


def _silu(x):
    return x * (1.0 / (1.0 + jnp.exp(-x)))


def _dot(a, b):
    return jnp.dot(a, b, preferred_element_type=F32)


def _dot_nt(a, b):
    return lax.dot_general(a, b, (((1,), (1,)), ((), ())), preferred_element_type=F32)


def _dot_tn(a, b):
    return lax.dot_general(a, b, (((0,), (0,)), ((), ())), preferred_element_type=F32)


def _ffn_kernel(x_ref, g_ref, wgu_ref, wd_ref, fin_ref, *refs, final_norm, n_cast):
    cast_in, o_ref, cast_out = refs[:n_cast], refs[n_cast], refs[n_cast + 1:2 * n_cast + 1]
    h_ref, act_ref = refs[2 * n_cast + 1:]
    n_sub = FFN_ROWS // FFN_SUB
    gain = g_ref[...]

    def rows(i):
        return slice(i * FFN_SUB, (i + 1) * FFN_SUB)

    def norm(i):
        h_ref[rows(i), :] = _rmsnorm(x_ref[rows(i), :], gain).astype(BF16)

    def gate_up(i):
        h = h_ref[rows(i), :]
        for c in range(D_FF // FFN_COLS):
            lo = c * FFN_COLS
            gate = _dot(h, wgu_ref[:, lo:lo + FFN_COLS])
            up = _dot(h, wgu_ref[:, D_FF + lo:D_FF + lo + FFN_COLS])
            act_ref[rows(i), lo:lo + FFN_COLS] = (_silu(gate) * up).astype(BF16)

    def down(i):
        y = x_ref[rows(i), :] + 0.5 * _dot(act_ref[rows(i), :], wd_ref[...])
        if final_norm:
            y = _rmsnorm(y, fin_ref[...])
        o_ref[rows(i), :] = y

    norm(0)
    for i in range(n_sub):
        if i + 1 < n_sub:
            norm(i + 1)
        gate_up(i)
        if i >= 1:
            down(i - 1)
    down(n_sub - 1)

    for src, dst in zip(cast_in, cast_out):
        dst[...] = src[...].astype(BF16)


def _ffn(x2d, norm, w_gu, w_down, fin, *, final_norm, cast=()):
    n = x2d.shape[0]
    tm = FFN_ROWS
    steps = n // tm
    slices = [w.shape[0] // steps for w in cast]
    assert all(s % BF16_ROWS == 0 and s * steps == w.shape[0] for s, w in zip(slices, cast))
    est = (4 * tm * D_MODEL * 4 + w_gu.size * 2 + w_down.size * 2 + tm * D_FF * 2
           + tm * D_MODEL * 2 + 4 * FFN_SUB * FFN_COLS * 4
           + 2 * sum(s * w.shape[1] for s, w in zip(slices, cast)) * (4 + 2))
    cast_specs = [pl.BlockSpec((s, w.shape[1]), lambda i: (i, 0)) for s, w in zip(slices, cast)]
    out = pl.pallas_call(
        functools.partial(_ffn_kernel, final_norm=final_norm, n_cast=len(cast)),
        grid=(steps,),
        in_specs=[
            pl.BlockSpec((tm, D_MODEL), lambda i: (i, 0)),
            _resident((1, D_MODEL)),
            _resident((D_MODEL, 2 * D_FF)),
            _resident((D_FF, D_MODEL)),
            _resident((1, D_MODEL)),
        ] + cast_specs,
        out_specs=[pl.BlockSpec((tm, D_MODEL), lambda i: (i, 0))] + cast_specs,
        out_shape=[jax.ShapeDtypeStruct((n, D_MODEL), F32)]
        + [jax.ShapeDtypeStruct(w.shape, BF16) for w in cast],
        scratch_shapes=[pltpu.VMEM((tm, D_MODEL), BF16), pltpu.VMEM((tm, D_FF), BF16)],
        compiler_params=pltpu.CompilerParams(
            dimension_semantics=("arbitrary",), vmem_limit_bytes=_vmem_limit(est)),
        name="ffn_final" if final_norm else "ffn",
    )(x2d, norm.reshape(1, D_MODEL), w_gu, w_down, fin.reshape(1, D_MODEL), *cast)
    return out[0], list(out[1:])


def _chunk_cumsum(g):
    n = g.shape[-1]
    nb = CHUNK // SUB
    row = lax.broadcasted_iota(jnp.int32, (nb, SUB, n), 1)
    s = g.reshape(nb, SUB, n)
    for shift in (1, 2, 4):
        s = s + jnp.where(row >= shift, pltpu.roll(s, shift, 1), 0.0)
    carry = [jnp.zeros((1, 1, n), F32)]
    for i in range(1, nb):
        carry.append(carry[-1] + s[i - 1:i, SUB - 1:SUB, :])
    return (s + jnp.concatenate(carry, axis=0)).reshape(CHUNK, n)


def _hgrn2_masks():
    t = lax.broadcasted_iota(jnp.int32, (CHUNK, CHUNK), 0)
    s = lax.broadcasted_iota(jnp.int32, (CHUNK, CHUNK), 1)
    levels = []
    blk = CHUNK
    while blk > DIAG:
        half = blk // 2
        levels.append((t // blk == s // blk) & (t % blk >= half) & (s % blk < half))
        blk = half
    diag = [(s == t - d) & (t % DIAG >= d) for d in range(DIAG)]
    return levels, diag


def _mix_kernel(x_ref, xn_ref, nrm_ref, win_ref, lbl_ref, hgn_ref, cw_ref, cn_ref, wout_ref, o_ref,
                p_a, p_b, mix_a, mix_b, h_a, h_b, st_ref, ybuf_ref):
    n_macro = MIX_ROWS // MACRO

    @pl.when(pl.program_id(1) == 0)
    def _():
        st_ref[...] = jnp.zeros_like(st_ref)
        ybuf_ref[0:SUB, :] = jnp.zeros((SUB, CONV_WIDTH), F32)

    mix_b[...] = jnp.zeros((MACRO, D_MODEL), BF16)

    lbl = lbl_ref[...]
    lmax = jnp.max(lbl, axis=0, keepdims=True)
    lexp = jnp.exp(lbl - lmax)
    lb_all = lexp[0:1, :] / jnp.sum(lexp, axis=0, keepdims=True)

    level_masks, diag_masks = _hgrn2_masks()
    hgn = hgn_ref[...]
    nrm = nrm_ref[...]
    cw = cw_ref[...]
    cn = cn_ref[...]
    cb = 4 * HG_WIDTH

    def macro_rows(m):
        return pl.ds(pl.multiple_of(m * MACRO, MACRO), MACRO)

    def in_norm(m, h_ref):
        h_ref[...] = _rmsnorm(x_ref[0, macro_rows(m), :], nrm).astype(BF16)

    def in_norm_or_next(m, h_ref):
        rows = x_ref[0, macro_rows(jnp.minimum(m, n_macro - 1)), :]
        rows = jnp.where(m >= n_macro, xn_ref[0], rows)
        h_ref[...] = _rmsnorm(rows, nrm).astype(BF16)

    def in_proj_piece(h_ref, p_ref, c):
        cols = slice(c * PIECE, (c + 1) * PIECE)
        p_ref[:, cols] = _dot(h_ref[...], win_ref[:, cols])

    def out_proj_piece(m, mix_ref, c):
        rows = macro_rows(m)
        cols = slice(c * PIECE, (c + 1) * PIECE)
        o_ref[0, rows, cols] = x_ref[0, rows, cols] + _dot(mix_ref[...], wout_ref[:, cols])


    def hg_ab(p_ref, rows, hh):
        c0 = hh * HG_DK
        q = p_ref[rows, pl.ds(c0, HG_DK)]
        fz = p_ref[rows, pl.ds(HG_WIDTH + c0, HG_DK)]
        v = p_ref[rows, pl.ds(2 * HG_WIDTH + c0, HG_DK)]
        lb = lb_all[:, c0:c0 + HG_DK]
        f = lb + (1.0 - lb) * (1.0 / (1.0 + jnp.exp(-fz)))
        k = 1.0 - f
        qs = _silu(q)
        b = _chunk_cumsum(jnp.log2(f))
        b_end = b[CHUNK - 1:CHUNK, :]
        u = dict(hh=hh, rows=rows, p_ref=p_ref, v_bf=v.astype(BF16))
        q_bf, k_bf = qs.astype(BF16), k.astype(BF16)
        ops = []
        blk = CHUNK
        for _ in level_masks:
            half = blk // 2
            nb = CHUNK // blk
            ref = b.reshape(nb, blk, HG_DK)[:, half - 1:half, :]
            ref = jnp.broadcast_to(ref, (nb, blk, HG_DK)).reshape(CHUNK, HG_DK)
            z = jnp.exp2(-jnp.abs(b - ref)).astype(BF16)
            ops.append((q_bf * z, k_bf * z))
            blk = half
        u["level_ops"] = ops
        attn = jnp.where(diag_masks[0], jnp.sum(qs * k, axis=-1, keepdims=True), 0.0)
        f3 = f.reshape(CHUNK // SUB, SUB, HG_DK)
        decay = f
        for d in range(1, DIAG):
            f_d = pltpu.roll(f3, d, 1).reshape(CHUNK, HG_DK)
            r_d = jnp.sum(qs * (1.0 - f_d) * decay, axis=-1, keepdims=True)
            attn = jnp.where(diag_masks[d], r_d, attn)
            decay = decay * f_d
        u["attn_diag"] = attn
        u["qe"] = (qs * jnp.exp2(b)).astype(BF16)
        u["ke"] = (k * jnp.exp2(b_end - b)).astype(BF16)
        u["decay"] = jnp.exp2(b_end)
        return u

    def hg_mb(u):
        u["level_out"] = [_dot_nt(ql, kl) for ql, kl in u.pop("level_ops")]

    def hg_c(u):
        attn = u.pop("attn_diag")
        for lm, a in zip(level_masks, u.pop("level_out")):
            attn = jnp.where(lm, a, attn)
        u["attn"] = attn.astype(BF16)

    def hg_mc(u):
        hh, v_bf = u["hh"], u.pop("v_bf")
        st = st_ref[hh]
        o = _dot(u.pop("attn"), v_bf)
        u["o"] = o + _dot(u.pop("qe"), st.astype(BF16))
        decay_col = jnp.transpose(jnp.broadcast_to(u.pop("decay"), (HG_DK, HG_DK)))
        st_ref[hh] = st * decay_col + _dot_tn(u.pop("ke"), v_bf)

    def hg_d(u, mix_ref):
        c0 = u["hh"] * HG_DK
        gt = u["p_ref"][u["rows"], pl.ds(3 * HG_WIDTH + c0, HG_DK)]
        on = _rmsnorm(u.pop("o"), hgn) * _silu(gt)
        mix_ref[u["rows"], pl.ds(c0, HG_DK)] = on.astype(BF16)

    def short_conv(p_ref, mix_ref, rows):
        y = p_ref[rows, cb + CONV_WIDTH:cb + 2 * CONV_WIDTH] * \
            p_ref[rows, cb + 2 * CONV_WIDTH:cb + 3 * CONV_WIDTH]
        ybuf_ref[SUB:SUB + CHUNK, :] = y
        y1 = ybuf_ref[SUB - 1:SUB - 1 + CHUNK, :]
        y2 = ybuf_ref[SUB - 2:SUB - 2 + CHUNK, :]
        conv = cw[0:1, :] * y2 + cw[1:2, :] * y1 + cw[2:3, :] * y
        ocv = _rmsnorm(p_ref[rows, cb:cb + CONV_WIDTH] * conv, cn)
        mix_ref[rows, HG_WIDTH:HG_WIDTH + CONV_WIDTH] = ocv.astype(BF16)
        ybuf_ref[0:SUB, :] = ybuf_ref[CHUNK:CHUNK + SUB, :]

    def each(phase, units, *args):
        for u in units:
            phase(u, *args)

    n_chunks = MACRO // CHUNK
    chunk_rows = [pl.ds(j * CHUNK, CHUNK) for j in range(n_chunks)]
    head_groups = [tuple(range(h, h + GROUP_HEADS)) for h in range(0, HG_HEADS, GROUP_HEADS)]
    last = n_macro - 1

    @pl.when((pl.program_id(0) == 0) & (pl.program_id(1) == 0))
    def _():
        in_norm(0, h_b)
        for c in range(IN_COLS // PIECE):
            in_proj_piece(h_b, p_a, c)

    in_norm(1, h_a)

    def trip(j, carry):
        m = 2 * j
        big_a = [functools.partial(in_proj_piece, h_a, p_b, c) for c in range(IN_COLS // PIECE)]
        big_a += [functools.partial(out_proj_piece, jnp.maximum(m - 1, 0), mix_b, c)
                  for c in range(D_MODEL // PIECE)]
        big_b = [functools.partial(in_proj_piece, h_b, p_a, c) for c in range(IN_COLS // PIECE)]
        big_b += [functools.partial(out_proj_piece, m, mix_a, c) for c in range(D_MODEL // PIECE)]
        groups = [(p_ref, mix_ref, r, hs, hs[0] == 0)
                  for p_ref, mix_ref in ((p_a, mix_a), (p_b, mix_b)) for r in chunk_rows
                  for hs in head_groups]
        ng = len(groups)
        na = ng // 2
        plan = {i: [] for i in range(-1, ng + 1)}
        for pieces, slots in ((big_a, range(-1, na + 1)), (big_b, range(na + 1, ng + 1))):
            for k, piece in enumerate(pieces):
                plan[slots[k * len(slots) // len(pieces)]].append(piece)

        def issue(slot):
            for piece in plan[slot]:
                piece()

        issue(-1)
        units = []
        for i in range(ng + 1):
            if i < ng:
                p_ref, _, rows, hs, _ = groups[i]
                units.append([hg_ab(p_ref, rows, hh) for hh in hs])
                each(hg_mb, units[i])
            if i >= 1:
                p_ref, mix_ref, rows, _, with_conv = groups[i - 1]
                if with_conv:
                    short_conv(p_ref, mix_ref, rows)
                each(hg_c, units[i - 1])
                each(hg_mc, units[i - 1])
            if i >= 2:
                each(hg_d, units[i - 2], groups[i - 2][1])
            if i == na - 1:
                in_norm_or_next(m + 2, h_b)
            if i == ng - 1:
                in_norm(jnp.minimum(m + 3, last), h_a)
            issue(i)
        each(hg_d, units[ng - 1], groups[ng - 1][1])
        return carry

    lax.fori_loop(0, n_macro // 2, trip, 0, unroll=2)
    for c in range(D_MODEL // PIECE):
        out_proj_piece(last, mix_b, c)


def _mix(x, norm, w_in, lb_logits, hg_norm, conv_w, conv_norm, w_out):
    bsz, t, _ = x.shape
    tt = MIX_ROWS
    tiles = t // tt

    def next_group(b, i):
        step = jnp.minimum(b * tiles + i + 1, bsz * tiles - 1)
        return step // tiles, (step % tiles) * (tt // MACRO), 0

    est = (4 * tt * D_MODEL * 4 + 2 * MACRO * D_MODEL * 4 + w_in.size * 2 + w_out.size * 2 + 2 * MACRO * IN_COLS * 4
           + HG_HEADS * HG_DK * HG_DK * 4 + (CHUNK + SUB) * CONV_WIDTH * 4
           + 4 * MACRO * D_MODEL * 2)
    return pl.pallas_call(
        _mix_kernel,
        grid=(bsz, t // tt),
        in_specs=[
            pl.BlockSpec((1, tt, D_MODEL), lambda b, i: (b, i, 0)),
            pl.BlockSpec((1, MACRO, D_MODEL), next_group),
            _resident((1, D_MODEL)),
            _resident((D_MODEL, IN_COLS)),
            _resident(lb_logits.shape),
            _resident((1, HG_DK)),
            _resident((CONV_K, CONV_WIDTH)),
            _resident((1, CONV_WIDTH)),
            _resident((D_MODEL, D_MODEL)),
        ],
        out_specs=pl.BlockSpec((1, tt, D_MODEL), lambda b, i: (b, i, 0)),
        out_shape=jax.ShapeDtypeStruct(x.shape, F32),
        scratch_shapes=[
            pltpu.VMEM((MACRO, IN_COLS), F32),
            pltpu.VMEM((MACRO, IN_COLS), F32),
            pltpu.VMEM((MACRO, D_MODEL), BF16),
            pltpu.VMEM((MACRO, D_MODEL), BF16),
            pltpu.VMEM((MACRO, D_MODEL), BF16),
            pltpu.VMEM((MACRO, D_MODEL), BF16),
            pltpu.VMEM((HG_HEADS, HG_DK, HG_DK), F32),
            pltpu.VMEM((CHUNK + SUB, CONV_WIDTH), F32),
        ],
        compiler_params=pltpu.CompilerParams(
            dimension_semantics=("arbitrary", "arbitrary"), vmem_limit_bytes=_vmem_limit(est)),
        name="mix",
    )(x, x, norm.reshape(1, D_MODEL), w_in, lb_logits, hg_norm.reshape(1, HG_DK), conv_w,
      conv_norm.reshape(1, CONV_WIDTH), w_out)


def _kv_kernel(m_ref, g_ref, w_ref, o_ref):
    h = _rmsnorm(m_ref[...], g_ref[...]).astype(BF16)
    o_ref[...] = _dot(h, w_ref[...]).astype(BF16)


def _kv(mem2d, norm, w_kv):
    n = mem2d.shape[0]
    est = 2 * KV_ROWS * D_MODEL * 4 + w_kv.size * 2 + 2 * KV_ROWS * 2 * D_MODEL * 2
    return pl.pallas_call(
        _kv_kernel,
        grid=(n // KV_ROWS,),
        in_specs=[
            pl.BlockSpec((KV_ROWS, D_MODEL), lambda i: (i, 0)),
            _resident((1, D_MODEL)),
            _resident((D_MODEL, 2 * D_MODEL)),
        ],
        out_specs=pl.BlockSpec((KV_ROWS, 2 * D_MODEL), lambda i: (i, 0)),
        out_shape=jax.ShapeDtypeStruct((n, 2 * D_MODEL), BF16),
        compiler_params=pltpu.CompilerParams(
            dimension_semantics=("arbitrary",), vmem_limit_bytes=_vmem_limit(est)),
        name="kv",
    )(mem2d, norm.reshape(1, D_MODEL), w_kv)


def _xattn_kernel(x_ref, g_ref, wq_ref, kv_ref, wo_ref, o_ref, h_ref, q_ref, s_ref, p_ref,
                  att_ref):
    n_sub = XA_ROWS // XA_SUB
    scale = 1.0 / math.sqrt(XA_HEAD_DIM)
    gain = g_ref[...]
    heads = [slice(hh * XA_HEAD_DIM, (hh + 1) * XA_HEAD_DIM) for hh in range(XA_HEADS)]

    def rows(i):
        return slice(i * XA_SUB, (i + 1) * XA_SUB)

    def norm(i):
        h_ref[i % 2] = _rmsnorm(x_ref[0, rows(i), :], gain).astype(BF16)

    def q_proj(i):
        for hd in heads:
            q_ref[i % 2, :, hd] = (_dot(h_ref[i % 2], wq_ref[:, hd]) * scale).astype(BF16)

    def scores(i):
        for hd in heads:
            s_ref[i % 2, :, hd] = _dot_nt(q_ref[i % 2, :, hd], kv_ref[0, :, hd])

    def softmax(i):
        for hd in heads:
            s = s_ref[i % 2, :, hd]
            e = jnp.exp(s - jnp.max(s, axis=-1, keepdims=True))
            p_ref[i % 2, :, hd] = (e * (1.0 / jnp.sum(e, axis=-1, keepdims=True))).astype(BF16)

    def pv(i):
        for hh, hd in enumerate(heads):
            v = kv_ref[0, :, D_MODEL + hh * XA_HEAD_DIM:D_MODEL + (hh + 1) * XA_HEAD_DIM]
            att_ref[i % 2, :, hd] = _dot(p_ref[i % 2, :, hd], v).astype(BF16)

    def out_proj(i):
        o_ref[0, rows(i), :] = x_ref[0, rows(i), :] + _dot(att_ref[i % 2], wo_ref[...])

    norm(0)
    if n_sub > 1:
        norm(1)
    q_proj(0)
    scores(0)
    for i in range(n_sub):
        if i + 1 < n_sub:
            q_proj(i + 1)
        softmax(i)
        pv(i)
        if i + 1 < n_sub:
            scores(i + 1)
        if i + 2 < n_sub:
            norm(i + 2)
        out_proj(i)


def _xattn(x, norm, w_q, kv, w_o):
    bsz, t, _ = x.shape
    tm = XA_ROWS
    est = (4 * tm * D_MODEL * 4 + w_q.size * 2 + w_o.size * 2 + 2 * N_MEM * 2 * D_MODEL * 2
           + 2 * XA_SUB * D_MODEL * (4 + 4 * 2))
    return pl.pallas_call(
        _xattn_kernel,
        grid=(bsz, t // tm),
        in_specs=[
            pl.BlockSpec((1, tm, D_MODEL), lambda b, i: (b, i, 0)),
            _resident((1, D_MODEL)),
            _resident((D_MODEL, D_MODEL)),
            pl.BlockSpec((1, N_MEM, 2 * D_MODEL), lambda b, i: (b, 0, 0)),
            _resident((D_MODEL, D_MODEL)),
        ],
        out_specs=pl.BlockSpec((1, tm, D_MODEL), lambda b, i: (b, i, 0)),
        out_shape=jax.ShapeDtypeStruct(x.shape, F32),
        scratch_shapes=[
            pltpu.VMEM((2, XA_SUB, D_MODEL), BF16),
            pltpu.VMEM((2, XA_SUB, D_MODEL), BF16),
            pltpu.VMEM((2, XA_SUB, D_MODEL), F32),
            pltpu.VMEM((2, XA_SUB, D_MODEL), BF16),
            pltpu.VMEM((2, XA_SUB, D_MODEL), BF16),
        ],
        compiler_params=pltpu.CompilerParams(
            dimension_semantics=("arbitrary", "arbitrary"), vmem_limit_bytes=_vmem_limit(est)),
        name="xattn",
    )(x, norm.reshape(1, D_MODEL), w_q, kv, w_o)


def kernel(x, mem, ffn1_norm, ffn1_w_gu, ffn1_w_down, mix_norm, w_in, lb_logits, hg_norm, conv_w,
           conv_norm, w_out, xa_norm, mem_norm, xa_wq, xa_wkv, xa_wo, ffn2_norm, ffn2_w_gu,
           ffn2_w_down, final_norm):
    bsz, t, d = x.shape
    assert d == D_MODEL and t % MIX_ROWS == 0 and (bsz * t) % FFN_ROWS == 0
    assert ffn1_norm.shape[0] == 1, "single-layer block"
    later = [w_in[0], w_out[0], xa_wq[0], xa_wkv[0], xa_wo[0], ffn2_w_gu[0], ffn2_w_down[0]]
    x1, later_bf = _ffn(x.reshape(bsz * t, d), ffn1_norm[0], ffn1_w_gu[0].astype(BF16),
                        ffn1_w_down[0].astype(BF16), final_norm, final_norm=False, cast=later)
    w_in_bf, w_out_bf, wq_bf, wkv_bf, wo_bf, w_gu2_bf, w_down2_bf = later_bf
    x2 = _mix(x1.reshape(bsz, t, d), mix_norm[0], w_in_bf, lb_logits, hg_norm[0], conv_w[0],
              conv_norm[0], w_out_bf)
    kv = _kv(mem.reshape(bsz * N_MEM, d), mem_norm[0], wkv_bf).reshape(bsz, N_MEM, 2 * d)
    x3 = _xattn(x2, xa_norm[0], wq_bf, kv, wo_bf)
    y, _ = _ffn(x3.reshape(bsz * t, d), ffn2_norm[0], w_gu2_bf, w_down2_bf, final_norm,
                final_norm=True)
    return y.reshape(bsz, t, d)
```

```python
import functools
import math

import jax
import jax.numpy as jnp
from jax import lax
from jax.experimental import pallas as pl
from jax.experimental.pallas import tpu as pltpu

D_MODEL = 1024
CHUNK = 64
SUB = 8
DIAG = 8
HG_WIDTH = 512
CONV_WIDTH = 512
HG_HEADS = 4
HG_DK = 128
CONV_K = 3
N_MEM = 256
XA_HEADS = 4
XA_HEAD_DIM = 256
D_FF = 2816
IN_COLS = 4 * HG_WIDTH + 3 * CONV_WIDTH
EPS = 1e-6

V7X_VMEM_BYTES = 64 * 1024 * 1024
MIB = 1024 * 1024

FFN_ROWS = 1024
FFN_SUB = 1024
FFN_COLS = 256
MIX_ROWS = 2048
MACRO = 128
PIECE = 512
GROUP_HEADS = 4
XA_ROWS = 2048
XA_SUB = 256
KV_ROWS = 256

BF16 = jnp.bfloat16
F32 = jnp.float32
BF16_ROWS = 16


def _vmem_limit(estimate_bytes):
    return int(min(estimate_bytes * 5 // 4 + 8 * MIB, V7X_VMEM_BYTES - 6 * MIB))


def _resident(shape):
    nd = len(shape)
    return pl.BlockSpec(shape, lambda *_: (0,) * nd, pipeline_mode=pl.Buffered(1))


def _rmsnorm(x, g):
    ms = jnp.mean(x * x, axis=-1, keepdims=True)
    return x * lax.rsqrt(ms + EPS) * g


def _silu(x):
    return x * (1.0 / (1.0 + jnp.exp(-x)))


def _dot(a, b):
    return jnp.dot(a, b, preferred_element_type=F32)


def _dot_nt(a, b):
    return lax.dot_general(a, b, (((1,), (1,)), ((), ())), preferred_element_type=F32)


def _dot_tn(a, b):
    return lax.dot_general(a, b, (((0,), (0,)), ((), ())), preferred_element_type=F32)


def _ffn_kernel(x_ref, g_ref, wgu_ref, wd_ref, fin_ref, *refs, final_norm, n_cast):
    cast_in, o_ref, cast_out = refs[:n_cast], refs[n_cast], refs[n_cast + 1:2 * n_cast + 1]
    h_ref, act_ref = refs[2 * n_cast + 1:]
    n_sub = FFN_ROWS // FFN_SUB
    gain = g_ref[...]

    def rows(i):
        return slice(i * FFN_SUB, (i + 1) * FFN_SUB)

    def norm(i):
        h_ref[rows(i), :] = _rmsnorm(x_ref[rows(i), :], gain).astype(BF16)

    def gate_up(i):
        h = h_ref[rows(i), :]
        for c in range(D_FF // FFN_COLS):
            lo = c * FFN_COLS
            gate = _dot(h, wgu_ref[:, lo:lo + FFN_COLS])
            up = _dot(h, wgu_ref[:, D_FF + lo:D_FF + lo + FFN_COLS])
            act_ref[rows(i), lo:lo + FFN_COLS] = (_silu(gate) * up).astype(BF16)

    def down(i):
        y = x_ref[rows(i), :] + 0.5 * _dot(act_ref[rows(i), :], wd_ref[...])
        if final_norm:
            y = _rmsnorm(y, fin_ref[...])
        o_ref[rows(i), :] = y

    norm(0)
    for i in range(n_sub):
        if i + 1 < n_sub:
            norm(i + 1)
        gate_up(i)
        if i >= 1:
            down(i - 1)
    down(n_sub - 1)

    for src, dst in zip(cast_in, cast_out):
        dst[...] = src[...].astype(BF16)


def _ffn(x2d, norm, w_gu, w_down, fin, *, final_norm, cast=()):
    n = x2d.shape[0]
    tm = FFN_ROWS
    steps = n // tm
    slices = [w.shape[0] // steps for w in cast]
    assert all(s % BF16_ROWS == 0 and s * steps == w.shape[0] for s, w in zip(slices, cast))
    est = (4 * tm * D_MODEL * 4 + w_gu.size * 2 + w_down.size * 2 + tm * D_FF * 2
           + tm * D_MODEL * 2 + 4 * FFN_SUB * FFN_COLS * 4
           + 2 * sum(s * w.shape[1] for s, w in zip(slices, cast)) * (4 + 2))
    cast_specs = [pl.BlockSpec((s, w.shape[1]), lambda i: (i, 0)) for s, w in zip(slices, cast)]
    out = pl.pallas_call(
        functools.partial(_ffn_kernel, final_norm=final_norm, n_cast=len(cast)),
        grid=(steps,),
        in_specs=[
            pl.BlockSpec((tm, D_MODEL), lambda i: (i, 0)),
            _resident((1, D_MODEL)),
            _resident((D_MODEL, 2 * D_FF)),
            _resident((D_FF, D_MODEL)),
            _resident((1, D_MODEL)),
        ] + cast_specs,
        out_specs=[pl.BlockSpec((tm, D_MODEL), lambda i: (i, 0))] + cast_specs,
        out_shape=[jax.ShapeDtypeStruct((n, D_MODEL), F32)]
        + [jax.ShapeDtypeStruct(w.shape, BF16) for w in cast],
        scratch_shapes=[pltpu.VMEM((tm, D_MODEL), BF16), pltpu.VMEM((tm, D_FF), BF16)],
        compiler_params=pltpu.CompilerParams(
            dimension_semantics=("arbitrary",), vmem_limit_bytes=_vmem_limit(est)),
        name="ffn_final" if final_norm else "ffn",
    )(x2d, norm.reshape(1, D_MODEL), w_gu, w_down, fin.reshape(1, D_MODEL), *cast)
    return out[0], list(out[1:])


def _chunk_cumsum(g):
    n = g.shape[-1]
    nb = CHUNK // SUB
    row = lax.broadcasted_iota(jnp.int32, (nb, SUB, n), 1)
    s = g.reshape(nb, SUB, n)
    for shift in (1, 2, 4):
        s = s + jnp.where(row >= shift, pltpu.roll(s, shift, 1), 0.0)
    carry = [jnp.zeros((1, 1, n), F32)]
    for i in range(1, nb):
        carry.append(carry[-1] + s[i - 1:i, SUB - 1:SUB, :])
    return (s + jnp.concatenate(carry, axis=0)).reshape(CHUNK, n)


def _hgrn2_masks():
    t = lax.broadcasted_iota(jnp.int32, (CHUNK, CHUNK), 0)
    s = lax.broadcasted_iota(jnp.int32, (CHUNK, CHUNK), 1)
    levels = []
    blk = CHUNK
    while blk > DIAG:
        half = blk // 2
        levels.append((t // blk == s // blk) & (t % blk >= half) & (s % blk < half))
        blk = half
    diag = [(s == t - d) & (t % DIAG >= d) for d in range(DIAG)]
    return levels, diag


def _mix_kernel(x_ref, xn_ref, nrm_ref, win_ref, lbl_ref, hgn_ref, cw_ref, cn_ref, wout_ref, o_ref,
                p_a, p_b, mix_a, mix_b, h_a, h_b, st_ref, ybuf_ref):
    n_macro = MIX_ROWS // MACRO

    @pl.when(pl.program_id(1) == 0)
    def _():
        st_ref[...] = jnp.zeros_like(st_ref)
        ybuf_ref[0:SUB, :] = jnp.zeros((SUB, CONV_WIDTH), F32)

    mix_b[...] = jnp.zeros((MACRO, D_MODEL), BF16)

    lbl = lbl_ref[...]
    lmax = jnp.max(lbl, axis=0, keepdims=True)
    lexp = jnp.exp(lbl - lmax)
    lb_all = lexp[0:1, :] / jnp.sum(lexp, axis=0, keepdims=True)

    level_masks, diag_masks = _hgrn2_masks()
    hgn = hgn_ref[...]
    nrm = nrm_ref[...]
    cw = cw_ref[...]
    cn = cn_ref[...]
    cb = 4 * HG_WIDTH

    def macro_rows(m):
        return pl.ds(pl.multiple_of(m * MACRO, MACRO), MACRO)

    def in_norm(m, h_ref):
        h_ref[...] = _rmsnorm(x_ref[0, macro_rows(m), :], nrm).astype(BF16)

    def in_norm_or_next(m, h_ref):
        rows = x_ref[0, macro_rows(jnp.minimum(m, n_macro - 1)), :]
        rows = jnp.where(m >= n_macro, xn_ref[0], rows)
        h_ref[...] = _rmsnorm(rows, nrm).astype(BF16)

    def in_proj_piece(h_ref, p_ref, c):
        cols = slice(c * PIECE, (c + 1) * PIECE)
        p_ref[:, cols] = _dot(h_ref[...], win_ref[:, cols])

    def out_proj_piece(m, mix_ref, c):
        rows = macro_rows(m)
        cols = slice(c * PIECE, (c + 1) * PIECE)
        o_ref[0, rows, cols] = x_ref[0, rows, cols] + _dot(mix_ref[...], wout_ref[:, cols])


    def hg_ab(p_ref, rows, hh):
        c0 = hh * HG_DK
        q = p_ref[rows, pl.ds(c0, HG_DK)]
        fz = p_ref[rows, pl.ds(HG_WIDTH + c0, HG_DK)]
        v = p_ref[rows, pl.ds(2 * HG_WIDTH + c0, HG_DK)]
        lb = lb_all[:, c0:c0 + HG_DK]
        f = lb + (1.0 - lb) * (1.0 / (1.0 + jnp.exp(-fz)))
        k = 1.0 - f
        qs = _silu(q)
        b = _chunk_cumsum(jnp.log2(f))
        b_end = b[CHUNK - 1:CHUNK, :]
        u = dict(hh=hh, rows=rows, p_ref=p_ref, v_bf=v.astype(BF16))
        q_bf, k_bf = qs.astype(BF16), k.astype(BF16)
        ops = []
        blk = CHUNK
        for _ in level_masks:
            half = blk // 2
            nb = CHUNK // blk
            ref = b.reshape(nb, blk, HG_DK)[:, half - 1:half, :]
            ref = jnp.broadcast_to(ref, (nb, blk, HG_DK)).reshape(CHUNK, HG_DK)
            z = jnp.exp2(-jnp.abs(b - ref)).astype(BF16)
            ops.append((q_bf * z, k_bf * z))
            blk = half
        u["level_ops"] = ops
        attn = jnp.where(diag_masks[0], jnp.sum(qs * k, axis=-1, keepdims=True), 0.0)
        f3 = f.reshape(CHUNK // SUB, SUB, HG_DK)
        decay = f
        for d in range(1, DIAG):
            f_d = pltpu.roll(f3, d, 1).reshape(CHUNK, HG_DK)
            r_d = jnp.sum(qs * (1.0 - f_d) * decay, axis=-1, keepdims=True)
            attn = jnp.where(diag_masks[d], r_d, attn)
            decay = decay * f_d
        u["attn_diag"] = attn
        u["qe"] = (qs * jnp.exp2(b)).astype(BF16)
        u["ke"] = (k * jnp.exp2(b_end - b)).astype(BF16)
        u["decay"] = jnp.exp2(b_end)
        return u

    def hg_mb(u):
        u["level_out"] = [_dot_nt(ql, kl) for ql, kl in u.pop("level_ops")]

    def hg_c(u):
        attn = u.pop("attn_diag")
        for lm, a in zip(level_masks, u.pop("level_out")):
            attn = jnp.where(lm, a, attn)
        u["attn"] = attn.astype(BF16)

    def hg_mc(u):
        hh, v_bf = u["hh"], u.pop("v_bf")
        st = st_ref[hh]
        o = _dot(u.pop("attn"), v_bf)
        u["o"] = o + _dot(u.pop("qe"), st.astype(BF16))
        decay_col = jnp.transpose(jnp.broadcast_to(u.pop("decay"), (HG_DK, HG_DK)))
        st_ref[hh] = st * decay_col + _dot_tn(u.pop("ke"), v_bf)

    def hg_d(u, mix_ref):
        c0 = u["hh"] * HG_DK
        gt = u["p_ref"][u["rows"], pl.ds(3 * HG_WIDTH + c0, HG_DK)]
        on = _rmsnorm(u.pop("o"), hgn) * _silu(gt)
        mix_ref[u["rows"], pl.ds(c0, HG_DK)] = on.astype(BF16)

    def short_conv(p_ref, mix_ref, rows):
        y = p_ref[rows, cb + CONV_WIDTH:cb + 2 * CONV_WIDTH] * \
            p_ref[rows, cb + 2 * CONV_WIDTH:cb + 3 * CONV_WIDTH]
        ybuf_ref[SUB:SUB + CHUNK, :] = y
        y1 = ybuf_ref[SUB - 1:SUB - 1 + CHUNK, :]
        y2 = ybuf_ref[SUB - 2:SUB - 2 + CHUNK, :]
        conv = cw[0:1, :] * y2 + cw[1:2, :] * y1 + cw[2:3, :] * y
        ocv = _rmsnorm(p_ref[rows, cb:cb + CONV_WIDTH] * conv, cn)
        mix_ref[rows, HG_WIDTH:HG_WIDTH + CONV_WIDTH] = ocv.astype(BF16)
        ybuf_ref[0:SUB, :] = ybuf_ref[CHUNK:CHUNK + SUB, :]

    def each(phase, units, *args):
        for u in units:
            phase(u, *args)

    n_chunks = MACRO // CHUNK
    chunk_rows = [pl.ds(j * CHUNK, CHUNK) for j in range(n_chunks)]
    head_groups = [tuple(range(h, h + GROUP_HEADS)) for h in range(0, HG_HEADS, GROUP_HEADS)]
    last = n_macro - 1

    @pl.when((pl.program_id(0) == 0) & (pl.program_id(1) == 0))
    def _():
        in_norm(0, h_b)
        for c in range(IN_COLS // PIECE):
            in_proj_piece(h_b, p_a, c)

    in_norm(1, h_a)

    def trip(j, carry):
        m = 2 * j
        big_a = [functools.partial(in_proj_piece, h_a, p_b, c) for c in range(IN_COLS // PIECE)]
        big_a += [functools.partial(out_proj_piece, jnp.maximum(m - 1, 0), mix_b, c)
                  for c in range(D_MODEL // PIECE)]
        big_b = [functools.partial(in_proj_piece, h_b, p_a, c) for c in range(IN_COLS // PIECE)]
        big_b += [functools.partial(out_proj_piece, m, mix_a, c) for c in range(D_MODEL // PIECE)]
        groups = [(p_ref, mix_ref, r, hs, hs[0] == 0)
                  for p_ref, mix_ref in ((p_a, mix_a), (p_b, mix_b)) for r in chunk_rows
                  for hs in head_groups]
        ng = len(groups)
        na = ng // 2
        plan = {i: [] for i in range(-1, ng + 1)}
        for pieces, slots in ((big_a, range(-1, na + 1)), (big_b, range(na + 1, ng + 1))):
            for k, piece in enumerate(pieces):
                plan[slots[k * len(slots) // len(pieces)]].append(piece)

        def issue(slot):
            for piece in plan[slot]:
                piece()

        issue(-1)
        units = []
        for i in range(ng + 1):
            if i < ng:
                p_ref, _, rows, hs, _ = groups[i]
                units.append([hg_ab(p_ref, rows, hh) for hh in hs])
                each(hg_mb, units[i])
            if i >= 1:
                p_ref, mix_ref, rows, _, with_conv = groups[i - 1]
                if with_conv:
                    short_conv(p_ref, mix_ref, rows)
                each(hg_c, units[i - 1])
                each(hg_mc, units[i - 1])
            if i >= 2:
                each(hg_d, units[i - 2], groups[i - 2][1])
            if i == na - 1:
                in_norm_or_next(m + 2, h_b)
            if i == ng - 1:
                in_norm(jnp.minimum(m + 3, last), h_a)
            issue(i)
        each(hg_d, units[ng - 1], groups[ng - 1][1])
        return carry

    lax.fori_loop(0, n_macro // 2, trip, 0, unroll=2)
    for c in range(D_MODEL // PIECE):
        out_proj_piece(last, mix_b, c)


def _mix(x, norm, w_in, lb_logits, hg_norm, conv_w, conv_norm, w_out):
    bsz, t, _ = x.shape
    tt = MIX_ROWS
    tiles = t // tt

    def next_group(b, i):
        step = jnp.minimum(b * tiles + i + 1, bsz * tiles - 1)
        return step // tiles, (step % tiles) * (tt // MACRO), 0

    est = (4 * tt * D_MODEL * 4 + 2 * MACRO * D_MODEL * 4 + w_in.size * 2 + w_out.size * 2 + 2 * MACRO * IN_COLS * 4
           + HG_HEADS * HG_DK * HG_DK * 4 + (CHUNK + SUB) * CONV_WIDTH * 4
           + 4 * MACRO * D_MODEL * 2)
    return pl.pallas_call(
        _mix_kernel,
        grid=(bsz, t // tt),
        in_specs=[
            pl.BlockSpec((1, tt, D_MODEL), lambda b, i: (b, i, 0)),
            pl.BlockSpec((1, MACRO, D_MODEL), next_group),
            _resident((1, D_MODEL)),
            _resident((D_MODEL, IN_COLS)),
            _resident(lb_logits.shape),
            _resident((1, HG_DK)),
            _resident((CONV_K, CONV_WIDTH)),
            _resident((1, CONV_WIDTH)),
            _resident((D_MODEL, D_MODEL)),
        ],
        out_specs=pl.BlockSpec((1, tt, D_MODEL), lambda b, i: (b, i, 0)),
        out_shape=jax.ShapeDtypeStruct(x.shape, F32),
        scratch_shapes=[
            pltpu.VMEM((MACRO, IN_COLS), F32),
            pltpu.VMEM((MACRO, IN_COLS), F32),
            pltpu.VMEM((MACRO, D_MODEL), BF16),
            pltpu.VMEM((MACRO, D_MODEL), BF16),
            pltpu.VMEM((MACRO, D_MODEL), BF16),
            pltpu.VMEM((MACRO, D_MODEL), BF16),
            pltpu.VMEM((HG_HEADS, HG_DK, HG_DK), F32),
            pltpu.VMEM((CHUNK + SUB, CONV_WIDTH), F32),
        ],
        compiler_params=pltpu.CompilerParams(
            dimension_semantics=("arbitrary", "arbitrary"), vmem_limit_bytes=_vmem_limit(est)),
        name="mix",
    )(x, x, norm.reshape(1, D_MODEL), w_in, lb_logits, hg_norm.reshape(1, HG_DK), conv_w,
      conv_norm.reshape(1, CONV_WIDTH), w_out)


def _kv_kernel(m_ref, g_ref, w_ref, o_ref):
    h = _rmsnorm(m_ref[...], g_ref[...]).astype(BF16)
    o_ref[...] = _dot(h, w_ref[...]).astype(BF16)


def _kv(mem2d, norm, w_kv):
    n = mem2d.shape[0]
    est = 2 * KV_ROWS * D_MODEL * 4 + w_kv.size * 2 + 2 * KV_ROWS * 2 * D_MODEL * 2
    return pl.pallas_call(
        _kv_kernel,
        grid=(n // KV_ROWS,),
        in_specs=[
            pl.BlockSpec((KV_ROWS, D_MODEL), lambda i: (i, 0)),
            _resident((1, D_MODEL)),
            _resident((D_MODEL, 2 * D_MODEL)),
        ],
        out_specs=pl.BlockSpec((KV_ROWS, 2 * D_MODEL), lambda i: (i, 0)),
        out_shape=jax.ShapeDtypeStruct((n, 2 * D_MODEL), BF16),
        compiler_params=pltpu.CompilerParams(
            dimension_semantics=("arbitrary",), vmem_limit_bytes=_vmem_limit(est)),
        name="kv",
    )(mem2d, norm.reshape(1, D_MODEL), w_kv)


def _xattn_kernel(x_ref, g_ref, wq_ref, kv_ref, wo_ref, o_ref, h_ref, q_ref, s_ref, p_ref,
                  att_ref):
    n_sub = XA_ROWS // XA_SUB
    scale = 1.0 / math.sqrt(XA_HEAD_DIM)
    gain = g_ref[...]
    heads = [slice(hh * XA_HEAD_DIM, (hh + 1) * XA_HEAD_DIM) for hh in range(XA_HEADS)]

    def rows(i):
        return slice(i * XA_SUB, (i + 1) * XA_SUB)

    def norm(i):
        h_ref[i % 2] = _rmsnorm(x_ref[0, rows(i), :], gain).astype(BF16)

    def q_proj(i):
        for hd in heads:
            q_ref[i % 2, :, hd] = (_dot(h_ref[i % 2], wq_ref[:, hd]) * scale).astype(BF16)

    def scores(i):
        for hd in heads:
            s_ref[i % 2, :, hd] = _dot_nt(q_ref[i % 2, :, hd], kv_ref[0, :, hd])

    def softmax(i):
        for hd in heads:
            s = s_ref[i % 2, :, hd]
            e = jnp.exp(s - jnp.max(s, axis=-1, keepdims=True))
            p_ref[i % 2, :, hd] = (e * (1.0 / jnp.sum(e, axis=-1, keepdims=True))).astype(BF16)

    def pv(i):
        for hh, hd in enumerate(heads):
            v = kv_ref[0, :, D_MODEL + hh * XA_HEAD_DIM:D_MODEL + (hh + 1) * XA_HEAD_DIM]
            att_ref[i % 2, :, hd] = _dot(p_ref[i % 2, :, hd], v).astype(BF16)

    def out_proj(i):
        o_ref[0, rows(i), :] = x_ref[0, rows(i), :] + _dot(att_ref[i % 2], wo_ref[...])

    norm(0)
    if n_sub > 1:
        norm(1)
    q_proj(0)
    scores(0)
    for i in range(n_sub):
        if i + 1 < n_sub:
            q_proj(i + 1)
        softmax(i)
        pv(i)
        if i + 1 < n_sub:
            scores(i + 1)
        if i + 2 < n_sub:
            norm(i + 2)
        out_proj(i)


def _xattn(x, norm, w_q, kv, w_o):
    bsz, t, _ = x.shape
    tm = XA_ROWS
    est = (4 * tm * D_MODEL * 4 + w_q.size * 2 + w_o.size * 2 + 2 * N_MEM * 2 * D_MODEL * 2
           + 2 * XA_SUB * D_MODEL * (4 + 4 * 2))
    return pl.pallas_call(
        _xattn_kernel,
        grid=(bsz, t // tm),
        in_specs=[
            pl.BlockSpec((1, tm, D_MODEL), lambda b, i: (b, i, 0)),
            _resident((1, D_MODEL)),
            _resident((D_MODEL, D_MODEL)),
            pl.BlockSpec((1, N_MEM, 2 * D_MODEL), lambda b, i: (b, 0, 0)),
            _resident((D_MODEL, D_MODEL)),
        ],
        out_specs=pl.BlockSpec((1, tm, D_MODEL), lambda b, i: (b, i, 0)),
        out_shape=jax.ShapeDtypeStruct(x.shape, F32),
        scratch_shapes=[
            pltpu.VMEM((2, XA_SUB, D_MODEL), BF16),
            pltpu.VMEM((2, XA_SUB, D_MODEL), BF16),
            pltpu.VMEM((2, XA_SUB, D_MODEL), F32),
            pltpu.VMEM((2, XA_SUB, D_MODEL), BF16),
            pltpu.VMEM((2, XA_SUB, D_MODEL), BF16),
        ],
        compiler_params=pltpu.CompilerParams(
            dimension_semantics=("arbitrary", "arbitrary"), vmem_limit_bytes=_vmem_limit(est)),
        name="xattn",
    )(x, norm.reshape(1, D_MODEL), w_q, kv, w_o)


def kernel(x, mem, ffn1_norm, ffn1_w_gu, ffn1_w_down, mix_norm, w_in, lb_logits, hg_norm, conv_w,
           conv_norm, w_out, xa_norm, mem_norm, xa_wq, xa_wkv, xa_wo, ffn2_norm, ffn2_w_gu,
           ffn2_w_down, final_norm):
    bsz, t, d = x.shape
    assert d == D_MODEL and t % MIX_ROWS == 0 and (bsz * t) % FFN_ROWS == 0
    assert ffn1_norm.shape[0] == 1, "single-layer block"
    later = [w_in[0], w_out[0], xa_wq[0], xa_wkv[0], xa_wo[0], ffn2_w_gu[0], ffn2_w_down[0]]
    x1, later_bf = _ffn(x.reshape(bsz * t, d), ffn1_norm[0], ffn1_w_gu[0].astype(BF16),
                        ffn1_w_down[0].astype(BF16), final_norm, final_norm=False, cast=later)
    w_in_bf, w_out_bf, wq_bf, wkv_bf, wo_bf, w_gu2_bf, w_down2_bf = later_bf
    x2 = _mix(x1.reshape(bsz, t, d), mix_norm[0], w_in_bf, lb_logits, hg_norm[0], conv_w[0],
              conv_norm[0], w_out_bf)
    kv = _kv(mem.reshape(bsz * N_MEM, d), mem_norm[0], wkv_bf).reshape(bsz, N_MEM, 2 * d)
    x3 = _xattn(x2, xa_norm[0], wq_bf, kv, wo_bf)
    y, _ = _ffn(x3.reshape(bsz * t, d), ffn2_norm[0], w_gu2_bf, w_down2_bf, final_norm,
                final_norm=True)
    return y.reshape(bsz, t, d)
```

```python
import functools
import math

import jax
import jax.numpy as jnp
from jax import lax
from jax.experimental import pallas as pl
from jax.experimental.pallas import tpu as pltpu

D_MODEL = 1024
CHUNK = 64
SUB = 8
DIAG = 4
HG_WIDTH = 512
CONV_WIDTH = 512
HG_HEADS = 4
HG_DK = 128
CONV_K = 3
N_MEM = 256
XA_HEADS = 4
XA_HEAD_DIM = 256
D_FF = 2816
IN_COLS = 4 * HG_WIDTH + 3 * CONV_WIDTH
EPS = 1e-6

V7X_VMEM_BYTES = 64 * 1024 * 1024
MIB = 1024 * 1024

FFN_ROWS = 1024
FFN_SUB = 1024
FFN_COLS = 256
MIX_ROWS = 2048
MACRO = 128
PIECE = 256
GROUP_HEADS = 4
XA_ROWS = 2048
XA_SUB = 512
KV_ROWS = 256

BF16 = jnp.bfloat16
F32 = jnp.float32
BF16_ROWS = 16


def _vmem_limit(estimate_bytes):
    return int(min(estimate_bytes * 5 // 4 + 8 * MIB, V7X_VMEM_BYTES - 6 * MIB))


def _resident(shape):
    nd = len(shape)
    return pl.BlockSpec(shape, lambda *_: (0,) * nd, pipeline_mode=pl.Buffered(1))


def _rmsnorm(x, g):
    ms = jnp.mean(x * x, axis=-1, keepdims=True)
    return x * lax.rsqrt(ms + EPS) * g


def _silu(x):
    return x * (1.0 / (1.0 + jnp.exp(-x)))


def _dot(a, b):
    return jnp.dot(a, b, preferred_element_type=F32)


def _dot_nt(a, b):
    return lax.dot_general(a, b, (((1,), (1,)), ((), ())), preferred_element_type=F32)


def _dot_tn(a, b):
    return lax.dot_general(a, b, (((0,), (0,)), ((), ())), preferred_element_type=F32)


def _ffn_kernel(x_ref, g_ref, wgu_ref, wd_ref, fin_ref, *refs, final_norm, n_cast):
    cast_in, o_ref, cast_out = refs[:n_cast], refs[n_cast], refs[n_cast + 1:2 * n_cast + 1]
    h_ref, act_ref = refs[2 * n_cast + 1:]
    n_sub = FFN_ROWS // FFN_SUB
    gain = g_ref[...]

    def rows(i):
        return slice(i * FFN_SUB, (i + 1) * FFN_SUB)

    def norm(i):
        h_ref[rows(i), :] = _rmsnorm(x_ref[rows(i), :], gain).astype(BF16)

    def gate_up(i):
        h = h_ref[rows(i), :]
        for c in range(D_FF // FFN_COLS):
            lo = c * FFN_COLS
            gate = _dot(h, wgu_ref[:, lo:lo + FFN_COLS])
            up = _dot(h, wgu_ref[:, D_FF + lo:D_FF + lo + FFN_COLS])
            act_ref[rows(i), lo:lo + FFN_COLS] = (_silu(gate) * up).astype(BF16)

    def down(i):
        y = x_ref[rows(i), :] + 0.5 * _dot(act_ref[rows(i), :], wd_ref[...])
        if final_norm:
            y = _rmsnorm(y, fin_ref[...])
        o_ref[rows(i), :] = y

    norm(0)
    for i in range(n_sub):
        if i + 1 < n_sub:
            norm(i + 1)
        gate_up(i)
        if i >= 1:
            down(i - 1)
    down(n_sub - 1)

    for src, dst in zip(cast_in, cast_out):
        dst[...] = src[...].astype(BF16)


def _ffn(x2d, norm, w_gu, w_down, fin, *, final_norm, cast=()):
    n = x2d.shape[0]
    tm = FFN_ROWS
    steps = n // tm
    slices = [w.shape[0] // steps for w in cast]
    assert all(s % BF16_ROWS == 0 and s * steps == w.shape[0] for s, w in zip(slices, cast))
    est = (4 * tm * D_MODEL * 4 + w_gu.size * 2 + w_down.size * 2 + tm * D_FF * 2
           + tm * D_MODEL * 2 + 4 * FFN_SUB * FFN_COLS * 4
           + 2 * sum(s * w.shape[1] for s, w in zip(slices, cast)) * (4 + 2))
    cast_specs = [pl.BlockSpec((s, w.shape[1]), lambda i: (i, 0)) for s, w in zip(slices, cast)]
    out = pl.pallas_call(
        functools.partial(_ffn_kernel, final_norm=final_norm, n_cast=len(cast)),
        grid=(steps,),
        in_specs=[
            pl.BlockSpec((tm, D_MODEL), lambda i: (i, 0)),
            _resident((1, D_MODEL)),
            _resident((D_MODEL, 2 * D_FF)),
            _resident((D_FF, D_MODEL)),
            _resident((1, D_MODEL)),
        ] + cast_specs,
        out_specs=[pl.BlockSpec((tm, D_MODEL), lambda i: (i, 0))] + cast_specs,
        out_shape=[jax.ShapeDtypeStruct((n, D_MODEL), F32)]
        + [jax.ShapeDtypeStruct(w.shape, BF16) for w in cast],
        scratch_shapes=[pltpu.VMEM((tm, D_MODEL), BF16), pltpu.VMEM((tm, D_FF), BF16)],
        compiler_params=pltpu.CompilerParams(
            dimension_semantics=("arbitrary",), vmem_limit_bytes=_vmem_limit(est)),
        name="ffn_final" if final_norm else "ffn",
    )(x2d, norm.reshape(1, D_MODEL), w_gu, w_down, fin.reshape(1, D_MODEL), *cast)
    return out[0], list(out[1:])


def _chunk_cumsum(g):
    n = g.shape[-1]
    nb = CHUNK // SUB
    row = lax.broadcasted_iota(jnp.int32, (nb, SUB, n), 1)
    s = g.reshape(nb, SUB, n)
    for shift in (1, 2, 4):
        s = s + jnp.where(row >= shift, pltpu.roll(s, shift, 1), 0.0)
    carry = [jnp.zeros((1, 1, n), F32)]
    for i in range(1, nb):
        carry.append(carry[-1] + s[i - 1:i, SUB - 1:SUB, :])
    return (s + jnp.concatenate(carry, axis=0)).reshape(CHUNK, n)


def _hgrn2_masks():
    t = lax.broadcasted_iota(jnp.int32, (CHUNK, CHUNK), 0)
    s = lax.broadcasted_iota(jnp.int32, (CHUNK, CHUNK), 1)
    levels = []
    blk = CHUNK
    while blk > DIAG:
        half = blk // 2
        levels.append((t // blk == s // blk) & (t % blk >= half) & (s % blk < half))
        blk = half
    diag = [(s == t - d) & (t % DIAG >= d) for d in range(DIAG)]
    return levels, diag


def _mix_kernel(x_ref, xn_ref, nrm_ref, win_ref, lbl_ref, hgn_ref, cw_ref, cn_ref, wout_ref, o_ref,
                p_a, p_b, mix_a, mix_b, h_a, h_b, st_ref, ybuf_ref):
    n_macro = MIX_ROWS // MACRO

    @pl.when(pl.program_id(1) == 0)
    def _():
        st_ref[...] = jnp.zeros_like(st_ref)
        ybuf_ref[0:SUB, :] = jnp.zeros((SUB, CONV_WIDTH), F32)

    mix_b[...] = jnp.zeros((MACRO, D_MODEL), BF16)

    lbl = lbl_ref[...]
    lmax = jnp.max(lbl, axis=0, keepdims=True)
    lexp = jnp.exp(lbl - lmax)
    lb_all = lexp[0:1, :] / jnp.sum(lexp, axis=0, keepdims=True)

    level_masks, diag_masks = _hgrn2_masks()
    hgn = hgn_ref[...]
    nrm = nrm_ref[...]
    cw = cw_ref[...]
    cn = cn_ref[...]
    cb = 4 * HG_WIDTH

    def macro_rows(m):
        return pl.ds(pl.multiple_of(m * MACRO, MACRO), MACRO)

    def in_norm(m, h_ref):
        h_ref[...] = _rmsnorm(x_ref[0, macro_rows(m), :], nrm).astype(BF16)

    def in_norm_or_next(m, h_ref):
        rows = x_ref[0, macro_rows(jnp.minimum(m, n_macro - 1)), :]
        rows = jnp.where(m >= n_macro, xn_ref[0], rows)
        h_ref[...] = _rmsnorm(rows, nrm).astype(BF16)

    def in_proj_piece(h_ref, p_ref, c):
        cols = slice(c * PIECE, (c + 1) * PIECE)
        p_ref[:, cols] = _dot(h_ref[...], win_ref[:, cols])

    def out_proj_piece(m, mix_ref, c):
        rows = macro_rows(m)
        cols = slice(c * PIECE, (c + 1) * PIECE)
        o_ref[0, rows, cols] = x_ref[0, rows, cols] + _dot(mix_ref[...], wout_ref[:, cols])


    def hg_ab(p_ref, rows, hh):
        c0 = hh * HG_DK
        q = p_ref[rows, pl.ds(c0, HG_DK)]
        fz = p_ref[rows, pl.ds(HG_WIDTH + c0, HG_DK)]
        v = p_ref[rows, pl.ds(2 * HG_WIDTH + c0, HG_DK)]
        gt = p_ref[rows, pl.ds(3 * HG_WIDTH + c0, HG_DK)]
        lb = lb_all[:, c0:c0 + HG_DK]
        f = lb + (1.0 - lb) * (1.0 / (1.0 + jnp.exp(-fz)))
        k = 1.0 - f
        qs = _silu(q)
        b = _chunk_cumsum(jnp.log2(f))
        b_end = b[CHUNK - 1:CHUNK, :]
        u = dict(hh=hh, rows=rows, v_bf=v.astype(BF16), gate=_silu(gt))
        q_bf, k_bf = qs.astype(BF16), k.astype(BF16)
        ops = []
        blk = CHUNK
        for _ in level_masks:
            half = blk // 2
            nb = CHUNK // blk
            ref = b.reshape(nb, blk, HG_DK)[:, half - 1:half, :]
            ref = jnp.broadcast_to(ref, (nb, blk, HG_DK)).reshape(CHUNK, HG_DK)
            z = jnp.exp2(-jnp.abs(b - ref)).astype(BF16)
            ops.append((q_bf * z, k_bf * z))
            blk = half
        u["level_ops"] = ops
        attn = jnp.where(diag_masks[0], jnp.sum(qs * k, axis=-1, keepdims=True), 0.0)
        f3 = f.reshape(CHUNK // SUB, SUB, HG_DK)
        decay = f
        for d in range(1, DIAG):
            f_d = pltpu.roll(f3, d, 1).reshape(CHUNK, HG_DK)
            r_d = jnp.sum(qs * (1.0 - f_d) * decay, axis=-1, keepdims=True)
            attn = jnp.where(diag_masks[d], r_d, attn)
            decay = decay * f_d
        u["attn_diag"] = attn
        u["qe"] = (qs * jnp.exp2(b)).astype(BF16)
        u["ke"] = (k * jnp.exp2(b_end - b)).astype(BF16)
        u["decay"] = jnp.exp2(b_end)
        return u

    def hg_mb(u):
        u["level_out"] = [_dot_nt(ql, kl) for ql, kl in u.pop("level_ops")]

    def hg_c(u):
        attn = u.pop("attn_diag")
        for lm, a in zip(level_masks, u.pop("level_out")):
            attn = jnp.where(lm, a, attn)
        u["attn"] = attn.astype(BF16)

    def hg_mc(u):
        hh, v_bf = u["hh"], u.pop("v_bf")
        st = st_ref[hh]
        o = _dot(u.pop("attn"), v_bf)
        u["o"] = o + _dot(u.pop("qe"), st.astype(BF16))
        decay_col = jnp.transpose(jnp.broadcast_to(u.pop("decay"), (HG_DK, HG_DK)))
        st_ref[hh] = st * decay_col + _dot_tn(u.pop("ke"), v_bf)

    def hg_d(u, mix_ref):
        on = _rmsnorm(u.pop("o"), hgn) * u.pop("gate")
        mix_ref[u["rows"], pl.ds(u["hh"] * HG_DK, HG_DK)] = on.astype(BF16)

    def short_conv(p_ref, mix_ref, rows):
        y = p_ref[rows, cb + CONV_WIDTH:cb + 2 * CONV_WIDTH] * \
            p_ref[rows, cb + 2 * CONV_WIDTH:cb + 3 * CONV_WIDTH]
        ybuf_ref[SUB:SUB + CHUNK, :] = y
        y1 = ybuf_ref[SUB - 1:SUB - 1 + CHUNK, :]
        y2 = ybuf_ref[SUB - 2:SUB - 2 + CHUNK, :]
        conv = cw[0:1, :] * y2 + cw[1:2, :] * y1 + cw[2:3, :] * y
        ocv = _rmsnorm(p_ref[rows, cb:cb + CONV_WIDTH] * conv, cn)
        mix_ref[rows, HG_WIDTH:HG_WIDTH + CONV_WIDTH] = ocv.astype(BF16)
        ybuf_ref[0:SUB, :] = ybuf_ref[CHUNK:CHUNK + SUB, :]

    def each(phase, units, *args):
        for u in units:
            phase(u, *args)

    n_chunks = MACRO // CHUNK
    chunk_rows = [pl.ds(j * CHUNK, CHUNK) for j in range(n_chunks)]
    head_groups = [tuple(range(h, h + GROUP_HEADS)) for h in range(0, HG_HEADS, GROUP_HEADS)]
    last = n_macro - 1

    @pl.when((pl.program_id(0) == 0) & (pl.program_id(1) == 0))
    def _():
        in_norm(0, h_b)
        for c in range(IN_COLS // PIECE):
            in_proj_piece(h_b, p_a, c)

    in_norm(1, h_a)

    def trip(j, carry):
        m = 2 * j
        big_a = [functools.partial(in_proj_piece, h_a, p_b, c) for c in range(IN_COLS // PIECE)]
        big_a += [functools.partial(out_proj_piece, jnp.maximum(m - 1, 0), mix_b, c)
                  for c in range(D_MODEL // PIECE)]
        big_b = [functools.partial(in_proj_piece, h_b, p_a, c) for c in range(IN_COLS // PIECE)]
        big_b += [functools.partial(out_proj_piece, m, mix_a, c) for c in range(D_MODEL // PIECE)]
        groups = [(p_ref, mix_ref, r, hs, hs[0] == 0)
                  for p_ref, mix_ref in ((p_a, mix_a), (p_b, mix_b)) for r in chunk_rows
                  for hs in head_groups]
        ng = len(groups)
        na = ng // 2
        plan = {i: [] for i in range(-1, ng + 1)}
        n_in = IN_COLS // PIECE
        for pieces, slots in ((big_a, range(-1, na)), (big_b, range(na, ng + 1))):
            for k, piece in enumerate(pieces):
                plan[slots[k * len(slots) // len(pieces)]].append(piece)
        assert range(na, ng + 1)[n_in * (ng + 1 - na) // len(big_b)] > na + 1

        def issue(slot):
            for piece in plan[slot]:
                piece()

        issue(-1)
        units = []
        for i in range(ng + 1):
            if i < ng:
                p_ref, _, rows, hs, _ = groups[i]
                units.append([hg_ab(p_ref, rows, hh) for hh in hs])
                each(hg_mb, units[i])
            if i >= 1:
                p_ref, mix_ref, rows, _, with_conv = groups[i - 1]
                if with_conv:
                    short_conv(p_ref, mix_ref, rows)
                each(hg_c, units[i - 1])
                each(hg_mc, units[i - 1])
            if i >= 2:
                each(hg_d, units[i - 2], groups[i - 2][1])
            if i == na - 1:
                in_norm_or_next(m + 2, h_b)
            if i == ng - 1:
                in_norm(jnp.minimum(m + 3, last), h_a)
            issue(i)
        each(hg_d, units[ng - 1], groups[ng - 1][1])
        return carry

    lax.fori_loop(0, n_macro // 2, trip, 0, unroll=2)
    for c in range(D_MODEL // PIECE):
        out_proj_piece(last, mix_b, c)


def _mix(x, norm, w_in, lb_logits, hg_norm, conv_w, conv_norm, w_out):
    bsz, t, _ = x.shape
    tt = MIX_ROWS
    tiles = t // tt

    def next_group(b, i):
        step = jnp.minimum(b * tiles + i + 1, bsz * tiles - 1)
        return step // tiles, (step % tiles) * (tt // MACRO), 0

    est = (4 * tt * D_MODEL * 4 + 2 * MACRO * D_MODEL * 4 + w_in.size * 2 + w_out.size * 2 + 2 * MACRO * IN_COLS * 4
           + HG_HEADS * HG_DK * HG_DK * 4 + (CHUNK + SUB) * CONV_WIDTH * 4
           + 4 * MACRO * D_MODEL * 2)
    return pl.pallas_call(
        _mix_kernel,
        grid=(bsz, t // tt),
        in_specs=[
            pl.BlockSpec((1, tt, D_MODEL), lambda b, i: (b, i, 0)),
            pl.BlockSpec((1, MACRO, D_MODEL), next_group),
            _resident((1, D_MODEL)),
            _resident((D_MODEL, IN_COLS)),
            _resident(lb_logits.shape),
            _resident((1, HG_DK)),
            _resident((CONV_K, CONV_WIDTH)),
            _resident((1, CONV_WIDTH)),
            _resident((D_MODEL, D_MODEL)),
        ],
        out_specs=pl.BlockSpec((1, tt, D_MODEL), lambda b, i: (b, i, 0)),
        out_shape=jax.ShapeDtypeStruct(x.shape, F32),
        scratch_shapes=[
            pltpu.VMEM((MACRO, IN_COLS), F32),
            pltpu.VMEM((MACRO, IN_COLS), F32),
            pltpu.VMEM((MACRO, D_MODEL), BF16),
            pltpu.VMEM((MACRO, D_MODEL), BF16),
            pltpu.VMEM((MACRO, D_MODEL), BF16),
            pltpu.VMEM((MACRO, D_MODEL), BF16),
            pltpu.VMEM((HG_HEADS, HG_DK, HG_DK), F32),
            pltpu.VMEM((CHUNK + SUB, CONV_WIDTH), F32),
        ],
        compiler_params=pltpu.CompilerParams(
            dimension_semantics=("arbitrary", "arbitrary"), vmem_limit_bytes=_vmem_limit(est)),
        name="mix",
    )(x, x, norm.reshape(1, D_MODEL), w_in, lb_logits, hg_norm.reshape(1, HG_DK), conv_w,
      conv_norm.reshape(1, CONV_WIDTH), w_out)


def _kv_kernel(m_ref, g_ref, w_ref, o_ref):
    h = _rmsnorm(m_ref[...], g_ref[...]).astype(BF16)
    o_ref[...] = _dot(h, w_ref[...]).astype(BF16)


def _kv(mem2d, norm, w_kv):
    n = mem2d.shape[0]
    est = 2 * KV_ROWS * D_MODEL * 4 + w_kv.size * 2 + 2 * KV_ROWS * 2 * D_MODEL * 2
    return pl.pallas_call(
        _kv_kernel,
        grid=(n // KV_ROWS,),
        in_specs=[
            pl.BlockSpec((KV_ROWS, D_MODEL), lambda i: (i, 0)),
            _resident((1, D_MODEL)),
            _resident((D_MODEL, 2 * D_MODEL)),
        ],
        out_specs=pl.BlockSpec((KV_ROWS, 2 * D_MODEL), lambda i: (i, 0)),
        out_shape=jax.ShapeDtypeStruct((n, 2 * D_MODEL), BF16),
        compiler_params=pltpu.CompilerParams(
            dimension_semantics=("arbitrary",), vmem_limit_bytes=_vmem_limit(est)),
        name="kv",
    )(mem2d, norm.reshape(1, D_MODEL), w_kv)


def _xattn_kernel(x_ref, g_ref, wq_ref, kv_ref, wo_ref, o_ref, h_ref, q_ref, s_ref, p_ref,
                  att_ref):
    n_sub = XA_ROWS // XA_SUB
    scale = 1.0 / math.sqrt(XA_HEAD_DIM)
    gain = g_ref[...]
    heads = [slice(hh * XA_HEAD_DIM, (hh + 1) * XA_HEAD_DIM) for hh in range(XA_HEADS)]

    def rows(i):
        return slice(i * XA_SUB, (i + 1) * XA_SUB)

    def norm(i):
        h_ref[i % 2] = _rmsnorm(x_ref[0, rows(i), :], gain).astype(BF16)

    def q_proj(i):
        for hd in heads:
            q_ref[i % 2, :, hd] = (_dot(h_ref[i % 2], wq_ref[:, hd]) * scale).astype(BF16)

    def scores(i):
        for hd in heads:
            s_ref[i % 2, :, hd] = _dot_nt(q_ref[i % 2, :, hd], kv_ref[0, :, hd])

    def softmax(i):
        for hd in heads:
            s = s_ref[i % 2, :, hd]
            e = jnp.exp(s - jnp.max(s, axis=-1, keepdims=True))
            p_ref[i % 2, :, hd] = (e * (1.0 / jnp.sum(e, axis=-1, keepdims=True))).astype(BF16)

    def pv(i):
        for hh, hd in enumerate(heads):
            v = kv_ref[0, :, D_MODEL + hh * XA_HEAD_DIM:D_MODEL + (hh + 1) * XA_HEAD_DIM]
            att_ref[i % 2, :, hd] = _dot(p_ref[i % 2, :, hd], v).astype(BF16)

    def out_proj(i):
        o_ref[0, rows(i), :] = x_ref[0, rows(i), :] + _dot(att_ref[i % 2], wo_ref[...])

    norm(0)
    if n_sub > 1:
        norm(1)
    q_proj(0)
    scores(0)
    for i in range(n_sub):
        if i + 1 < n_sub:
            q_proj(i + 1)
        softmax(i)
        pv(i)
        if i + 1 < n_sub:
            scores(i + 1)
        if i + 2 < n_sub:
            norm(i + 2)
        out_proj(i)


def _xattn(x, norm, w_q, kv, w_o):
    bsz, t, _ = x.shape
    tm = XA_ROWS
    est = (4 * tm * D_MODEL * 4 + w_q.size * 2 + w_o.size * 2 + 2 * N_MEM * 2 * D_MODEL * 2
           + 2 * XA_SUB * D_MODEL * (4 + 4 * 2))
    return pl.pallas_call(
        _xattn_kernel,
        grid=(bsz, t // tm),
        in_specs=[
            pl.BlockSpec((1, tm, D_MODEL), lambda b, i: (b, i, 0)),
            _resident((1, D_MODEL)),
            _resident((D_MODEL, D_MODEL)),
            pl.BlockSpec((1, N_MEM, 2 * D_MODEL), lambda b, i: (b, 0, 0)),
            _resident((D_MODEL, D_MODEL)),
        ],
        out_specs=pl.BlockSpec((1, tm, D_MODEL), lambda b, i: (b, i, 0)),
        out_shape=jax.ShapeDtypeStruct(x.shape, F32),
        scratch_shapes=[
            pltpu.VMEM((2, XA_SUB, D_MODEL), BF16),
            pltpu.VMEM((2, XA_SUB, D_MODEL), BF16),
            pltpu.VMEM((2, XA_SUB, D_MODEL), F32),
            pltpu.VMEM((2, XA_SUB, D_MODEL), BF16),
            pltpu.VMEM((2, XA_SUB, D_MODEL), BF16),
        ],
        compiler_params=pltpu.CompilerParams(
            dimension_semantics=("arbitrary", "arbitrary"), vmem_limit_bytes=_vmem_limit(est)),
        name="xattn",
    )(x, norm.reshape(1, D_MODEL), w_q, kv, w_o)


def kernel(x, mem, ffn1_norm, ffn1_w_gu, ffn1_w_down, mix_norm, w_in, lb_logits, hg_norm, conv_w,
           conv_norm, w_out, xa_norm, mem_norm, xa_wq, xa_wkv, xa_wo, ffn2_norm, ffn2_w_gu,
           ffn2_w_down, final_norm):
    bsz, t, d = x.shape
    assert d == D_MODEL and t % MIX_ROWS == 0 and (bsz * t) % FFN_ROWS == 0
    assert ffn1_norm.shape[0] == 1, "single-layer block"
    later = [w_in[0], w_out[0], xa_wq[0], xa_wkv[0], xa_wo[0], ffn2_w_gu[0], ffn2_w_down[0]]
    x1, later_bf = _ffn(x.reshape(bsz * t, d), ffn1_norm[0], ffn1_w_gu[0].astype(BF16),
                        ffn1_w_down[0].astype(BF16), final_norm, final_norm=False, cast=later)
    w_in_bf, w_out_bf, wq_bf, wkv_bf, wo_bf, w_gu2_bf, w_down2_bf = later_bf
    x2 = _mix(x1.reshape(bsz, t, d), mix_norm[0], w_in_bf, lb_logits, hg_norm[0], conv_w[0],
              conv_norm[0], w_out_bf)
    kv = _kv(mem.reshape(bsz * N_MEM, d), mem_norm[0], wkv_bf).reshape(bsz, N_MEM, 2 * d)
    x3 = _xattn(x2, xa_norm[0], wq_bf, kv, wo_bf)
    y, _ = _ffn(x3.reshape(bsz * t, d), ffn2_norm[0], w_gu2_bf, w_down2_bf, final_norm,
                final_norm=True)
    return y.reshape(bsz, t, d)
```

```python
import functools
import math

import jax
import jax.numpy as jnp
from jax import lax
from jax.experimental import pallas as pl
from jax.experimental.pallas import tpu as pltpu

D_MODEL = 1024
CHUNK = 64
SUB = 8
DIAG = 4
HG_WIDTH = 512
CONV_WIDTH = 512
HG_HEADS = 4
HG_DK = 128
CONV_K = 3
N_MEM = 256
XA_HEADS = 4
XA_HEAD_DIM = 256
D_FF = 2816
IN_COLS = 4 * HG_WIDTH + 3 * CONV_WIDTH
EPS = 1e-6

V7X_VMEM_BYTES = 64 * 1024 * 1024
MIB = 1024 * 1024

FFN_ROWS = 1024
FFN_SUB = 1024
FFN_COLS = 256
MIX_ROWS = 2048
MACRO = 128
PIECE = 256
GROUP_HEADS = 2
XA_ROWS = 2048
XA_SUB = 512
KV_ROWS = 256

BF16 = jnp.bfloat16
F32 = jnp.float32
BF16_ROWS = 16


def _vmem_limit(estimate_bytes):
    return int(min(estimate_bytes * 5 // 4 + 8 * MIB, V7X_VMEM_BYTES - 6 * MIB))


def _resident(shape):
    nd = len(shape)
    return pl.BlockSpec(shape, lambda *_: (0,) * nd, pipeline_mode=pl.Buffered(1))


def _rmsnorm(x, g):
    ms = jnp.mean(x * x, axis=-1, keepdims=True)
    return x * lax.rsqrt(ms + EPS) * g


def _silu(x):
    return x * (1.0 / (1.0 + jnp.exp(-x)))


def _dot(a, b):
    return jnp.dot(a, b, preferred_element_type=F32)


def _dot_nt(a, b):
    return lax.dot_general(a, b, (((1,), (1,)), ((), ())), preferred_element_type=F32)


def _dot_tn(a, b):
    return lax.dot_general(a, b, (((0,), (0,)), ((), ())), preferred_element_type=F32)


def _ffn_kernel(x_ref, g_ref, wgu_ref, wd_ref, fin_ref, *refs, final_norm, n_cast):
    cast_in, o_ref, cast_out = refs[:n_cast], refs[n_cast], refs[n_cast + 1:2 * n_cast + 1]
    h_ref, act_ref = refs[2 * n_cast + 1:]
    n_sub = FFN_ROWS // FFN_SUB
    gain = g_ref[...]

    def rows(i):
        return slice(i * FFN_SUB, (i + 1) * FFN_SUB)

    def norm(i):
        h_ref[rows(i), :] = _rmsnorm(x_ref[rows(i), :], gain).astype(BF16)

    def gate_up(i):
        h = h_ref[rows(i), :]
        for c in range(D_FF // FFN_COLS):
            lo = c * FFN_COLS
            gate = _dot(h, wgu_ref[:, lo:lo + FFN_COLS])
            up = _dot(h, wgu_ref[:, D_FF + lo:D_FF + lo + FFN_COLS])
            act_ref[rows(i), lo:lo + FFN_COLS] = (_silu(gate) * up).astype(BF16)

    def down(i):
        y = x_ref[rows(i), :] + 0.5 * _dot(act_ref[rows(i), :], wd_ref[...])
        if final_norm:
            y = _rmsnorm(y, fin_ref[...])
        o_ref[rows(i), :] = y

    norm(0)
    for i in range(n_sub):
        if i + 1 < n_sub:
            norm(i + 1)
        gate_up(i)
        if i >= 1:
            down(i - 1)
    down(n_sub - 1)

    for src, dst in zip(cast_in, cast_out):
        dst[...] = src[...].astype(BF16)


def _ffn(x2d, norm, w_gu, w_down, fin, *, final_norm, cast=()):
    n = x2d.shape[0]
    tm = FFN_ROWS
    steps = n // tm
    slices = [w.shape[0] // steps for w in cast]
    assert all(s % BF16_ROWS == 0 and s * steps == w.shape[0] for s, w in zip(slices, cast))
    est = (4 * tm * D_MODEL * 4 + w_gu.size * 2 + w_down.size * 2 + tm * D_FF * 2
           + tm * D_MODEL * 2 + 4 * FFN_SUB * FFN_COLS * 4
           + 2 * sum(s * w.shape[1] for s, w in zip(slices, cast)) * (4 + 2))
    cast_specs = [pl.BlockSpec((s, w.shape[1]), lambda i: (i, 0)) for s, w in zip(slices, cast)]
    out = pl.pallas_call(
        functools.partial(_ffn_kernel, final_norm=final_norm, n_cast=len(cast)),
        grid=(steps,),
        in_specs=[
            pl.BlockSpec((tm, D_MODEL), lambda i: (i, 0)),
            _resident((1, D_MODEL)),
            _resident((D_MODEL, 2 * D_FF)),
            _resident((D_FF, D_MODEL)),
            _resident((1, D_MODEL)),
        ] + cast_specs,
        out_specs=[pl.BlockSpec((tm, D_MODEL), lambda i: (i, 0))] + cast_specs,
        out_shape=[jax.ShapeDtypeStruct((n, D_MODEL), F32)]
        + [jax.ShapeDtypeStruct(w.shape, BF16) for w in cast],
        scratch_shapes=[pltpu.VMEM((tm, D_MODEL), BF16), pltpu.VMEM((tm, D_FF), BF16)],
        compiler_params=pltpu.CompilerParams(
            dimension_semantics=("arbitrary",), vmem_limit_bytes=_vmem_limit(est)),
        name="ffn_final" if final_norm else "ffn",
    )(x2d, norm.reshape(1, D_MODEL), w_gu, w_down, fin.reshape(1, D_MODEL), *cast)
    return out[0], list(out[1:])


def _chunk_cumsum(g):
    n = g.shape[-1]
    nb = CHUNK // SUB
    row = lax.broadcasted_iota(jnp.int32, (nb, SUB, n), 1)
    s = g.reshape(nb, SUB, n)
    for shift in (1, 2, 4):
        s = s + jnp.where(row >= shift, pltpu.roll(s, shift, 1), 0.0)
    carry = [jnp.zeros((1, 1, n), F32)]
    for i in range(1, nb):
        carry.append(carry[-1] + s[i - 1:i, SUB - 1:SUB, :])
    return (s + jnp.concatenate(carry, axis=0)).reshape(CHUNK, n)


def _hgrn2_masks():
    t = lax.broadcasted_iota(jnp.int32, (CHUNK, CHUNK), 0)
    s = lax.broadcasted_iota(jnp.int32, (CHUNK, CHUNK), 1)
    levels = []
    blk = CHUNK
    while blk > DIAG:
        half = blk // 2
        levels.append((t // blk == s // blk) & (t % blk >= half) & (s % blk < half))
        blk = half
    diag = [(s == t - d) & (t % DIAG >= d) for d in range(DIAG)]
    return levels, diag


def _mix_kernel(x_ref, xn_ref, nrm_ref, win_ref, lbl_ref, hgn_ref, cw_ref, cn_ref, wout_ref, o_ref,
                p_a, p_b, mix_a, mix_b, h_a, h_b, st_ref, ybuf_ref):
    n_macro = MIX_ROWS // MACRO

    @pl.when(pl.program_id(1) == 0)
    def _():
        st_ref[...] = jnp.zeros_like(st_ref)
        ybuf_ref[0:SUB, :] = jnp.zeros((SUB, CONV_WIDTH), F32)

    mix_b[...] = jnp.zeros((MACRO, D_MODEL), BF16)

    lbl = lbl_ref[...]
    lmax = jnp.max(lbl, axis=0, keepdims=True)
    lexp = jnp.exp(lbl - lmax)
    lb_all = lexp[0:1, :] / jnp.sum(lexp, axis=0, keepdims=True)

    level_masks, diag_masks = _hgrn2_masks()
    hgn = hgn_ref[...]
    nrm = nrm_ref[...]
    cw = cw_ref[...]
    cn = cn_ref[...]
    cb = 4 * HG_WIDTH

    def macro_rows(m):
        return pl.ds(pl.multiple_of(m * MACRO, MACRO), MACRO)

    def in_norm(m, h_ref):
        h_ref[...] = _rmsnorm(x_ref[0, macro_rows(m), :], nrm).astype(BF16)

    def in_norm_or_next(m, h_ref):
        rows = x_ref[0, macro_rows(jnp.minimum(m, n_macro - 1)), :]
        rows = jnp.where(m >= n_macro, xn_ref[0], rows)
        h_ref[...] = _rmsnorm(rows, nrm).astype(BF16)

    def in_proj_piece(h_ref, p_ref, c):
        cols = slice(c * PIECE, (c + 1) * PIECE)
        p_ref[:, cols] = _dot(h_ref[...], win_ref[:, cols])

    def out_proj_piece(m, mix_ref, c):
        rows = macro_rows(m)
        cols = slice(c * PIECE, (c + 1) * PIECE)
        o_ref[0, rows, cols] = x_ref[0, rows, cols] + _dot(mix_ref[...], wout_ref[:, cols])


    def hg_ab(p_ref, rows, hh):
        c0 = hh * HG_DK
        q = p_ref[rows, pl.ds(c0, HG_DK)]
        fz = p_ref[rows, pl.ds(HG_WIDTH + c0, HG_DK)]
        v = p_ref[rows, pl.ds(2 * HG_WIDTH + c0, HG_DK)]
        lb = lb_all[:, c0:c0 + HG_DK]
        f = lb + (1.0 - lb) * (1.0 / (1.0 + jnp.exp(-fz)))
        k = 1.0 - f
        qs = _silu(q)
        b = _chunk_cumsum(jnp.log2(f))
        b_end = b[CHUNK - 1:CHUNK, :]
        u = dict(hh=hh, rows=rows, p_ref=p_ref, v_bf=v.astype(BF16))
        q_bf, k_bf = qs.astype(BF16), k.astype(BF16)
        ops = []
        blk = CHUNK
        for _ in level_masks:
            half = blk // 2
            nb = CHUNK // blk
            ref = b.reshape(nb, blk, HG_DK)[:, half - 1:half, :]
            ref = jnp.broadcast_to(ref, (nb, blk, HG_DK)).reshape(CHUNK, HG_DK)
            z = jnp.exp2(-jnp.abs(b - ref)).astype(BF16)
            ops.append((q_bf * z, k_bf * z))
            blk = half
        u["level_ops"] = ops
        attn = jnp.where(diag_masks[0], jnp.sum(qs * k, axis=-1, keepdims=True), 0.0)
        f3 = f.reshape(CHUNK // SUB, SUB, HG_DK)
        decay = f
        for d in range(1, DIAG):
            f_d = pltpu.roll(f3, d, 1).reshape(CHUNK, HG_DK)
            r_d = jnp.sum(qs * (1.0 - f_d) * decay, axis=-1, keepdims=True)
            attn = jnp.where(diag_masks[d], r_d, attn)
            decay = decay * f_d
        u["attn_diag"] = attn
        u["qe"] = (qs * jnp.exp2(b)).astype(BF16)
        u["ke"] = (k * jnp.exp2(b_end - b)).astype(BF16)
        u["decay"] = jnp.exp2(b_end)
        return u

    def hg_mb(u):
        u["level_out"] = [_dot_nt(ql, kl) for ql, kl in u.pop("level_ops")]

    def hg_c(u):
        attn = u.pop("attn_diag")
        for lm, a in zip(level_masks, u.pop("level_out")):
            attn = jnp.where(lm, a, attn)
        u["attn"] = attn.astype(BF16)

    def hg_mc(u):
        hh, v_bf = u["hh"], u.pop("v_bf")
        st = st_ref[hh]
        o = _dot(u.pop("attn"), v_bf)
        u["o"] = o + _dot(u.pop("qe"), st.astype(BF16))
        decay_col = jnp.transpose(jnp.broadcast_to(u.pop("decay"), (HG_DK, HG_DK)))
        st_ref[hh] = st * decay_col + _dot_tn(u.pop("ke"), v_bf)

    def hg_d(u, mix_ref):
        c0 = u["hh"] * HG_DK
        gt = u["p_ref"][u["rows"], pl.ds(3 * HG_WIDTH + c0, HG_DK)]
        on = _rmsnorm(u.pop("o"), hgn) * _silu(gt)
        mix_ref[u["rows"], pl.ds(c0, HG_DK)] = on.astype(BF16)

    def short_conv(p_ref, mix_ref, rows):
        y = p_ref[rows, cb + CONV_WIDTH:cb + 2 * CONV_WIDTH] * \
            p_ref[rows, cb + 2 * CONV_WIDTH:cb + 3 * CONV_WIDTH]
        ybuf_ref[SUB:SUB + CHUNK, :] = y
        y1 = ybuf_ref[SUB - 1:SUB - 1 + CHUNK, :]
        y2 = ybuf_ref[SUB - 2:SUB - 2 + CHUNK, :]
        conv = cw[0:1, :] * y2 + cw[1:2, :] * y1 + cw[2:3, :] * y
        ocv = _rmsnorm(p_ref[rows, cb:cb + CONV_WIDTH] * conv, cn)
        mix_ref[rows, HG_WIDTH:HG_WIDTH + CONV_WIDTH] = ocv.astype(BF16)
        ybuf_ref[0:SUB, :] = ybuf_ref[CHUNK:CHUNK + SUB, :]

    def each(phase, units, *args):
        for u in units:
            phase(u, *args)

    n_chunks = MACRO // CHUNK
    chunk_rows = [pl.ds(j * CHUNK, CHUNK) for j in range(n_chunks)]
    head_groups = [tuple(range(h, h + GROUP_HEADS)) for h in range(0, HG_HEADS, GROUP_HEADS)]
    last = n_macro - 1

    @pl.when((pl.program_id(0) == 0) & (pl.program_id(1) == 0))
    def _():
        in_norm(0, h_b)
        for c in range(IN_COLS // PIECE):
            in_proj_piece(h_b, p_a, c)

    in_norm(1, h_a)

    def trip(j, carry):
        m = 2 * j
        big_a = [functools.partial(in_proj_piece, h_a, p_b, c) for c in range(IN_COLS // PIECE)]
        big_a += [functools.partial(out_proj_piece, jnp.maximum(m - 1, 0), mix_b, c)
                  for c in range(D_MODEL // PIECE)]
        big_b = [functools.partial(in_proj_piece, h_b, p_a, c) for c in range(IN_COLS // PIECE)]
        big_b += [functools.partial(out_proj_piece, m, mix_a, c) for c in range(D_MODEL // PIECE)]
        groups = [(p_ref, mix_ref, r, hs, hs[0] == 0)
                  for p_ref, mix_ref in ((p_a, mix_a), (p_b, mix_b)) for r in chunk_rows
                  for hs in head_groups]
        ng = len(groups)
        na = ng // 2
        plan = {i: [] for i in range(-1, ng + 1)}
        for pieces, slots in ((big_a, range(-1, na + 1)), (big_b, range(na + 1, ng + 1))):
            for k, piece in enumerate(pieces):
                plan[slots[k * len(slots) // len(pieces)]].append(piece)

        def issue(slot):
            for piece in plan[slot]:
                piece()

        issue(-1)
        units = []
        for i in range(ng + 1):
            if i < ng:
                p_ref, _, rows, hs, _ = groups[i]
                units.append([hg_ab(p_ref, rows, hh) for hh in hs])
                each(hg_mb, units[i])
            if i >= 1:
                p_ref, mix_ref, rows, _, with_conv = groups[i - 1]
                if with_conv:
                    short_conv(p_ref, mix_ref, rows)
                each(hg_c, units[i - 1])
                each(hg_mc, units[i - 1])
            if i >= 2:
                each(hg_d, units[i - 2], groups[i - 2][1])
            if i == na - 1:
                in_norm_or_next(m + 2, h_b)
            if i == ng - 1:
                in_norm(jnp.minimum(m + 3, last), h_a)
            issue(i)
        each(hg_d, units[ng - 1], groups[ng - 1][1])
        return carry

    lax.fori_loop(0, n_macro // 2, trip, 0, unroll=2)
    for c in range(D_MODEL // PIECE):
        out_proj_piece(last, mix_b, c)


def _mix(x, norm, w_in, lb_logits, hg_norm, conv_w, conv_norm, w_out):
    bsz, t, _ = x.shape
    tt = MIX_ROWS
    tiles = t // tt

    def next_group(b, i):
        step = jnp.minimum(b * tiles + i + 1, bsz * tiles - 1)
        return step // tiles, (step % tiles) * (tt // MACRO), 0

    est = (4 * tt * D_MODEL * 4 + 2 * MACRO * D_MODEL * 4 + w_in.size * 2 + w_out.size * 2 + 2 * MACRO * IN_COLS * 4
           + HG_HEADS * HG_DK * HG_DK * 4 + (CHUNK + SUB) * CONV_WIDTH * 4
           + 4 * MACRO * D_MODEL * 2)
    return pl.pallas_call(
        _mix_kernel,
        grid=(bsz, t // tt),
        in_specs=[
            pl.BlockSpec((1, tt, D_MODEL), lambda b, i: (b, i, 0)),
            pl.BlockSpec((1, MACRO, D_MODEL), next_group),
            _resident((1, D_MODEL)),
            _resident((D_MODEL, IN_COLS)),
            _resident(lb_logits.shape),
            _resident((1, HG_DK)),
            _resident((CONV_K, CONV_WIDTH)),
            _resident((1, CONV_WIDTH)),
            _resident((D_MODEL, D_MODEL)),
        ],
        out_specs=pl.BlockSpec((1, tt, D_MODEL), lambda b, i: (b, i, 0)),
        out_shape=jax.ShapeDtypeStruct(x.shape, F32),
        scratch_shapes=[
            pltpu.VMEM((MACRO, IN_COLS), F32),
            pltpu.VMEM((MACRO, IN_COLS), F32),
            pltpu.VMEM((MACRO, D_MODEL), BF16),
            pltpu.VMEM((MACRO, D_MODEL), BF16),
            pltpu.VMEM((MACRO, D_MODEL), BF16),
            pltpu.VMEM((MACRO, D_MODEL), BF16),
            pltpu.VMEM((HG_HEADS, HG_DK, HG_DK), F32),
            pltpu.VMEM((CHUNK + SUB, CONV_WIDTH), F32),
        ],
        compiler_params=pltpu.CompilerParams(
            dimension_semantics=("arbitrary", "arbitrary"), vmem_limit_bytes=_vmem_limit(est)),
        name="mix",
    )(x, x, norm.reshape(1, D_MODEL), w_in, lb_logits, hg_norm.reshape(1, HG_DK), conv_w,
      conv_norm.reshape(1, CONV_WIDTH), w_out)


def _kv_kernel(m_ref, g_ref, w_ref, o_ref):
    h = _rmsnorm(m_ref[...], g_ref[...]).astype(BF16)
    o_ref[...] = _dot(h, w_ref[...]).astype(BF16)


def _kv(mem2d, norm, w_kv):
    n = mem2d.shape[0]
    est = 2 * KV_ROWS * D_MODEL * 4 + w_kv.size * 2 + 2 * KV_ROWS * 2 * D_MODEL * 2
    return pl.pallas_call(
        _kv_kernel,
        grid=(n // KV_ROWS,),
        in_specs=[
            pl.BlockSpec((KV_ROWS, D_MODEL), lambda i: (i, 0)),
            _resident((1, D_MODEL)),
            _resident((D_MODEL, 2 * D_MODEL)),
        ],
        out_specs=pl.BlockSpec((KV_ROWS, 2 * D_MODEL), lambda i: (i, 0)),
        out_shape=jax.ShapeDtypeStruct((n, 2 * D_MODEL), BF16),
        compiler_params=pltpu.CompilerParams(
            dimension_semantics=("arbitrary",), vmem_limit_bytes=_vmem_limit(est)),
        name="kv",
    )(mem2d, norm.reshape(1, D_MODEL), w_kv)


def _xattn_kernel(x_ref, g_ref, wq_ref, kv_ref, wo_ref, o_ref, h_ref, q_ref, s_ref, p_ref,
                  att_ref):
    n_sub = XA_ROWS // XA_SUB
    scale = 1.0 / math.sqrt(XA_HEAD_DIM)
    gain = g_ref[...]
    heads = [slice(hh * XA_HEAD_DIM, (hh + 1) * XA_HEAD_DIM) for hh in range(XA_HEADS)]

    def rows(i):
        return slice(i * XA_SUB, (i + 1) * XA_SUB)

    def norm(i):
        h_ref[i % 2] = _rmsnorm(x_ref[0, rows(i), :], gain).astype(BF16)

    def q_proj(i):
        for hd in heads:
            q_ref[i % 2, :, hd] = (_dot(h_ref[i % 2], wq_ref[:, hd]) * scale).astype(BF16)

    def scores(i):
        for hd in heads:
            s_ref[i % 2, :, hd] = _dot_nt(q_ref[i % 2, :, hd], kv_ref[0, :, hd])

    def softmax(i):
        for hd in heads:
            s = s_ref[i % 2, :, hd]
            e = jnp.exp(s - jnp.max(s, axis=-1, keepdims=True))
            p_ref[i % 2, :, hd] = (e * (1.0 / jnp.sum(e, axis=-1, keepdims=True))).astype(BF16)

    def pv(i):
        for hh, hd in enumerate(heads):
            v = kv_ref[0, :, D_MODEL + hh * XA_HEAD_DIM:D_MODEL + (hh + 1) * XA_HEAD_DIM]
            att_ref[i % 2, :, hd] = _dot(p_ref[i % 2, :, hd], v).astype(BF16)

    def out_proj(i):
        o_ref[0, rows(i), :] = x_ref[0, rows(i), :] + _dot(att_ref[i % 2], wo_ref[...])

    norm(0)
    if n_sub > 1:
        norm(1)
    q_proj(0)
    scores(0)
    for i in range(n_sub):
        if i + 1 < n_sub:
            q_proj(i + 1)
        softmax(i)
        pv(i)
        if i + 1 < n_sub:
            scores(i + 1)
        if i + 2 < n_sub:
            norm(i + 2)
        out_proj(i)


def _xattn(x, norm, w_q, kv, w_o):
    bsz, t, _ = x.shape
    tm = XA_ROWS
    est = (4 * tm * D_MODEL * 4 + w_q.size * 2 + w_o.size * 2 + 2 * N_MEM * 2 * D_MODEL * 2
           + 2 * XA_SUB * D_MODEL * (4 + 4 * 2))
    return pl.pallas_call(
        _xattn_kernel,
        grid=(bsz, t // tm),
        in_specs=[
            pl.BlockSpec((1, tm, D_MODEL), lambda b, i: (b, i, 0)),
            _resident((1, D_MODEL)),
            _resident((D_MODEL, D_MODEL)),
            pl.BlockSpec((1, N_MEM, 2 * D_MODEL), lambda b, i: (b, 0, 0)),
            _resident((D_MODEL, D_MODEL)),
        ],
        out_specs=pl.BlockSpec((1, tm, D_MODEL), lambda b, i: (b, i, 0)),
        out_shape=jax.ShapeDtypeStruct(x.shape, F32),
        scratch_shapes=[
            pltpu.VMEM((2, XA_SUB, D_MODEL), BF16),
            pltpu.VMEM((2, XA_SUB, D_MODEL), BF16),
            pltpu.VMEM((2, XA_SUB, D_MODEL), F32),
            pltpu.VMEM((2, XA_SUB, D_MODEL), BF16),
            pltpu.VMEM((2, XA_SUB, D_MODEL), BF16),
        ],
        compiler_params=pltpu.CompilerParams(
            dimension_semantics=("arbitrary", "arbitrary"), vmem_limit_bytes=_vmem_limit(est)),
        name="xattn",
    )(x, norm.reshape(1, D_MODEL), w_q, kv, w_o)


def kernel(x, mem, ffn1_norm, ffn1_w_gu, ffn1_w_down, mix_norm, w_in, lb_logits, hg_norm, conv_w,
           conv_norm, w_out, xa_norm, mem_norm, xa_wq, xa_wkv, xa_wo, ffn2_norm, ffn2_w_gu,
           ffn2_w_down, final_norm):
    bsz, t, d = x.shape
    assert d == D_MODEL and t % MIX_ROWS == 0 and (bsz * t) % FFN_ROWS == 0
    assert ffn1_norm.shape[0] == 1, "single-layer block"
    later = [w_in[0], w_out[0], xa_wq[0], xa_wkv[0], xa_wo[0], ffn2_w_gu[0], ffn2_w_down[0]]
    x1, later_bf = _ffn(x.reshape(bsz * t, d), ffn1_norm[0], ffn1_w_gu[0].astype(BF16),
                        ffn1_w_down[0].astype(BF16), final_norm, final_norm=False, cast=later)
    w_in_bf, w_out_bf, wq_bf, wkv_bf, wo_bf, w_gu2_bf, w_down2_bf = later_bf
    x2 = _mix(x1.reshape(bsz, t, d), mix_norm[0], w_in_bf, lb_logits, hg_norm[0], conv_w[0],
              conv_norm[0], w_out_bf)
    kv = _kv(mem.reshape(bsz * N_MEM, d), mem_norm[0], wkv_bf).reshape(bsz, N_MEM, 2 * d)
    x3 = _xattn(x2, xa_norm[0], wq_bf, kv, wo_bf)
    y, _ = _ffn(x3.reshape(bsz * t, d), ffn2_norm[0], w_gu2_bf, w_down2_bf, final_norm,
                final_norm=True)
    return y.reshape(bsz, t, d)
```

```python
import functools
import math

import jax
import jax.numpy as jnp
from jax import lax
from jax.experimental import pallas as pl
from jax.experimental.pallas import tpu as pltpu

D_MODEL = 1024
CHUNK = 64
SUB = 8
DIAG = 4
HG_WIDTH = 512
CONV_WIDTH = 512
HG_HEADS = 4
HG_DK = 128
CONV_K = 3
N_MEM = 256
XA_HEADS = 4
XA_HEAD_DIM = 256
D_FF = 2816
IN_COLS = 4 * HG_WIDTH + 3 * CONV_WIDTH
EPS = 1e-6

V7X_VMEM_BYTES = 64 * 1024 * 1024
MIB = 1024 * 1024

FFN_ROWS = 1024
FFN_SUB = 1024
FFN_COLS = 256
MIX_ROWS = 2048
MACRO = 128
PIECE = 256
GROUP_HEADS = 4
XA_ROWS = 2048
XA_SUB = 512
KV_ROWS = 256

BF16 = jnp.bfloat16
F32 = jnp.float32
BF16_ROWS = 16


def _vmem_limit(estimate_bytes):
    return int(min(estimate_bytes * 5 // 4 + 8 * MIB, V7X_VMEM_BYTES - 6 * MIB))


def _resident(shape):
    nd = len(shape)
    return pl.BlockSpec(shape, lambda *_: (0,) * nd, pipeline_mode=pl.Buffered(1))


def _rmsnorm(x, g):
    ms = jnp.mean(x * x, axis=-1, keepdims=True)
    return x * lax.rsqrt(ms + EPS) * g


def _silu(x):
    return x * (1.0 / (1.0 + jnp.exp(-x)))


def _dot(a, b):
    return jnp.dot(a, b, preferred_element_type=F32)


def _dot_nt(a, b):
    return lax.dot_general(a, b, (((1,), (1,)), ((), ())), preferred_element_type=F32)


def _dot_tn(a, b):
    return lax.dot_general(a, b, (((0,), (0,)), ((), ())), preferred_element_type=F32)


def _ffn_kernel(x_ref, g_ref, wgu_ref, wd_ref, fin_ref, *refs, final_norm, n_cast):
    cast_in, o_ref, cast_out = refs[:n_cast], refs[n_cast], refs[n_cast + 1:2 * n_cast + 1]
    h_ref, act_ref = refs[2 * n_cast + 1:]
    n_sub = FFN_ROWS // FFN_SUB
    gain = g_ref[...]

    def rows(i):
        return slice(i * FFN_SUB, (i + 1) * FFN_SUB)

    def norm(i):
        h_ref[rows(i), :] = _rmsnorm(x_ref[rows(i), :], gain).astype(BF16)

    def gate_up(i):
        h = h_ref[rows(i), :]
        for c in range(D_FF // FFN_COLS):
            lo = c * FFN_COLS
            gate = _dot(h, wgu_ref[:, lo:lo + FFN_COLS])
            up = _dot(h, wgu_ref[:, D_FF + lo:D_FF + lo + FFN_COLS])
            act_ref[rows(i), lo:lo + FFN_COLS] = (_silu(gate) * up).astype(BF16)

    def down(i):
        y = x_ref[rows(i), :] + 0.5 * _dot(act_ref[rows(i), :], wd_ref[...])
        if final_norm:
            y = _rmsnorm(y, fin_ref[...])
        o_ref[rows(i), :] = y

    norm(0)
    for i in range(n_sub):
        if i + 1 < n_sub:
            norm(i + 1)
        gate_up(i)
        if i >= 1:
            down(i - 1)
    down(n_sub - 1)

    for src, dst in zip(cast_in, cast_out):
        dst[...] = src[...].astype(BF16)


def _ffn(x2d, norm, w_gu, w_down, fin, *, final_norm, cast=()):
    n = x2d.shape[0]
    tm = FFN_ROWS
    steps = n // tm
    slices = [w.shape[0] // steps for w in cast]
    assert all(s % BF16_ROWS == 0 and s * steps == w.shape[0] for s, w in zip(slices, cast))
    est = (4 * tm * D_MODEL * 4 + w_gu.size * 2 + w_down.size * 2 + tm * D_FF * 2
           + tm * D_MODEL * 2 + 4 * FFN_SUB * FFN_COLS * 4
           + 2 * sum(s * w.shape[1] for s, w in zip(slices, cast)) * (4 + 2))
    cast_specs = [pl.BlockSpec((s, w.shape[1]), lambda i: (i, 0)) for s, w in zip(slices, cast)]
    out = pl.pallas_call(
        functools.partial(_ffn_kernel, final_norm=final_norm, n_cast=len(cast)),
        grid=(steps,),
        in_specs=[
            pl.BlockSpec((tm, D_MODEL), lambda i: (i, 0)),
            _resident((1, D_MODEL)),
            _resident((D_MODEL, 2 * D_FF)),
            _resident((D_FF, D_MODEL)),
            _resident((1, D_MODEL)),
        ] + cast_specs,
        out_specs=[pl.BlockSpec((tm, D_MODEL), lambda i: (i, 0))] + cast_specs,
        out_shape=[jax.ShapeDtypeStruct((n, D_MODEL), F32)]
        + [jax.ShapeDtypeStruct(w.shape, BF16) for w in cast],
        scratch_shapes=[pltpu.VMEM((tm, D_MODEL), BF16), pltpu.VMEM((tm, D_FF), BF16)],
        compiler_params=pltpu.CompilerParams(
            dimension_semantics=("arbitrary",), vmem_limit_bytes=_vmem_limit(est)),
        name="ffn_final" if final_norm else "ffn",
    )(x2d, norm.reshape(1, D_MODEL), w_gu, w_down, fin.reshape(1, D_MODEL), *cast)
    return out[0], list(out[1:])


def _chunk_cumsum(g):
    n = g.shape[-1]
    nb = CHUNK // SUB
    row = lax.broadcasted_iota(jnp.int32, (nb, SUB, n), 1)
    s = g.reshape(nb, SUB, n)
    for shift in (1, 2, 4):
        s = s + jnp.where(row >= shift, pltpu.roll(s, shift, 1), 0.0)
    carry = [jnp.zeros((1, 1, n), F32)]
    for i in range(1, nb):
        carry.append(carry[-1] + s[i - 1:i, SUB - 1:SUB, :])
    return (s + jnp.concatenate(carry, axis=0)).reshape(CHUNK, n)


def _hgrn2_masks():
    t = lax.broadcasted_iota(jnp.int32, (CHUNK, CHUNK), 0)
    s = lax.broadcasted_iota(jnp.int32, (CHUNK, CHUNK), 1)
    levels = []
    blk = CHUNK
    while blk > DIAG:
        half = blk // 2
        levels.append((t // blk == s // blk) & (t % blk >= half) & (s % blk < half))
        blk = half
    diag = [(s == t - d) & (t % DIAG >= d) for d in range(DIAG)]
    return levels, diag


def _mix_kernel(x_ref, xn_ref, nrm_ref, win_ref, lbl_ref, hgn_ref, cw_ref, cn_ref, wout_ref, o_ref,
                p_a, p_b, mix_a, mix_b, h_a, h_b, st_ref, ybuf_ref):
    n_macro = MIX_ROWS // MACRO

    @pl.when(pl.program_id(1) == 0)
    def _():
        st_ref[...] = jnp.zeros_like(st_ref)
        ybuf_ref[0:SUB, :] = jnp.zeros((SUB, CONV_WIDTH), F32)

    mix_b[...] = jnp.zeros((MACRO, D_MODEL), BF16)

    lbl = lbl_ref[...]
    lmax = jnp.max(lbl, axis=0, keepdims=True)
    lexp = jnp.exp(lbl - lmax)
    lb_all = lexp[0:1, :] / jnp.sum(lexp, axis=0, keepdims=True)

    level_masks, diag_masks = _hgrn2_masks()
    hgn = hgn_ref[...]
    nrm = nrm_ref[...]
    cw = cw_ref[...]
    cn = cn_ref[...]
    cb = 4 * HG_WIDTH

    def macro_rows(m):
        return pl.ds(pl.multiple_of(m * MACRO, MACRO), MACRO)

    def in_norm(m, h_ref):
        h_ref[...] = _rmsnorm(x_ref[0, macro_rows(m), :], nrm).astype(BF16)

    def in_norm_or_next(m, h_ref):
        rows = x_ref[0, macro_rows(jnp.minimum(m, n_macro - 1)), :]
        rows = jnp.where(m >= n_macro, xn_ref[0], rows)
        h_ref[...] = _rmsnorm(rows, nrm).astype(BF16)

    def in_proj_piece(h_ref, p_ref, c):
        cols = slice(c * PIECE, (c + 1) * PIECE)
        p_ref[:, cols] = _dot(h_ref[...], win_ref[:, cols])

    def out_proj_piece(m, mix_ref, c):
        rows = macro_rows(m)
        cols = slice(c * PIECE, (c + 1) * PIECE)
        o_ref[0, rows, cols] = x_ref[0, rows, cols] + _dot(mix_ref[...], wout_ref[:, cols])


    def hg_ab(p_ref, rows, hh):
        c0 = hh * HG_DK
        q = p_ref[rows, pl.ds(c0, HG_DK)]
        fz = p_ref[rows, pl.ds(HG_WIDTH + c0, HG_DK)]
        v = p_ref[rows, pl.ds(2 * HG_WIDTH + c0, HG_DK)]
        lb = lb_all[:, c0:c0 + HG_DK]
        f = lb + (1.0 - lb) * (1.0 / (1.0 + jnp.exp(-fz)))
        k = 1.0 - f
        qs = _silu(q)
        b = _chunk_cumsum(jnp.log2(f))
        b_end = b[CHUNK - 1:CHUNK, :]
        u = dict(hh=hh, rows=rows, p_ref=p_ref, v_bf=v.astype(BF16))
        q_bf, k_bf = qs.astype(BF16), k.astype(BF16)
        ops = []
        blk = CHUNK
        for _ in level_masks:
            half = blk // 2
            nb = CHUNK // blk
            ref = b.reshape(nb, blk, HG_DK)[:, half - 1:half, :]
            ref = jnp.broadcast_to(ref, (nb, blk, HG_DK)).reshape(CHUNK, HG_DK)
            z = jnp.exp2(-jnp.abs(b - ref)).astype(BF16)
            ops.append((q_bf * z, k_bf * z))
            blk = half
        u["level_ops"] = ops
        attn = jnp.where(diag_masks[0], jnp.sum(qs * k, axis=-1, keepdims=True), 0.0)
        f3 = f.reshape(CHUNK // SUB, SUB, HG_DK)
        decay = f
        for d in range(1, DIAG):
            f_d = pltpu.roll(f3, d, 1).reshape(CHUNK, HG_DK)
            r_d = jnp.sum(qs * (1.0 - f_d) * decay, axis=-1, keepdims=True)
            attn = jnp.where(diag_masks[d], r_d, attn)
            decay = decay * f_d
        u["attn_diag"] = attn
        u["qe"] = (qs * jnp.exp2(b)).astype(BF16)
        u["ke"] = (k * jnp.exp2(b_end - b)).astype(BF16)
        u["decay"] = jnp.exp2(b_end)
        return u

    def hg_mb(u):
        u["level_out"] = [_dot_nt(ql, kl) for ql, kl in u.pop("level_ops")]

    def hg_c(u):
        attn = u.pop("attn_diag")
        for lm, a in zip(level_masks, u.pop("level_out")):
            attn = jnp.where(lm, a, attn)
        u["attn"] = attn.astype(BF16)

    def hg_mc(u):
        hh, v_bf = u["hh"], u.pop("v_bf")
        st = st_ref[hh]
        o = _dot(u.pop("attn"), v_bf)
        u["o"] = o + _dot(u.pop("qe"), st.astype(BF16))
        decay_col = jnp.transpose(jnp.broadcast_to(u.pop("decay"), (HG_DK, HG_DK)))
        st_ref[hh] = st * decay_col + _dot_tn(u.pop("ke"), v_bf)

    def hg_d(u, mix_ref):
        c0 = u["hh"] * HG_DK
        gt = u["p_ref"][u["rows"], pl.ds(3 * HG_WIDTH + c0, HG_DK)]
        on = _rmsnorm(u.pop("o"), hgn) * _silu(gt)
        mix_ref[u["rows"], pl.ds(c0, HG_DK)] = on.astype(BF16)

    def short_conv(p_ref, mix_ref, rows):
        y = p_ref[rows, cb + CONV_WIDTH:cb + 2 * CONV_WIDTH] * \
            p_ref[rows, cb + 2 * CONV_WIDTH:cb + 3 * CONV_WIDTH]
        ybuf_ref[SUB:SUB + CHUNK, :] = y
        y1 = ybuf_ref[SUB - 1:SUB - 1 + CHUNK, :]
        y2 = ybuf_ref[SUB - 2:SUB - 2 + CHUNK, :]
        conv = cw[0:1, :] * y2 + cw[1:2, :] * y1 + cw[2:3, :] * y
        ocv = _rmsnorm(p_ref[rows, cb:cb + CONV_WIDTH] * conv, cn)
        mix_ref[rows, HG_WIDTH:HG_WIDTH + CONV_WIDTH] = ocv.astype(BF16)
        ybuf_ref[0:SUB, :] = ybuf_ref[CHUNK:CHUNK + SUB, :]

    def each(phase, units, *args):
        for u in units:
            phase(u, *args)

    n_chunks = MACRO // CHUNK
    chunk_rows = [pl.ds(j * CHUNK, CHUNK) for j in range(n_chunks)]
    head_groups = [tuple(range(h, h + GROUP_HEADS)) for h in range(0, HG_HEADS, GROUP_HEADS)]
    last = n_macro - 1

    @pl.when((pl.program_id(0) == 0) & (pl.program_id(1) == 0))
    def _():
        in_norm(0, h_b)
        for c in range(IN_COLS // PIECE):
            in_proj_piece(h_b, p_a, c)

    in_norm(1, h_a)

    def trip(j, carry):
        m = 2 * j
        big_a = [functools.partial(in_proj_piece, h_a, p_b, c) for c in range(IN_COLS // PIECE)]
        big_a += [functools.partial(out_proj_piece, jnp.maximum(m - 1, 0), mix_b, c)
                  for c in range(D_MODEL // PIECE)]
        big_b = [functools.partial(in_proj_piece, h_b, p_a, c) for c in range(IN_COLS // PIECE)]
        big_b += [functools.partial(out_proj_piece, m, mix_a, c) for c in range(D_MODEL // PIECE)]
        groups = [(p_ref, mix_ref, r, hs, hs[0] == 0)
                  for p_ref, mix_ref in ((p_a, mix_a), (p_b, mix_b)) for r in chunk_rows
                  for hs in head_groups]
        ng = len(groups)
        na = ng // 2
        plan = {i: [] for i in range(-1, ng + 1)}
        for pieces, slots in ((big_a, range(-1, na + 1)), (big_b, range(na + 1, ng + 1))):
            for k, piece in enumerate(pieces):
                plan[slots[k * len(slots) // len(pieces)]].append(piece)

        def issue(slot):
            for piece in plan[slot]:
                piece()

        issue(-1)
        units = []
        for i in range(ng + 1):
            if i < ng:
                p_ref, _, rows, hs, _ = groups[i]
                units.append([hg_ab(p_ref, rows, hh) for hh in hs])
                each(hg_mb, units[i])
            if i >= 1:
                p_ref, mix_ref, rows, _, with_conv = groups[i - 1]
                if with_conv:
                    short_conv(p_ref, mix_ref, rows)
                each(hg_c, units[i - 1])
                each(hg_mc, units[i - 1])
            if i >= 2:
                each(hg_d, units[i - 2], groups[i - 2][1])
            if i == na - 1:
                in_norm_or_next(m + 2, h_b)
            if i == ng - 1:
                in_norm(jnp.minimum(m + 3, last), h_a)
            issue(i)
        each(hg_d, units[ng - 1], groups[ng - 1][1])
        return carry

    lax.fori_loop(0, n_macro // 2, trip, 0, unroll=2)
    for c in range(D_MODEL // PIECE):
        out_proj_piece(last, mix_b, c)


def _mix(x, norm, w_in, lb_logits, hg_norm, conv_w, conv_norm, w_out):
    bsz, t, _ = x.shape
    tt = MIX_ROWS
    tiles = t // tt

    def next_group(b, i):
        step = jnp.minimum(b * tiles + i + 1, bsz * tiles - 1)
        return step // tiles, (step % tiles) * (tt // MACRO), 0

    est = (4 * tt * D_MODEL * 4 + 2 * MACRO * D_MODEL * 4 + w_in.size * 2 + w_out.size * 2 + 2 * MACRO * IN_COLS * 4
           + HG_HEADS * HG_DK * HG_DK * 4 + (CHUNK + SUB) * CONV_WIDTH * 4
           + 4 * MACRO * D_MODEL * 2)
    return pl.pallas_call(
        _mix_kernel,
        grid=(bsz, t // tt),
        in_specs=[
            pl.BlockSpec((1, tt, D_MODEL), lambda b, i: (b, i, 0)),
            pl.BlockSpec((1, MACRO, D_MODEL), next_group),
            _resident((1, D_MODEL)),
            _resident((D_MODEL, IN_COLS)),
            _resident(lb_logits.shape),
            _resident((1, HG_DK)),
            _resident((CONV_K, CONV_WIDTH)),
            _resident((1, CONV_WIDTH)),
            _resident((D_MODEL, D_MODEL)),
        ],
        out_specs=pl.BlockSpec((1, tt, D_MODEL), lambda b, i: (b, i, 0)),
        out_shape=jax.ShapeDtypeStruct(x.shape, F32),
        scratch_shapes=[
            pltpu.VMEM((MACRO, IN_COLS), F32),
            pltpu.VMEM((MACRO, IN_COLS), F32),
            pltpu.VMEM((MACRO, D_MODEL), BF16),
            pltpu.VMEM((MACRO, D_MODEL), BF16),
            pltpu.VMEM((MACRO, D_MODEL), BF16),
            pltpu.VMEM((MACRO, D_MODEL), BF16),
            pltpu.VMEM((HG_HEADS, HG_DK, HG_DK), F32),
            pltpu.VMEM((CHUNK + SUB, CONV_WIDTH), F32),
        ],
        compiler_params=pltpu.CompilerParams(
            dimension_semantics=("arbitrary", "arbitrary"), vmem_limit_bytes=_vmem_limit(est)),
        name="mix",
    )(x, x, norm.reshape(1, D_MODEL), w_in, lb_logits, hg_norm.reshape(1, HG_DK), conv_w,
      conv_norm.reshape(1, CONV_WIDTH), w_out)


def _kv_kernel(m_ref, g_ref, w_ref, o_ref):
    h = _rmsnorm(m_ref[...], g_ref[...]).astype(BF16)
    o_ref[...] = _dot(h, w_ref[...]).astype(BF16)


def _kv(mem2d, norm, w_kv):
    n = mem2d.shape[0]
    est = 2 * KV_ROWS * D_MODEL * 4 + w_kv.size * 2 + 2 * KV_ROWS * 2 * D_MODEL * 2
    return pl.pallas_call(
        _kv_kernel,
        grid=(n // KV_ROWS,),
        in_specs=[
            pl.BlockSpec((KV_ROWS, D_MODEL), lambda i: (i, 0)),
            _resident((1, D_MODEL)),
            _resident((D_MODEL, 2 * D_MODEL)),
        ],
        out_specs=pl.BlockSpec((KV_ROWS, 2 * D_MODEL), lambda i: (i, 0)),
        out_shape=jax.ShapeDtypeStruct((n, 2 * D_MODEL), BF16),
        compiler_params=pltpu.CompilerParams(
            dimension_semantics=("arbitrary",), vmem_limit_bytes=_vmem_limit(est)),
        name="kv",
    )(mem2d, norm.reshape(1, D_MODEL), w_kv)


def _xattn_kernel(x_ref, g_ref, wq_ref, kv_ref, wo_ref, o_ref, h_ref, q_ref, s_ref, p_ref,
                  att_ref):
    n_sub = XA_ROWS // XA_SUB
    scale = 1.0 / math.sqrt(XA_HEAD_DIM)
    gain = g_ref[...]
    heads = [slice(hh * XA_HEAD_DIM, (hh + 1) * XA_HEAD_DIM) for hh in range(XA_HEADS)]

    def rows(i):
        return slice(i * XA_SUB, (i + 1) * XA_SUB)

    def norm(i):
        h_ref[i % 2] = _rmsnorm(x_ref[0, rows(i), :], gain).astype(BF16)

    def q_proj(i):
        for hd in heads:
            q_ref[i % 2, :, hd] = (_dot(h_ref[i % 2], wq_ref[:, hd]) * scale).astype(BF16)

    def scores(i):
        for hd in heads:
            s_ref[i % 2, :, hd] = _dot_nt(q_ref[i % 2, :, hd], kv_ref[0, :, hd])

    def softmax(i):
        for hd in heads:
            s = s_ref[i % 2, :, hd]
            e = jnp.exp(s - jnp.max(s, axis=-1, keepdims=True))
            p_ref[i % 2, :, hd] = (e * (1.0 / jnp.sum(e, axis=-1, keepdims=True))).astype(BF16)

    def pv(i):
        for hh, hd in enumerate(heads):
            v = kv_ref[0, :, D_MODEL + hh * XA_HEAD_DIM:D_MODEL + (hh + 1) * XA_HEAD_DIM]
            att_ref[i % 2, :, hd] = _dot(p_ref[i % 2, :, hd], v).astype(BF16)

    def out_proj(i):
        for hd in heads:
            o_ref[0, rows(i), hd] = x_ref[0, rows(i), hd] + _dot(att_ref[i % 2], wo_ref[:, hd])

    norm(0)
    if n_sub > 1:
        norm(1)
    q_proj(0)
    scores(0)
    for i in range(n_sub):
        if i + 1 < n_sub:
            q_proj(i + 1)
        softmax(i)
        pv(i)
        if i + 1 < n_sub:
            scores(i + 1)
        if i + 2 < n_sub:
            norm(i + 2)
        out_proj(i)


def _xattn(x, norm, w_q, kv, w_o):
    bsz, t, _ = x.shape
    tm = XA_ROWS
    est = (4 * tm * D_MODEL * 4 + w_q.size * 2 + w_o.size * 2 + 2 * N_MEM * 2 * D_MODEL * 2
           + 2 * XA_SUB * D_MODEL * (4 + 4 * 2))
    return pl.pallas_call(
        _xattn_kernel,
        grid=(bsz, t // tm),
        in_specs=[
            pl.BlockSpec((1, tm, D_MODEL), lambda b, i: (b, i, 0)),
            _resident((1, D_MODEL)),
            _resident((D_MODEL, D_MODEL)),
            pl.BlockSpec((1, N_MEM, 2 * D_MODEL), lambda b, i: (b, 0, 0)),
            _resident((D_MODEL, D_MODEL)),
        ],
        out_specs=pl.BlockSpec((1, tm, D_MODEL), lambda b, i: (b, i, 0)),
        out_shape=jax.ShapeDtypeStruct(x.shape, F32),
        scratch_shapes=[
            pltpu.VMEM((2, XA_SUB, D_MODEL), BF16),
            pltpu.VMEM((2, XA_SUB, D_MODEL), BF16),
            pltpu.VMEM((2, XA_SUB, D_MODEL), F32),
            pltpu.VMEM((2, XA_SUB, D_MODEL), BF16),
            pltpu.VMEM((2, XA_SUB, D_MODEL), BF16),
        ],
        compiler_params=pltpu.CompilerParams(
            dimension_semantics=("arbitrary", "arbitrary"), vmem_limit_bytes=_vmem_limit(est)),
        name="xattn",
    )(x, norm.reshape(1, D_MODEL), w_q, kv, w_o)


def kernel(x, mem, ffn1_norm, ffn1_w_gu, ffn1_w_down, mix_norm, w_in, lb_logits, hg_norm, conv_w,
           conv_norm, w_out, xa_norm, mem_norm, xa_wq, xa_wkv, xa_wo, ffn2_norm, ffn2_w_gu,
           ffn2_w_down, final_norm):
    bsz, t, d = x.shape
    assert d == D_MODEL and t % MIX_ROWS == 0 and (bsz * t) % FFN_ROWS == 0
    assert ffn1_norm.shape[0] == 1, "single-layer block"
    later = [w_in[0], w_out[0], xa_wq[0], xa_wkv[0], xa_wo[0], ffn2_w_gu[0], ffn2_w_down[0]]
    x1, later_bf = _ffn(x.reshape(bsz * t, d), ffn1_norm[0], ffn1_w_gu[0].astype(BF16),
                        ffn1_w_down[0].astype(BF16), final_norm, final_norm=False, cast=later)
    w_in_bf, w_out_bf, wq_bf, wkv_bf, wo_bf, w_gu2_bf, w_down2_bf = later_bf
    x2 = _mix(x1.reshape(bsz, t, d), mix_norm[0], w_in_bf, lb_logits, hg_norm[0], conv_w[0],
              conv_norm[0], w_out_bf)
    kv = _kv(mem.reshape(bsz * N_MEM, d), mem_norm[0], wkv_bf).reshape(bsz, N_MEM, 2 * d)
    x3 = _xattn(x2, xa_norm[0], wq_bf, kv, wo_bf)
    y, _ = _ffn(x3.reshape(bsz * t, d), ffn2_norm[0], w_gu2_bf, w_down2_bf, final_norm,
                final_norm=True)
    return y.reshape(bsz, t, d)
```

```python
import functools
import math

import jax
import jax.numpy as jnp
from jax import lax
from jax.experimental import pallas as pl
from jax.experimental.pallas import tpu as pltpu

D_MODEL = 1024
CHUNK = 64
SUB = 8
DIAG = 4
HG_WIDTH = 512
CONV_WIDTH = 512
HG_HEADS = 4
HG_DK = 128
CONV_K = 3
N_MEM = 256
XA_HEADS = 4
XA_HEAD_DIM = 256
D_FF = 2816
IN_COLS = 4 * HG_WIDTH + 3 * CONV_WIDTH
EPS = 1e-6

V7X_VMEM_BYTES = 64 * 1024 * 1024
MIB = 1024 * 1024

FFN_ROWS = 1024
FFN_SUB = 1024
FFN_COLS = 256
MIX_ROWS = 1024
MACRO = 256
PIECE = 256
GROUP_HEADS = 4
XA_ROWS = 2048
XA_SUB = 512
KV_ROWS = 256

BF16 = jnp.bfloat16
F32 = jnp.float32
BF16_ROWS = 16


def _vmem_limit(estimate_bytes):
    return int(min(estimate_bytes * 5 // 4 + 8 * MIB, V7X_VMEM_BYTES - 6 * MIB))


def _resident(shape):
    nd = len(shape)
    return pl.BlockSpec(shape, lambda *_: (0,) * nd, pipeline_mode=pl.Buffered(1))


def _rmsnorm(x, g):
    ms = jnp.mean(x * x, axis=-1, keepdims=True)
    return x * lax.rsqrt(ms + EPS) * g


def _silu(x):
    return x * (1.0 / (1.0 + jnp.exp(-x)))


def _dot(a, b):
    return jnp.dot(a, b, preferred_element_type=F32)


def _dot_nt(a, b):
    return lax.dot_general(a, b, (((1,), (1,)), ((), ())), preferred_element_type=F32)


def _dot_tn(a, b):
    return lax.dot_general(a, b, (((0,), (0,)), ((), ())), preferred_element_type=F32)


def _ffn_kernel(x_ref, g_ref, wgu_ref, wd_ref, fin_ref, *refs, final_norm, n_cast):
    cast_in, o_ref, cast_out = refs[:n_cast], refs[n_cast], refs[n_cast + 1:2 * n_cast + 1]
    h_ref, act_ref = refs[2 * n_cast + 1:]
    n_sub = FFN_ROWS // FFN_SUB
    gain = g_ref[...]

    def rows(i):
        return slice(i * FFN_SUB, (i + 1) * FFN_SUB)

    def norm(i):
        h_ref[rows(i), :] = _rmsnorm(x_ref[rows(i), :], gain).astype(BF16)

    def gate_up(i):
        h = h_ref[rows(i), :]
        for c in range(D_FF // FFN_COLS):
            lo = c * FFN_COLS
            gate = _dot(h, wgu_ref[:, lo:lo + FFN_COLS])
            up = _dot(h, wgu_ref[:, D_FF + lo:D_FF + lo + FFN_COLS])
            act_ref[rows(i), lo:lo + FFN_COLS] = (_silu(gate) * up).astype(BF16)

    def down(i):
        y = x_ref[rows(i), :] + 0.5 * _dot(act_ref[rows(i), :], wd_ref[...])
        if final_norm:
            y = _rmsnorm(y, fin_ref[...])
        o_ref[rows(i), :] = y

    norm(0)
    for i in range(n_sub):
        if i + 1 < n_sub:
            norm(i + 1)
        gate_up(i)
        if i >= 1:
            down(i - 1)
    down(n_sub - 1)

    for src, dst in zip(cast_in, cast_out):
        dst[...] = src[...].astype(BF16)


def _ffn(x2d, norm, w_gu, w_down, fin, *, final_norm, cast=()):
    n = x2d.shape[0]
    tm = FFN_ROWS
    steps = n // tm
    slices = [w.shape[0] // steps for w in cast]
    assert all(s % BF16_ROWS == 0 and s * steps == w.shape[0] for s, w in zip(slices, cast))
    est = (4 * tm * D_MODEL * 4 + w_gu.size * 2 + w_down.size * 2 + tm * D_FF * 2
           + tm * D_MODEL * 2 + 4 * FFN_SUB * FFN_COLS * 4
           + 2 * sum(s * w.shape[1] for s, w in zip(slices, cast)) * (4 + 2))
    cast_specs = [pl.BlockSpec((s, w.shape[1]), lambda i: (i, 0)) for s, w in zip(slices, cast)]
    out = pl.pallas_call(
        functools.partial(_ffn_kernel, final_norm=final_norm, n_cast=len(cast)),
        grid=(steps,),
        in_specs=[
            pl.BlockSpec((tm, D_MODEL), lambda i: (i, 0)),
            _resident((1, D_MODEL)),
            _resident((D_MODEL, 2 * D_FF)),
            _resident((D_FF, D_MODEL)),
            _resident((1, D_MODEL)),
        ] + cast_specs,
        out_specs=[pl.BlockSpec((tm, D_MODEL), lambda i: (i, 0))] + cast_specs,
        out_shape=[jax.ShapeDtypeStruct((n, D_MODEL), F32)]
        + [jax.ShapeDtypeStruct(w.shape, BF16) for w in cast],
        scratch_shapes=[pltpu.VMEM((tm, D_MODEL), BF16), pltpu.VMEM((tm, D_FF), BF16)],
        compiler_params=pltpu.CompilerParams(
            dimension_semantics=("arbitrary",), vmem_limit_bytes=_vmem_limit(est)),
        name="ffn_final" if final_norm else "ffn",
    )(x2d, norm.reshape(1, D_MODEL), w_gu, w_down, fin.reshape(1, D_MODEL), *cast)
    return out[0], list(out[1:])


def _chunk_cumsum(g):
    n = g.shape[-1]
    nb = CHUNK // SUB
    row = lax.broadcasted_iota(jnp.int32, (nb, SUB, n), 1)
    s = g.reshape(nb, SUB, n)
    for shift in (1, 2, 4):
        s = s + jnp.where(row >= shift, pltpu.roll(s, shift, 1), 0.0)
    carry = [jnp.zeros((1, 1, n), F32)]
    for i in range(1, nb):
        carry.append(carry[-1] + s[i - 1:i, SUB - 1:SUB, :])
    return (s + jnp.concatenate(carry, axis=0)).reshape(CHUNK, n)


def _hgrn2_masks():
    t = lax.broadcasted_iota(jnp.int32, (CHUNK, CHUNK), 0)
    s = lax.broadcasted_iota(jnp.int32, (CHUNK, CHUNK), 1)
    levels = []
    blk = CHUNK
    while blk > DIAG:
        half = blk // 2
        levels.append((t // blk == s // blk) & (t % blk >= half) & (s % blk < half))
        blk = half
    diag = [(s == t - d) & (t % DIAG >= d) for d in range(DIAG)]
    return levels, diag


def _mix_kernel(x_ref, xn_ref, nrm_ref, win_ref, lbl_ref, hgn_ref, cw_ref, cn_ref, wout_ref, o_ref,
                p_a, p_b, mix_a, mix_b, h_a, h_b, st_ref, ybuf_ref):
    n_macro = MIX_ROWS // MACRO

    @pl.when(pl.program_id(1) == 0)
    def _():
        st_ref[...] = jnp.zeros_like(st_ref)
        ybuf_ref[0:SUB, :] = jnp.zeros((SUB, CONV_WIDTH), F32)

    mix_b[...] = jnp.zeros((MACRO, D_MODEL), BF16)

    lbl = lbl_ref[...]
    lmax = jnp.max(lbl, axis=0, keepdims=True)
    lexp = jnp.exp(lbl - lmax)
    lb_all = lexp[0:1, :] / jnp.sum(lexp, axis=0, keepdims=True)

    level_masks, diag_masks = _hgrn2_masks()
    hgn = hgn_ref[...]
    nrm = nrm_ref[...]
    cw = cw_ref[...]
    cn = cn_ref[...]
    cb = 4 * HG_WIDTH

    def macro_rows(m):
        return pl.ds(pl.multiple_of(m * MACRO, MACRO), MACRO)

    def in_norm(m, h_ref):
        h_ref[...] = _rmsnorm(x_ref[0, macro_rows(m), :], nrm).astype(BF16)

    def in_norm_or_next(m, h_ref):
        rows = x_ref[0, macro_rows(jnp.minimum(m, n_macro - 1)), :]
        rows = jnp.where(m >= n_macro, xn_ref[0], rows)
        h_ref[...] = _rmsnorm(rows, nrm).astype(BF16)

    def in_proj_piece(h_ref, p_ref, c):
        cols = slice(c * PIECE, (c + 1) * PIECE)
        p_ref[:, cols] = _dot(h_ref[...], win_ref[:, cols])

    def out_proj_piece(m, mix_ref, c):
        rows = macro_rows(m)
        cols = slice(c * PIECE, (c + 1) * PIECE)
        o_ref[0, rows, cols] = x_ref[0, rows, cols] + _dot(mix_ref[...], wout_ref[:, cols])


    def hg_ab(p_ref, rows, hh):
        c0 = hh * HG_DK
        q = p_ref[rows, pl.ds(c0, HG_DK)]
        fz = p_ref[rows, pl.ds(HG_WIDTH + c0, HG_DK)]
        v = p_ref[rows, pl.ds(2 * HG_WIDTH + c0, HG_DK)]
        lb = lb_all[:, c0:c0 + HG_DK]
        f = lb + (1.0 - lb) * (1.0 / (1.0 + jnp.exp(-fz)))
        k = 1.0 - f
        qs = _silu(q)
        b = _chunk_cumsum(jnp.log2(f))
        b_end = b[CHUNK - 1:CHUNK, :]
        u = dict(hh=hh, rows=rows, p_ref=p_ref, v_bf=v.astype(BF16))
        q_bf, k_bf = qs.astype(BF16), k.astype(BF16)
        ops = []
        blk = CHUNK
        for _ in level_masks:
            half = blk // 2
            nb = CHUNK // blk
            ref = b.reshape(nb, blk, HG_DK)[:, half - 1:half, :]
            ref = jnp.broadcast_to(ref, (nb, blk, HG_DK)).reshape(CHUNK, HG_DK)
            z = jnp.exp2(-jnp.abs(b - ref)).astype(BF16)
            ops.append((q_bf * z, k_bf * z))
            blk = half
        u["level_ops"] = ops
        attn = jnp.where(diag_masks[0], jnp.sum(qs * k, axis=-1, keepdims=True), 0.0)
        f3 = f.reshape(CHUNK // SUB, SUB, HG_DK)
        decay = f
        for d in range(1, DIAG):
            f_d = pltpu.roll(f3, d, 1).reshape(CHUNK, HG_DK)
            r_d = jnp.sum(qs * (1.0 - f_d) * decay, axis=-1, keepdims=True)
            attn = jnp.where(diag_masks[d], r_d, attn)
            decay = decay * f_d
        u["attn_diag"] = attn
        u["qe"] = (qs * jnp.exp2(b)).astype(BF16)
        u["ke"] = (k * jnp.exp2(b_end - b)).astype(BF16)
        u["decay"] = jnp.exp2(b_end)
        return u

    def hg_mb(u):
        u["level_out"] = [_dot_nt(ql, kl) for ql, kl in u.pop("level_ops")]

    def hg_c(u):
        attn = u.pop("attn_diag")
        for lm, a in zip(level_masks, u.pop("level_out")):
            attn = jnp.where(lm, a, attn)
        u["attn"] = attn.astype(BF16)

    def hg_mc(u):
        hh, v_bf = u["hh"], u.pop("v_bf")
        st = st_ref[hh]
        o = _dot(u.pop("attn"), v_bf)
        u["o"] = o + _dot(u.pop("qe"), st.astype(BF16))
        decay_col = jnp.transpose(jnp.broadcast_to(u.pop("decay"), (HG_DK, HG_DK)))
        st_ref[hh] = st * decay_col + _dot_tn(u.pop("ke"), v_bf)

    def hg_d(u, mix_ref):
        c0 = u["hh"] * HG_DK
        gt = u["p_ref"][u["rows"], pl.ds(3 * HG_WIDTH + c0, HG_DK)]
        on = _rmsnorm(u.pop("o"), hgn) * _silu(gt)
        mix_ref[u["rows"], pl.ds(c0, HG_DK)] = on.astype(BF16)

    def short_conv(p_ref, mix_ref, rows):
        y = p_ref[rows, cb + CONV_WIDTH:cb + 2 * CONV_WIDTH] * \
            p_ref[rows, cb + 2 * CONV_WIDTH:cb + 3 * CONV_WIDTH]
        ybuf_ref[SUB:SUB + CHUNK, :] = y
        y1 = ybuf_ref[SUB - 1:SUB - 1 + CHUNK, :]
        y2 = ybuf_ref[SUB - 2:SUB - 2 + CHUNK, :]
        conv = cw[0:1, :] * y2 + cw[1:2, :] * y1 + cw[2:3, :] * y
        ocv = _rmsnorm(p_ref[rows, cb:cb + CONV_WIDTH] * conv, cn)
        mix_ref[rows, HG_WIDTH:HG_WIDTH + CONV_WIDTH] = ocv.astype(BF16)
        ybuf_ref[0:SUB, :] = ybuf_ref[CHUNK:CHUNK + SUB, :]

    def each(phase, units, *args):
        for u in units:
            phase(u, *args)

    n_chunks = MACRO // CHUNK
    chunk_rows = [pl.ds(j * CHUNK, CHUNK) for j in range(n_chunks)]
    head_groups = [tuple(range(h, h + GROUP_HEADS)) for h in range(0, HG_HEADS, GROUP_HEADS)]
    last = n_macro - 1

    @pl.when((pl.program_id(0) == 0) & (pl.program_id(1) == 0))
    def _():
        in_norm(0, h_b)
        for c in range(IN_COLS // PIECE):
            in_proj_piece(h_b, p_a, c)

    in_norm(1, h_a)

    def trip(j, carry):
        m = 2 * j
        big_a = [functools.partial(in_proj_piece, h_a, p_b, c) for c in range(IN_COLS // PIECE)]
        big_a += [functools.partial(out_proj_piece, jnp.maximum(m - 1, 0), mix_b, c)
                  for c in range(D_MODEL // PIECE)]
        big_b = [functools.partial(in_proj_piece, h_b, p_a, c) for c in range(IN_COLS // PIECE)]
        big_b += [functools.partial(out_proj_piece, m, mix_a, c) for c in range(D_MODEL // PIECE)]
        groups = [(p_ref, mix_ref, r, hs, hs[0] == 0)
                  for p_ref, mix_ref in ((p_a, mix_a), (p_b, mix_b)) for r in chunk_rows
                  for hs in head_groups]
        ng = len(groups)
        na = ng // 2
        plan = {i: [] for i in range(-1, ng + 1)}
        for pieces, slots in ((big_a, range(-1, na + 1)), (big_b, range(na + 1, ng + 1))):
            for k, piece in enumerate(pieces):
                plan[slots[k * len(slots) // len(pieces)]].append(piece)

        def issue(slot):
            for piece in plan[slot]:
                piece()

        issue(-1)
        units = []
        for i in range(ng + 1):
            if i < ng:
                p_ref, _, rows, hs, _ = groups[i]
                units.append([hg_ab(p_ref, rows, hh) for hh in hs])
                each(hg_mb, units[i])
            if i >= 1:
                p_ref, mix_ref, rows, _, with_conv = groups[i - 1]
                if with_conv:
                    short_conv(p_ref, mix_ref, rows)
                each(hg_c, units[i - 1])
                each(hg_mc, units[i - 1])
            if i >= 2:
                each(hg_d, units[i - 2], groups[i - 2][1])
            if i == na - 1:
                in_norm_or_next(m + 2, h_b)
            if i == ng - 1:
                in_norm(jnp.minimum(m + 3, last), h_a)
            issue(i)
        each(hg_d, units[ng - 1], groups[ng - 1][1])
        return carry

    lax.fori_loop(0, n_macro // 2, trip, 0)
    for c in range(D_MODEL // PIECE):
        out_proj_piece(last, mix_b, c)


def _mix(x, norm, w_in, lb_logits, hg_norm, conv_w, conv_norm, w_out):
    bsz, t, _ = x.shape
    tt = MIX_ROWS
    tiles = t // tt

    def next_group(b, i):
        step = jnp.minimum(b * tiles + i + 1, bsz * tiles - 1)
        return step // tiles, (step % tiles) * (tt // MACRO), 0

    est = (4 * tt * D_MODEL * 4 + 2 * MACRO * D_MODEL * 4 + w_in.size * 2 + w_out.size * 2 + 2 * MACRO * IN_COLS * 4
           + HG_HEADS * HG_DK * HG_DK * 4 + (CHUNK + SUB) * CONV_WIDTH * 4
           + 4 * MACRO * D_MODEL * 2)
    return pl.pallas_call(
        _mix_kernel,
        grid=(bsz, t // tt),
        in_specs=[
            pl.BlockSpec((1, tt, D_MODEL), lambda b, i: (b, i, 0)),
            pl.BlockSpec((1, MACRO, D_MODEL), next_group),
            _resident((1, D_MODEL)),
            _resident((D_MODEL, IN_COLS)),
            _resident(lb_logits.shape),
            _resident((1, HG_DK)),
            _resident((CONV_K, CONV_WIDTH)),
            _resident((1, CONV_WIDTH)),
            _resident((D_MODEL, D_MODEL)),
        ],
        out_specs=pl.BlockSpec((1, tt, D_MODEL), lambda b, i: (b, i, 0)),
        out_shape=jax.ShapeDtypeStruct(x.shape, F32),
        scratch_shapes=[
            pltpu.VMEM((MACRO, IN_COLS), F32),
            pltpu.VMEM((MACRO, IN_COLS), F32),
            pltpu.VMEM((MACRO, D_MODEL), BF16),
            pltpu.VMEM((MACRO, D_MODEL), BF16),
            pltpu.VMEM((MACRO, D_MODEL), BF16),
            pltpu.VMEM((MACRO, D_MODEL), BF16),
            pltpu.VMEM((HG_HEADS, HG_DK, HG_DK), F32),
            pltpu.VMEM((CHUNK + SUB, CONV_WIDTH), F32),
        ],
        compiler_params=pltpu.CompilerParams(
            dimension_semantics=("arbitrary", "arbitrary"), vmem_limit_bytes=_vmem_limit(est)),
        name="mix",
    )(x, x, norm.reshape(1, D_MODEL), w_in, lb_logits, hg_norm.reshape(1, HG_DK), conv_w,
      conv_norm.reshape(1, CONV_WIDTH), w_out)


def _kv_kernel(m_ref, g_ref, w_ref, o_ref):
    h = _rmsnorm(m_ref[...], g_ref[...]).astype(BF16)
    o_ref[...] = _dot(h, w_ref[...]).astype(BF16)


def _kv(mem2d, norm, w_kv):
    n = mem2d.shape[0]
    est = 2 * KV_ROWS * D_MODEL * 4 + w_kv.size * 2 + 2 * KV_ROWS * 2 * D_MODEL * 2
    return pl.pallas_call(
        _kv_kernel,
        grid=(n // KV_ROWS,),
        in_specs=[
            pl.BlockSpec((KV_ROWS, D_MODEL), lambda i: (i, 0)),
            _resident((1, D_MODEL)),
            _resident((D_MODEL, 2 * D_MODEL)),
        ],
        out_specs=pl.BlockSpec((KV_ROWS, 2 * D_MODEL), lambda i: (i, 0)),
        out_shape=jax.ShapeDtypeStruct((n, 2 * D_MODEL), BF16),
        compiler_params=pltpu.CompilerParams(
            dimension_semantics=("arbitrary",), vmem_limit_bytes=_vmem_limit(est)),
        name="kv",
    )(mem2d, norm.reshape(1, D_MODEL), w_kv)


def _xattn_kernel(x_ref, g_ref, wq_ref, kv_ref, wo_ref, o_ref, h_ref, q_ref, s_ref, p_ref,
                  att_ref):
    n_sub = XA_ROWS // XA_SUB
    scale = 1.0 / math.sqrt(XA_HEAD_DIM)
    gain = g_ref[...]
    heads = [slice(hh * XA_HEAD_DIM, (hh + 1) * XA_HEAD_DIM) for hh in range(XA_HEADS)]

    def rows(i):
        return slice(i * XA_SUB, (i + 1) * XA_SUB)

    def norm(i):
        h_ref[i % 2] = _rmsnorm(x_ref[0, rows(i), :], gain).astype(BF16)

    def q_proj(i):
        for hd in heads:
            q_ref[i % 2, :, hd] = (_dot(h_ref[i % 2], wq_ref[:, hd]) * scale).astype(BF16)

    def scores(i):
        for hd in heads:
            s_ref[i % 2, :, hd] = _dot_nt(q_ref[i % 2, :, hd], kv_ref[0, :, hd])

    def softmax(i):
        for hd in heads:
            s = s_ref[i % 2, :, hd]
            e = jnp.exp(s - jnp.max(s, axis=-1, keepdims=True))
            p_ref[i % 2, :, hd] = (e * (1.0 / jnp.sum(e, axis=-1, keepdims=True))).astype(BF16)

    def pv(i):
        for hh, hd in enumerate(heads):
            v = kv_ref[0, :, D_MODEL + hh * XA_HEAD_DIM:D_MODEL + (hh + 1) * XA_HEAD_DIM]
            att_ref[i % 2, :, hd] = _dot(p_ref[i % 2, :, hd], v).astype(BF16)

    def out_proj(i):
        o_ref[0, rows(i), :] = x_ref[0, rows(i), :] + _dot(att_ref[i % 2], wo_ref[...])

    norm(0)
    if n_sub > 1:
        norm(1)
    q_proj(0)
    scores(0)
    for i in range(n_sub):
        if i + 1 < n_sub:
            q_proj(i + 1)
        softmax(i)
        pv(i)
        if i + 1 < n_sub:
            scores(i + 1)
        if i + 2 < n_sub:
            norm(i + 2)
        out_proj(i)


def _xattn(x, norm, w_q, kv, w_o):
    bsz, t, _ = x.shape
    tm = XA_ROWS
    est = (4 * tm * D_MODEL * 4 + w_q.size * 2 + w_o.size * 2 + 2 * N_MEM * 2 * D_MODEL * 2
           + 2 * XA_SUB * D_MODEL * (4 + 4 * 2))
    return pl.pallas_call(
        _xattn_kernel,
        grid=(bsz, t // tm),
        in_specs=[
            pl.BlockSpec((1, tm, D_MODEL), lambda b, i: (b, i, 0)),
            _resident((1, D_MODEL)),
            _resident((D_MODEL, D_MODEL)),
            pl.BlockSpec((1, N_MEM, 2 * D_MODEL), lambda b, i: (b, 0, 0)),
            _resident((D_MODEL, D_MODEL)),
        ],
        out_specs=pl.BlockSpec((1, tm, D_MODEL), lambda b, i: (b, i, 0)),
        out_shape=jax.ShapeDtypeStruct(x.shape, F32),
        scratch_shapes=[
            pltpu.VMEM((2, XA_SUB, D_MODEL), BF16),
            pltpu.VMEM((2, XA_SUB, D_MODEL), BF16),
            pltpu.VMEM((2, XA_SUB, D_MODEL), F32),
            pltpu.VMEM((2, XA_SUB, D_MODEL), BF16),
            pltpu.VMEM((2, XA_SUB, D_MODEL), BF16),
        ],
        compiler_params=pltpu.CompilerParams(
            dimension_semantics=("arbitrary", "arbitrary"), vmem_limit_bytes=_vmem_limit(est)),
        name="xattn",
    )(x, norm.reshape(1, D_MODEL), w_q, kv, w_o)


def kernel(x, mem, ffn1_norm, ffn1_w_gu, ffn1_w_down, mix_norm, w_in, lb_logits, hg_norm, conv_w,
           conv_norm, w_out, xa_norm, mem_norm, xa_wq, xa_wkv, xa_wo, ffn2_norm, ffn2_w_gu,
           ffn2_w_down, final_norm):
    bsz, t, d = x.shape
    assert d == D_MODEL and t % MIX_ROWS == 0 and (bsz * t) % FFN_ROWS == 0
    assert ffn1_norm.shape[0] == 1, "single-layer block"
    later = [w_in[0], w_out[0], xa_wq[0], xa_wkv[0], xa_wo[0], ffn2_w_gu[0], ffn2_w_down[0]]
    x1, later_bf = _ffn(x.reshape(bsz * t, d), ffn1_norm[0], ffn1_w_gu[0].astype(BF16),
                        ffn1_w_down[0].astype(BF16), final_norm, final_norm=False, cast=later)
    w_in_bf, w_out_bf, wq_bf, wkv_bf, wo_bf, w_gu2_bf, w_down2_bf = later_bf
    x2 = _mix(x1.reshape(bsz, t, d), mix_norm[0], w_in_bf, lb_logits, hg_norm[0], conv_w[0],
              conv_norm[0], w_out_bf)
    kv = _kv(mem.reshape(bsz * N_MEM, d), mem_norm[0], wkv_bf).reshape(bsz, N_MEM, 2 * d)
    x3 = _xattn(x2, xa_norm[0], wq_bf, kv, wo_bf)
    y, _ = _ffn(x3.reshape(bsz * t, d), ffn2_norm[0], w_gu2_bf, w_down2_bf, final_norm,
                final_norm=True)
    return y.reshape(bsz, t, d)
```

```python
import functools
import math

import jax
import jax.numpy as jnp
from jax import lax
from jax.experimental import pallas as pl
from jax.experimental.pallas import tpu as pltpu

D_MODEL = 1024
CHUNK = 64
SUB = 8
DIAG = 4
HG_WIDTH = 512
CONV_WIDTH = 512
HG_HEADS = 4
HG_DK = 128
CONV_K = 3
N_MEM = 256
XA_HEADS = 4
XA_HEAD_DIM = 256
D_FF = 2816
IN_COLS = 4 * HG_WIDTH + 3 * CONV_WIDTH
EPS = 1e-6

V7X_VMEM_BYTES = 64 * 1024 * 1024
MIB = 1024 * 1024

FFN_ROWS = 1024
FFN_SUB = 1024
FFN_COLS = 256
MIX_ROWS = 2048
MACRO = 128
PIECE = 256
GROUP_HEADS = 4
XA_ROWS = 2048
XA_SUB = 512
KV_ROWS = 256

BF16 = jnp.bfloat16
F32 = jnp.float32
BF16_ROWS = 16


def _vmem_limit(estimate_bytes):
    return int(min(estimate_bytes * 5 // 4 + 8 * MIB, V7X_VMEM_BYTES - 6 * MIB))


def _resident(shape):
    nd = len(shape)
    return pl.BlockSpec(shape, lambda *_: (0,) * nd, pipeline_mode=pl.Buffered(1))


def _rmsnorm(x, g):
    ms = jnp.mean(x * x, axis=-1, keepdims=True)
    return x * lax.rsqrt(ms + EPS) * g


def _silu(x):
    return x * (1.0 / (1.0 + jnp.exp(-x)))


def _dot(a, b):
    return jnp.dot(a, b, preferred_element_type=F32)


def _dot_nt(a, b):
    return lax.dot_general(a, b, (((1,), (1,)), ((), ())), preferred_element_type=F32)


def _dot_tn(a, b):
    return lax.dot_general(a, b, (((0,), (0,)), ((), ())), preferred_element_type=F32)


def _ffn_kernel(x_ref, g_ref, wgu_ref, wd_ref, fin_ref, *refs, final_norm, n_cast):
    cast_in, o_ref, cast_out = refs[:n_cast], refs[n_cast], refs[n_cast + 1:2 * n_cast + 1]
    h_ref, act_ref = refs[2 * n_cast + 1:]
    n_sub = FFN_ROWS // FFN_SUB
    gain = g_ref[...]

    def rows(i):
        return slice(i * FFN_SUB, (i + 1) * FFN_SUB)

    def norm(i):
        h_ref[rows(i), :] = _rmsnorm(x_ref[rows(i), :], gain).astype(BF16)

    def gate_up(i):
        h = h_ref[rows(i), :]
        for c in range(D_FF // FFN_COLS):
            lo = c * FFN_COLS
            gate = _dot(h, wgu_ref[:, lo:lo + FFN_COLS])
            up = _dot(h, wgu_ref[:, D_FF + lo:D_FF + lo + FFN_COLS])
            act_ref[rows(i), lo:lo + FFN_COLS] = (_silu(gate) * up).astype(BF16)

    def down(i):
        y = x_ref[rows(i), :] + 0.5 * _dot(act_ref[rows(i), :], wd_ref[...])
        if final_norm:
            y = _rmsnorm(y, fin_ref[...])
        o_ref[rows(i), :] = y

    norm(0)
    for i in range(n_sub):
        if i + 1 < n_sub:
            norm(i + 1)
        gate_up(i)
        if i >= 1:
            down(i - 1)
    down(n_sub - 1)

    for src, dst in zip(cast_in, cast_out):
        dst[...] = src[...].astype(BF16)


def _ffn(x2d, norm, w_gu, w_down, fin, *, final_norm, cast=()):
    n = x2d.shape[0]
    tm = FFN_ROWS
    steps = n // tm
    slices = [w.shape[0] // steps for w in cast]
    assert all(s % BF16_ROWS == 0 and s * steps == w.shape[0] for s, w in zip(slices, cast))
    est = (4 * tm * D_MODEL * 4 + w_gu.size * 2 + w_down.size * 2 + tm * D_FF * 2
           + tm * D_MODEL * 2 + 4 * FFN_SUB * FFN_COLS * 4
           + 2 * sum(s * w.shape[1] for s, w in zip(slices, cast)) * (4 + 2))
    cast_specs = [pl.BlockSpec((s, w.shape[1]), lambda i: (i, 0)) for s, w in zip(slices, cast)]
    out = pl.pallas_call(
        functools.partial(_ffn_kernel, final_norm=final_norm, n_cast=len(cast)),
        grid=(steps,),
        in_specs=[
            pl.BlockSpec((tm, D_MODEL), lambda i: (i, 0)),
            _resident((1, D_MODEL)),
            _resident((D_MODEL, 2 * D_FF)),
            _resident((D_FF, D_MODEL)),
            _resident((1, D_MODEL)),
        ] + cast_specs,
        out_specs=[pl.BlockSpec((tm, D_MODEL), lambda i: (i, 0))] + cast_specs,
        out_shape=[jax.ShapeDtypeStruct((n, D_MODEL), F32)]
        + [jax.ShapeDtypeStruct(w.shape, BF16) for w in cast],
        scratch_shapes=[pltpu.VMEM((tm, D_MODEL), BF16), pltpu.VMEM((tm, D_FF), BF16)],
        compiler_params=pltpu.CompilerParams(
            dimension_semantics=("arbitrary",), vmem_limit_bytes=_vmem_limit(est)),
        name="ffn_final" if final_norm else "ffn",
    )(x2d, norm.reshape(1, D_MODEL), w_gu, w_down, fin.reshape(1, D_MODEL), *cast)
    return out[0], list(out[1:])


def _chunk_cumsum(g):
    n = g.shape[-1]
    nb = CHUNK // SUB
    row = lax.broadcasted_iota(jnp.int32, (nb, SUB, n), 1)
    s = g.reshape(nb, SUB, n)
    for shift in (1, 2, 4):
        s = s + jnp.where(row >= shift, pltpu.roll(s, shift, 1), 0.0)
    carry = [jnp.zeros((1, 1, n), F32)]
    for i in range(1, nb):
        carry.append(carry[-1] + s[i - 1:i, SUB - 1:SUB, :])
    return (s + jnp.concatenate(carry, axis=0)).reshape(CHUNK, n)


def _hgrn2_masks():
    t = lax.broadcasted_iota(jnp.int32, (CHUNK, CHUNK), 0)
    s = lax.broadcasted_iota(jnp.int32, (CHUNK, CHUNK), 1)
    levels = []
    blk = CHUNK
    while blk > DIAG:
        half = blk // 2
        levels.append((t // blk == s // blk) & (t % blk >= half) & (s % blk < half))
        blk = half
    diag = [(s == t - d) & (t % DIAG >= d) for d in range(DIAG)]
    return levels, diag


def _mix_kernel(x_ref, xn_ref, nrm_ref, win_ref, lbl_ref, hgn_ref, cw_ref, cn_ref, wout_ref, o_ref,
                p_a, p_b, mix_a, mix_b, h_a, h_b, st_ref, ybuf_ref):
    n_macro = MIX_ROWS // MACRO

    @pl.when(pl.program_id(1) == 0)
    def _():
        st_ref[...] = jnp.zeros_like(st_ref)
        ybuf_ref[0:SUB, :] = jnp.zeros((SUB, CONV_WIDTH), F32)

    mix_b[...] = jnp.zeros((MACRO, D_MODEL), BF16)

    lbl = lbl_ref[...]
    lmax = jnp.max(lbl, axis=0, keepdims=True)
    lexp = jnp.exp(lbl - lmax)
    lb_all = lexp[0:1, :] / jnp.sum(lexp, axis=0, keepdims=True)

    level_masks, diag_masks = _hgrn2_masks()
    hgn = hgn_ref[...]
    nrm = nrm_ref[...]
    cw = cw_ref[...]
    cn = cn_ref[...]
    cb = 4 * HG_WIDTH

    def macro_rows(m):
        return pl.ds(pl.multiple_of(m * MACRO, MACRO), MACRO)

    def in_norm(m, h_ref):
        h_ref[...] = _rmsnorm(x_ref[0, macro_rows(m), :], nrm).astype(BF16)

    def in_norm_or_next(m, h_ref):
        rows = x_ref[0, macro_rows(jnp.minimum(m, n_macro - 1)), :]
        rows = jnp.where(m >= n_macro, xn_ref[0], rows)
        h_ref[...] = _rmsnorm(rows, nrm).astype(BF16)

    def in_proj_piece(h_ref, p_ref, c):
        cols = slice(c * PIECE, (c + 1) * PIECE)
        p_ref[:, cols] = _dot(h_ref[...], win_ref[:, cols])

    def out_proj_piece(m, mix_ref, c):
        rows = macro_rows(m)
        cols = slice(c * PIECE, (c + 1) * PIECE)
        o_ref[0, rows, cols] = x_ref[0, rows, cols] + _dot(mix_ref[...], wout_ref[:, cols])


    def hg_ab(p_ref, rows, hh):
        c0 = hh * HG_DK
        q = p_ref[rows, pl.ds(c0, HG_DK)]
        fz = p_ref[rows, pl.ds(HG_WIDTH + c0, HG_DK)]
        v = p_ref[rows, pl.ds(2 * HG_WIDTH + c0, HG_DK)]
        lb = lb_all[:, c0:c0 + HG_DK]
        f = lb + (1.0 - lb) * (1.0 / (1.0 + jnp.exp(-fz)))
        k = 1.0 - f
        qs = _silu(q)
        b = _chunk_cumsum(jnp.log2(f))
        b_end = b[CHUNK - 1:CHUNK, :]
        u = dict(hh=hh, rows=rows, p_ref=p_ref, v_bf=v.astype(BF16))
        q_bf, k_bf = qs.astype(BF16), k.astype(BF16)
        ops = []
        blk = CHUNK
        for _ in level_masks:
            half = blk // 2
            nb = CHUNK // blk
            ref = b.reshape(nb, blk, HG_DK)[:, half - 1:half, :]
            ref = jnp.broadcast_to(ref, (nb, blk, HG_DK)).reshape(CHUNK, HG_DK)
            z = jnp.exp2(-jnp.abs(b - ref)).astype(BF16)
            ops.append((q_bf * z, k_bf * z))
            blk = half
        u["level_ops"] = ops
        attn = jnp.where(diag_masks[0], jnp.sum(qs * k, axis=-1, keepdims=True), 0.0)
        f3 = f.reshape(CHUNK // SUB, SUB, HG_DK)
        decay = f
        for d in range(1, DIAG):
            f_d = pltpu.roll(f3, d, 1).reshape(CHUNK, HG_DK)
            r_d = jnp.sum(qs * (1.0 - f_d) * decay, axis=-1, keepdims=True)
            attn = jnp.where(diag_masks[d], r_d, attn)
            decay = decay * f_d
        u["attn_diag"] = attn
        u["qe"] = (qs * jnp.exp2(b)).astype(BF16)
        u["ke"] = (k * jnp.exp2(b_end - b)).astype(BF16)
        u["decay"] = jnp.exp2(b_end)
        return u

    def hg_mb(u):
        u["level_out"] = [_dot_nt(ql, kl) for ql, kl in u.pop("level_ops")]

    def hg_c(u):
        attn = u.pop("attn_diag")
        for lm, a in zip(level_masks, u.pop("level_out")):
            attn = jnp.where(lm, a, attn)
        u["attn"] = attn.astype(BF16)

    def hg_mc(u):
        hh, v_bf = u["hh"], u.pop("v_bf")
        st = st_ref[hh]
        o = _dot(u.pop("attn"), v_bf)
        u["o"] = o + _dot(u.pop("qe"), st.astype(BF16))
        decay_col = jnp.transpose(jnp.broadcast_to(u.pop("decay"), (HG_DK, HG_DK)))
        st_ref[hh] = st * decay_col + _dot_tn(u.pop("ke"), v_bf)

    def hg_d(u, mix_ref):
        c0 = u["hh"] * HG_DK
        gt = u["p_ref"][u["rows"], pl.ds(3 * HG_WIDTH + c0, HG_DK)]
        on = _rmsnorm(u.pop("o"), hgn) * _silu(gt)
        mix_ref[u["rows"], pl.ds(c0, HG_DK)] = on.astype(BF16)

    def short_conv(p_ref, mix_ref, rows):
        y = p_ref[rows, cb + CONV_WIDTH:cb + 2 * CONV_WIDTH] * \
            p_ref[rows, cb + 2 * CONV_WIDTH:cb + 3 * CONV_WIDTH]
        ybuf_ref[SUB:SUB + CHUNK, :] = y
        y1 = ybuf_ref[SUB - 1:SUB - 1 + CHUNK, :]
        y2 = ybuf_ref[SUB - 2:SUB - 2 + CHUNK, :]
        conv = cw[0:1, :] * y2 + cw[1:2, :] * y1 + cw[2:3, :] * y
        ocv = _rmsnorm(p_ref[rows, cb:cb + CONV_WIDTH] * conv, cn)
        mix_ref[rows, HG_WIDTH:HG_WIDTH + CONV_WIDTH] = ocv.astype(BF16)
        ybuf_ref[0:SUB, :] = ybuf_ref[CHUNK:CHUNK + SUB, :]

    def each(phase, units, *args):
        for u in units:
            phase(u, *args)

    n_chunks = MACRO // CHUNK
    chunk_rows = [pl.ds(j * CHUNK, CHUNK) for j in range(n_chunks)]
    head_groups = [tuple(range(h, h + GROUP_HEADS)) for h in range(0, HG_HEADS, GROUP_HEADS)]
    last = n_macro - 1

    @pl.when((pl.program_id(0) == 0) & (pl.program_id(1) == 0))
    def _():
        in_norm(0, h_b)
        for c in range(IN_COLS // PIECE):
            in_proj_piece(h_b, p_a, c)

    in_norm(1, h_a)

    def trip(j, carry):
        m = 2 * j
        big_a = [functools.partial(in_proj_piece, h_a, p_b, c) for c in range(IN_COLS // PIECE)]
        big_a += [functools.partial(out_proj_piece, jnp.maximum(m - 1, 0), mix_b, c)
                  for c in range(D_MODEL // PIECE)]
        big_b = [functools.partial(in_proj_piece, h_b, p_a, c) for c in range(IN_COLS // PIECE)]
        big_b += [functools.partial(out_proj_piece, m, mix_a, c) for c in range(D_MODEL // PIECE)]
        groups = [(p_ref, mix_ref, r, hs, hs[0] == 0)
                  for p_ref, mix_ref in ((p_a, mix_a), (p_b, mix_b)) for r in chunk_rows
                  for hs in head_groups]
        ng = len(groups)
        na = ng // 2
        plan = {i: [] for i in range(-1, ng + 1)}
        for pieces, slots in ((big_a, range(-1, na + 1)), (big_b, range(na + 1, ng + 1))):
            for k, piece in enumerate(pieces):
                plan[slots[k * len(slots) // len(pieces)]].append(piece)

        for piece in plan[-1]:
            piece()
        units = []
        for i in range(ng + 1):
            if i < ng:
                p_ref, _, rows, hs, _ = groups[i]
                units.append([hg_ab(p_ref, rows, hh) for hh in hs])
            if i >= 1:
                p_ref, mix_ref, rows, _, with_conv = groups[i - 1]
                if with_conv:
                    short_conv(p_ref, mix_ref, rows)
                each(hg_c, units[i - 1])
            if i >= 2:
                each(hg_d, units[i - 2], groups[i - 2][1])
            if i == na - 1:
                in_norm_or_next(m + 2, h_b)
            if i == ng - 1:
                in_norm(jnp.minimum(m + 3, last), h_a)
            pieces = list(plan[i])
            small = [(hg_mb, u) for u in (units[i] if i < ng else [])]
            small += [(hg_mc, u) for u in (units[i - 1] if i >= 1 else [])]
            for phase, u in small:
                phase(u)
                if pieces:
                    pieces.pop(0)()
            for piece in pieces:
                piece()
        each(hg_d, units[ng - 1], groups[ng - 1][1])
        return carry

    lax.fori_loop(0, n_macro // 2, trip, 0, unroll=2)
    for c in range(D_MODEL // PIECE):
        out_proj_piece(last, mix_b, c)


def _mix(x, norm, w_in, lb_logits, hg_norm, conv_w, conv_norm, w_out):
    bsz, t, _ = x.shape
    tt = MIX_ROWS
    tiles = t // tt

    def next_group(b, i):
        step = jnp.minimum(b * tiles + i + 1, bsz * tiles - 1)
        return step // tiles, (step % tiles) * (tt // MACRO), 0

    est = (4 * tt * D_MODEL * 4 + 2 * MACRO * D_MODEL * 4 + w_in.size * 2 + w_out.size * 2 + 2 * MACRO * IN_COLS * 4
           + HG_HEADS * HG_DK * HG_DK * 4 + (CHUNK + SUB) * CONV_WIDTH * 4
           + 4 * MACRO * D_MODEL * 2)
    return pl.pallas_call(
        _mix_kernel,
        grid=(bsz, t // tt),
        in_specs=[
            pl.BlockSpec((1, tt, D_MODEL), lambda b, i: (b, i, 0)),
            pl.BlockSpec((1, MACRO, D_MODEL), next_group),
            _resident((1, D_MODEL)),
            _resident((D_MODEL, IN_COLS)),
            _resident(lb_logits.shape),
            _resident((1, HG_DK)),
            _resident((CONV_K, CONV_WIDTH)),
            _resident((1, CONV_WIDTH)),
            _resident((D_MODEL, D_MODEL)),
        ],
        out_specs=pl.BlockSpec((1, tt, D_MODEL), lambda b, i: (b, i, 0)),
        out_shape=jax.ShapeDtypeStruct(x.shape, F32),
        scratch_shapes=[
            pltpu.VMEM((MACRO, IN_COLS), F32),
            pltpu.VMEM((MACRO, IN_COLS), F32),
            pltpu.VMEM((MACRO, D_MODEL), BF16),
            pltpu.VMEM((MACRO, D_MODEL), BF16),
            pltpu.VMEM((MACRO, D_MODEL), BF16),
            pltpu.VMEM((MACRO, D_MODEL), BF16),
            pltpu.VMEM((HG_HEADS, HG_DK, HG_DK), F32),
            pltpu.VMEM((CHUNK + SUB, CONV_WIDTH), F32),
        ],
        compiler_params=pltpu.CompilerParams(
            dimension_semantics=("arbitrary", "arbitrary"), vmem_limit_bytes=_vmem_limit(est)),
        name="mix",
    )(x, x, norm.reshape(1, D_MODEL), w_in, lb_logits, hg_norm.reshape(1, HG_DK), conv_w,
      conv_norm.reshape(1, CONV_WIDTH), w_out)


def _kv_kernel(m_ref, g_ref, w_ref, o_ref):
    h = _rmsnorm(m_ref[...], g_ref[...]).astype(BF16)
    o_ref[...] = _dot(h, w_ref[...]).astype(BF16)


def _kv(mem2d, norm, w_kv):
    n = mem2d.shape[0]
    est = 2 * KV_ROWS * D_MODEL * 4 + w_kv.size * 2 + 2 * KV_ROWS * 2 * D_MODEL * 2
    return pl.pallas_call(
        _kv_kernel,
        grid=(n // KV_ROWS,),
        in_specs=[
            pl.BlockSpec((KV_ROWS, D_MODEL), lambda i: (i, 0)),
            _resident((1, D_MODEL)),
            _resident((D_MODEL, 2 * D_MODEL)),
        ],
        out_specs=pl.BlockSpec((KV_ROWS, 2 * D_MODEL), lambda i: (i, 0)),
        out_shape=jax.ShapeDtypeStruct((n, 2 * D_MODEL), BF16),
        compiler_params=pltpu.CompilerParams(
            dimension_semantics=("arbitrary",), vmem_limit_bytes=_vmem_limit(est)),
        name="kv",
    )(mem2d, norm.reshape(1, D_MODEL), w_kv)


def _xattn_kernel(x_ref, g_ref, wq_ref, kv_ref, wo_ref, o_ref, h_ref, q_ref, s_ref, p_ref,
                  att_ref):
    n_sub = XA_ROWS // XA_SUB
    scale = 1.0 / math.sqrt(XA_HEAD_DIM)
    gain = g_ref[...]
    heads = [slice(hh * XA_HEAD_DIM, (hh + 1) * XA_HEAD_DIM) for hh in range(XA_HEADS)]

    def rows(i):
        return slice(i * XA_SUB, (i + 1) * XA_SUB)

    def norm(i):
        h_ref[i % 2] = _rmsnorm(x_ref[0, rows(i), :], gain).astype(BF16)

    def q_proj(i):
        for hd in heads:
            q_ref[i % 2, :, hd] = (_dot(h_ref[i % 2], wq_ref[:, hd]) * scale).astype(BF16)

    def scores(i):
        for hd in heads:
            s_ref[i % 2, :, hd] = _dot_nt(q_ref[i % 2, :, hd], kv_ref[0, :, hd])

    def softmax(i):
        for hd in heads:
            s = s_ref[i % 2, :, hd]
            e = jnp.exp(s - jnp.max(s, axis=-1, keepdims=True))
            p_ref[i % 2, :, hd] = (e * (1.0 / jnp.sum(e, axis=-1, keepdims=True))).astype(BF16)

    def pv(i):
        for hh, hd in enumerate(heads):
            v = kv_ref[0, :, D_MODEL + hh * XA_HEAD_DIM:D_MODEL + (hh + 1) * XA_HEAD_DIM]
            att_ref[i % 2, :, hd] = _dot(p_ref[i % 2, :, hd], v).astype(BF16)

    def out_proj(i):
        o_ref[0, rows(i), :] = x_ref[0, rows(i), :] + _dot(att_ref[i % 2], wo_ref[...])

    norm(0)
    if n_sub > 1:
        norm(1)
    q_proj(0)
    scores(0)
    for i in range(n_sub):
        if i + 1 < n_sub:
            q_proj(i + 1)
        softmax(i)
        pv(i)
        if i + 1 < n_sub:
            scores(i + 1)
        if i + 2 < n_sub:
            norm(i + 2)
        out_proj(i)


def _xattn(x, norm, w_q, kv, w_o):
    bsz, t, _ = x.shape
    tm = XA_ROWS
    est = (4 * tm * D_MODEL * 4 + w_q.size * 2 + w_o.size * 2 + 2 * N_MEM * 2 * D_MODEL * 2
           + 2 * XA_SUB * D_MODEL * (4 + 4 * 2))
    return pl.pallas_call(
        _xattn_kernel,
        grid=(bsz, t // tm),
        in_specs=[
            pl.BlockSpec((1, tm, D_MODEL), lambda b, i: (b, i, 0)),
            _resident((1, D_MODEL)),
            _resident((D_MODEL, D_MODEL)),
            pl.BlockSpec((1, N_MEM, 2 * D_MODEL), lambda b, i: (b, 0, 0)),
            _resident((D_MODEL, D_MODEL)),
        ],
        out_specs=pl.BlockSpec((1, tm, D_MODEL), lambda b, i: (b, i, 0)),
        out_shape=jax.ShapeDtypeStruct(x.shape, F32),
        scratch_shapes=[
            pltpu.VMEM((2, XA_SUB, D_MODEL), BF16),
            pltpu.VMEM((2, XA_SUB, D_MODEL), BF16),
            pltpu.VMEM((2, XA_SUB, D_MODEL), F32),
            pltpu.VMEM((2, XA_SUB, D_MODEL), BF16),
            pltpu.VMEM((2, XA_SUB, D_MODEL), BF16),
        ],
        compiler_params=pltpu.CompilerParams(
            dimension_semantics=("arbitrary", "arbitrary"), vmem_limit_bytes=_vmem_limit(est)),
        name="xattn",
    )(x, norm.reshape(1, D_MODEL), w_q, kv, w_o)


def kernel(x, mem, ffn1_norm, ffn1_w_gu, ffn1_w_down, mix_norm, w_in, lb_logits, hg_norm, conv_w,
           conv_norm, w_out, xa_norm, mem_norm, xa_wq, xa_wkv, xa_wo, ffn2_norm, ffn2_w_gu,
           ffn2_w_down, final_norm):
    bsz, t, d = x.shape
    assert d == D_MODEL and t % MIX_ROWS == 0 and (bsz * t) % FFN_ROWS == 0
    assert ffn1_norm.shape[0] == 1, "single-layer block"
    later = [w_in[0], w_out[0], xa_wq[0], xa_wkv[0], xa_wo[0], ffn2_w_gu[0], ffn2_w_down[0]]
    x1, later_bf = _ffn(x.reshape(bsz * t, d), ffn1_norm[0], ffn1_w_gu[0].astype(BF16),
                        ffn1_w_down[0].astype(BF16), final_norm, final_norm=False, cast=later)
    w_in_bf, w_out_bf, wq_bf, wkv_bf, wo_bf, w_gu2_bf, w_down2_bf = later_bf
    x2 = _mix(x1.reshape(bsz, t, d), mix_norm[0], w_in_bf, lb_logits, hg_norm[0], conv_w[0],
              conv_norm[0], w_out_bf)
    kv = _kv(mem.reshape(bsz * N_MEM, d), mem_norm[0], wkv_bf).reshape(bsz, N_MEM, 2 * d)
    x3 = _xattn(x2, xa_norm[0], wq_bf, kv, wo_bf)
    y, _ = _ffn(x3.reshape(bsz * t, d), ffn2_norm[0], w_gu2_bf, w_down2_bf, final_norm,
                final_norm=True)
    return y.reshape(bsz, t, d)
```

```python
import functools
import math

import jax
import jax.numpy as jnp
from jax import lax
from jax.experimental import pallas as pl
from jax.experimental.pallas import tpu as pltpu

D_MODEL = 1024
CHUNK = 64
SUB = 8
DIAG = 4
HG_WIDTH = 512
CONV_WIDTH = 512
HG_HEADS = 4
HG_DK = 128
CONV_K = 3
N_MEM = 256
XA_HEADS = 4
XA_HEAD_DIM = 256
D_FF = 2816
IN_COLS = 4 * HG_WIDTH + 3 * CONV_WIDTH
EPS = 1e-6

V7X_VMEM_BYTES = 64 * 1024 * 1024
MIB = 1024 * 1024

FFN_ROWS = 1024
FFN_SUB = 1024
FFN_COLS = 256
MIX_ROWS = 2048
MACRO = 128
PIECE = 256
GROUP_HEADS = 4
XA_ROWS = 2048
XA_SUB = 512
KV_ROWS = 256

BF16 = jnp.bfloat16
F32 = jnp.float32
BF16_ROWS = 16


VMEM_TEMP_SHARE = 4
VMEM_TEMP_FLOOR = 8 * MIB
V7X_VMEM_KEPT_FREE = 6 * MIB


def _vmem_limit(estimate_bytes):
    request = estimate_bytes + estimate_bytes // VMEM_TEMP_SHARE + VMEM_TEMP_FLOOR
    return int(min(request, V7X_VMEM_BYTES - V7X_VMEM_KEPT_FREE))


def _resident(shape):
    nd = len(shape)
    return pl.BlockSpec(shape, lambda *_: (0,) * nd, pipeline_mode=pl.Buffered(1))


def _rmsnorm(x, g):
    ms = jnp.mean(x * x, axis=-1, keepdims=True)
    return x * lax.rsqrt(ms + EPS) * g


def _silu(x):
    return x * (1.0 / (1.0 + jnp.exp(-x)))


def _dot(a, b):
    return jnp.dot(a, b, preferred_element_type=F32)


def _dot_nt(a, b):
    return lax.dot_general(a, b, (((1,), (1,)), ((), ())), preferred_element_type=F32)


def _dot_tn(a, b):
    return lax.dot_general(a, b, (((0,), (0,)), ((), ())), preferred_element_type=F32)


def _ffn_kernel(x_ref, g_ref, wgu_ref, wd_ref, fin_ref, *refs, final_norm, n_cast):
    cast_in, o_ref, cast_out = refs[:n_cast], refs[n_cast], refs[n_cast + 1:2 * n_cast + 1]
    h_ref, act_ref = refs[2 * n_cast + 1:]
    n_sub = FFN_ROWS // FFN_SUB
    gain = g_ref[...]

    def rows(i):
        return slice(i * FFN_SUB, (i + 1) * FFN_SUB)

    def norm(i):
        h_ref[rows(i), :] = _rmsnorm(x_ref[rows(i), :], gain).astype(BF16)

    def gate_up(i):
        h = h_ref[rows(i), :]
        for c in range(D_FF // FFN_COLS):
            lo = c * FFN_COLS
            gate = _dot(h, wgu_ref[:, lo:lo + FFN_COLS])
            up = _dot(h, wgu_ref[:, D_FF + lo:D_FF + lo + FFN_COLS])
            act_ref[rows(i), lo:lo + FFN_COLS] = (_silu(gate) * up).astype(BF16)

    def down(i):
        y = x_ref[rows(i), :] + 0.5 * _dot(act_ref[rows(i), :], wd_ref[...])
        if final_norm:
            y = _rmsnorm(y, fin_ref[...])
        o_ref[rows(i), :] = y

    norm(0)
    for i in range(n_sub):
        if i + 1 < n_sub:
            norm(i + 1)
        gate_up(i)
        if i >= 1:
            down(i - 1)
    down(n_sub - 1)

    for src, dst in zip(cast_in, cast_out):
        dst[...] = src[...].astype(BF16)


def _ffn(x2d, norm, w_gu, w_down, fin, *, final_norm, cast=()):
    n = x2d.shape[0]
    tm = FFN_ROWS
    steps = n // tm
    slices = [w.shape[0] // steps for w in cast]
    assert all(s % BF16_ROWS == 0 and s * steps == w.shape[0] for s, w in zip(slices, cast))
    est = (4 * tm * D_MODEL * 4 + w_gu.size * 2 + w_down.size * 2 + tm * D_FF * 2
           + tm * D_MODEL * 2 + 4 * FFN_SUB * FFN_COLS * 4
           + 2 * sum(s * w.shape[1] for s, w in zip(slices, cast)) * (4 + 2))
    cast_specs = [pl.BlockSpec((s, w.shape[1]), lambda i: (i, 0)) for s, w in zip(slices, cast)]
    out = pl.pallas_call(
        functools.partial(_ffn_kernel, final_norm=final_norm, n_cast=len(cast)),
        grid=(steps,),
        in_specs=[
            pl.BlockSpec((tm, D_MODEL), lambda i: (i, 0)),
            _resident((1, D_MODEL)),
            _resident((D_MODEL, 2 * D_FF)),
            _resident((D_FF, D_MODEL)),
            _resident((1, D_MODEL)),
        ] + cast_specs,
        out_specs=[pl.BlockSpec((tm, D_MODEL), lambda i: (i, 0))] + cast_specs,
        out_shape=[jax.ShapeDtypeStruct((n, D_MODEL), F32)]
        + [jax.ShapeDtypeStruct(w.shape, BF16) for w in cast],
        scratch_shapes=[pltpu.VMEM((tm, D_MODEL), BF16), pltpu.VMEM((tm, D_FF), BF16)],
        compiler_params=pltpu.CompilerParams(
            dimension_semantics=("arbitrary",), vmem_limit_bytes=_vmem_limit(est)),
        name="ffn_final" if final_norm else "ffn",
    )(x2d, norm.reshape(1, D_MODEL), w_gu, w_down, fin.reshape(1, D_MODEL), *cast)
    return out[0], list(out[1:])


def _chunk_cumsum(g):
    n = g.shape[-1]
    nb = CHUNK // SUB
    row = lax.broadcasted_iota(jnp.int32, (nb, SUB, n), 1)
    s = g.reshape(nb, SUB, n)
    for shift in (1, 2, 4):
        s = s + jnp.where(row >= shift, pltpu.roll(s, shift, 1), 0.0)
    carry = [jnp.zeros((1, 1, n), F32)]
    for i in range(1, nb):
        carry.append(carry[-1] + s[i - 1:i, SUB - 1:SUB, :])
    return (s + jnp.concatenate(carry, axis=0)).reshape(CHUNK, n)


def _hgrn2_masks():
    t = lax.broadcasted_iota(jnp.int32, (CHUNK, CHUNK), 0)
    s = lax.broadcasted_iota(jnp.int32, (CHUNK, CHUNK), 1)
    levels = []
    blk = CHUNK
    while blk > DIAG:
        half = blk // 2
        levels.append((t // blk == s // blk) & (t % blk >= half) & (s % blk < half))
        blk = half
    diag = [(s == t - d) & (t % DIAG >= d) for d in range(DIAG)]
    return levels, diag


def _mix_kernel(x_ref, xn_ref, nrm_ref, win_ref, lbl_ref, hgn_ref, cw_ref, cn_ref, wout_ref, o_ref,
                p_a, p_b, mix_a, mix_b, h_a, h_b, st_ref, ybuf_ref):
    n_macro = MIX_ROWS // MACRO

    @pl.when(pl.program_id(1) == 0)
    def _():
        st_ref[...] = jnp.zeros_like(st_ref)
        ybuf_ref[0:SUB, :] = jnp.zeros((SUB, CONV_WIDTH), F32)

    mix_b[...] = jnp.zeros((MACRO, D_MODEL), BF16)

    lbl = lbl_ref[...]
    lmax = jnp.max(lbl, axis=0, keepdims=True)
    lexp = jnp.exp(lbl - lmax)
    lb_all = lexp[0:1, :] / jnp.sum(lexp, axis=0, keepdims=True)

    level_masks, diag_masks = _hgrn2_masks()
    hgn = hgn_ref[...]
    nrm = nrm_ref[...]
    cw = cw_ref[...]
    cn = cn_ref[...]
    cb = 4 * HG_WIDTH

    def macro_rows(m):
        return pl.ds(pl.multiple_of(m * MACRO, MACRO), MACRO)

    def in_norm(m, h_ref):
        h_ref[...] = _rmsnorm(x_ref[0, macro_rows(m), :], nrm).astype(BF16)

    def in_norm_or_next(m, h_ref):
        rows = x_ref[0, macro_rows(jnp.minimum(m, n_macro - 1)), :]
        rows = jnp.where(m >= n_macro, xn_ref[0], rows)
        h_ref[...] = _rmsnorm(rows, nrm).astype(BF16)

    def in_proj_piece(h_ref, p_ref, c):
        cols = slice(c * PIECE, (c + 1) * PIECE)
        p_ref[:, cols] = _dot(h_ref[...], win_ref[:, cols])

    def out_proj_piece(m, mix_ref, c):
        rows = macro_rows(m)
        cols = slice(c * PIECE, (c + 1) * PIECE)
        o_ref[0, rows, cols] = x_ref[0, rows, cols] + _dot(mix_ref[...], wout_ref[:, cols])


    def hg_ab(p_ref, rows, hh):
        c0 = hh * HG_DK
        q = p_ref[rows, pl.ds(c0, HG_DK)]
        fz = p_ref[rows, pl.ds(HG_WIDTH + c0, HG_DK)]
        v = p_ref[rows, pl.ds(2 * HG_WIDTH + c0, HG_DK)]
        lb = lb_all[:, c0:c0 + HG_DK]
        f = lb + (1.0 - lb) * (1.0 / (1.0 + jnp.exp(-fz)))
        k = 1.0 - f
        qs = _silu(q)
        b = _chunk_cumsum(jnp.log2(f))
        b_end = b[CHUNK - 1:CHUNK, :]
        u = dict(hh=hh, rows=rows, p_ref=p_ref, v_bf=v.astype(BF16))
        q_bf, k_bf = qs.astype(BF16), k.astype(BF16)
        ops = []
        blk = CHUNK
        for _ in level_masks:
            half = blk // 2
            nb = CHUNK // blk
            ref = b.reshape(nb, blk, HG_DK)[:, half - 1:half, :]
            ref = jnp.broadcast_to(ref, (nb, blk, HG_DK)).reshape(CHUNK, HG_DK)
            z = jnp.exp2(-jnp.abs(b - ref)).astype(BF16)
            ops.append((q_bf * z, k_bf * z))
            blk = half
        u["level_ops"] = ops
        attn = jnp.where(diag_masks[0], jnp.sum(qs * k, axis=-1, keepdims=True), 0.0)
        f3 = f.reshape(CHUNK // SUB, SUB, HG_DK)
        decay = f
        for d in range(1, DIAG):
            f_d = pltpu.roll(f3, d, 1).reshape(CHUNK, HG_DK)
            r_d = jnp.sum(qs * (1.0 - f_d) * decay, axis=-1, keepdims=True)
            attn = jnp.where(diag_masks[d], r_d, attn)
            decay = decay * f_d
        u["attn_diag"] = attn
        u["qe"] = (qs * jnp.exp2(b)).astype(BF16)
        u["ke"] = (k * jnp.exp2(b_end - b)).astype(BF16)
        u["decay"] = jnp.exp2(b_end)
        return u

    def hg_mb(u):
        u["level_out"] = [_dot_nt(ql, kl) for ql, kl in u.pop("level_ops")]

    def hg_c(u):
        attn = u.pop("attn_diag")
        for lm, a in zip(level_masks, u.pop("level_out")):
            attn = jnp.where(lm, a, attn)
        u["attn"] = attn.astype(BF16)

    def hg_mc(u):
        hh, v_bf = u["hh"], u.pop("v_bf")
        st = st_ref[hh]
        o = _dot(u.pop("attn"), v_bf)
        u["o"] = o + _dot(u.pop("qe"), st.astype(BF16))
        decay_col = jnp.transpose(jnp.broadcast_to(u.pop("decay"), (HG_DK, HG_DK)))
        st_ref[hh] = st * decay_col + _dot_tn(u.pop("ke"), v_bf)

    def hg_d(u, mix_ref):
        c0 = u["hh"] * HG_DK
        gt = u["p_ref"][u["rows"], pl.ds(3 * HG_WIDTH + c0, HG_DK)]
        on = _rmsnorm(u.pop("o"), hgn) * _silu(gt)
        mix_ref[u["rows"], pl.ds(c0, HG_DK)] = on.astype(BF16)

    def short_conv(p_ref, mix_ref, rows):
        y = p_ref[rows, cb + CONV_WIDTH:cb + 2 * CONV_WIDTH] * \
            p_ref[rows, cb + 2 * CONV_WIDTH:cb + 3 * CONV_WIDTH]
        ybuf_ref[SUB:SUB + CHUNK, :] = y
        y1 = ybuf_ref[SUB - 1:SUB - 1 + CHUNK, :]
        y2 = ybuf_ref[SUB - 2:SUB - 2 + CHUNK, :]
        conv = cw[0:1, :] * y2 + cw[1:2, :] * y1 + cw[2:3, :] * y
        ocv = _rmsnorm(p_ref[rows, cb:cb + CONV_WIDTH] * conv, cn)
        mix_ref[rows, HG_WIDTH:HG_WIDTH + CONV_WIDTH] = ocv.astype(BF16)
        ybuf_ref[0:SUB, :] = ybuf_ref[CHUNK:CHUNK + SUB, :]

    def each(phase, units, *args):
        for u in units:
            phase(u, *args)

    n_chunks = MACRO // CHUNK
    chunk_rows = [pl.ds(j * CHUNK, CHUNK) for j in range(n_chunks)]
    head_groups = [tuple(range(h, h + GROUP_HEADS)) for h in range(0, HG_HEADS, GROUP_HEADS)]
    last = n_macro - 1

    @pl.when((pl.program_id(0) == 0) & (pl.program_id(1) == 0))
    def _():
        in_norm(0, h_b)
        for c in range(IN_COLS // PIECE):
            in_proj_piece(h_b, p_a, c)

    in_norm(1, h_a)

    def trip(j, carry):
        m = 2 * j
        big_a = [functools.partial(in_proj_piece, h_a, p_b, c) for c in range(IN_COLS // PIECE)]
        big_a += [functools.partial(out_proj_piece, jnp.maximum(m - 1, 0), mix_b, c)
                  for c in range(D_MODEL // PIECE)]
        big_b = [functools.partial(in_proj_piece, h_b, p_a, c) for c in range(IN_COLS // PIECE)]
        big_b += [functools.partial(out_proj_piece, m, mix_a, c) for c in range(D_MODEL // PIECE)]
        groups = [(p_ref, mix_ref, r, hs, hs[0] == 0)
                  for p_ref, mix_ref in ((p_a, mix_a), (p_b, mix_b)) for r in chunk_rows
                  for hs in head_groups]
        ng = len(groups)
        na = ng // 2
        plan = {i: [] for i in range(-1, ng + 1)}
        for pieces, slots in ((big_a, range(-1, na + 1)), (big_b, range(na + 1, ng + 1))):
            for k, piece in enumerate(pieces):
                plan[slots[k * len(slots) // len(pieces)]].append(piece)

        def issue(slot):
            for piece in plan[slot]:
                piece()

        issue(-1)
        units = []
        for i in range(ng + 1):
            if i < ng:
                p_ref, _, rows, hs, _ = groups[i]
                units.append([hg_ab(p_ref, rows, hh) for hh in hs])
                each(hg_mb, units[i])
            if i >= 1:
                p_ref, mix_ref, rows, _, with_conv = groups[i - 1]
                if with_conv:
                    short_conv(p_ref, mix_ref, rows)
                each(hg_c, units[i - 1])
                each(hg_mc, units[i - 1])
            if i >= 2:
                each(hg_d, units[i - 2], groups[i - 2][1])
            if i == na - 1:
                in_norm_or_next(m + 2, h_b)
            if i == ng - 1:
                in_norm(jnp.minimum(m + 3, last), h_a)
            issue(i)
        each(hg_d, units[ng - 1], groups[ng - 1][1])
        return carry

    lax.fori_loop(0, n_macro // 2, trip, 0, unroll=2)
    for c in range(D_MODEL // PIECE):
        out_proj_piece(last, mix_b, c)


def _mix(x, norm, w_in, lb_logits, hg_norm, conv_w, conv_norm, w_out):
    bsz, t, _ = x.shape
    tt = MIX_ROWS
    tiles = t // tt

    def next_group(b, i):
        step = jnp.minimum(b * tiles + i + 1, bsz * tiles - 1)
        return step // tiles, (step % tiles) * (tt // MACRO), 0

    est = (4 * tt * D_MODEL * 4 + 2 * MACRO * D_MODEL * 4 + w_in.size * 2 + w_out.size * 2 + 2 * MACRO * IN_COLS * 4
           + HG_HEADS * HG_DK * HG_DK * 4 + (CHUNK + SUB) * CONV_WIDTH * 4
           + 4 * MACRO * D_MODEL * 2)
    return pl.pallas_call(
        _mix_kernel,
        grid=(bsz, t // tt),
        in_specs=[
            pl.BlockSpec((1, tt, D_MODEL), lambda b, i: (b, i, 0)),
            pl.BlockSpec((1, MACRO, D_MODEL), next_group),
            _resident((1, D_MODEL)),
            _resident((D_MODEL, IN_COLS)),
            _resident(lb_logits.shape),
            _resident((1, HG_DK)),
            _resident((CONV_K, CONV_WIDTH)),
            _resident((1, CONV_WIDTH)),
            _resident((D_MODEL, D_MODEL)),
        ],
        out_specs=pl.BlockSpec((1, tt, D_MODEL), lambda b, i: (b, i, 0)),
        out_shape=jax.ShapeDtypeStruct(x.shape, F32),
        scratch_shapes=[
            pltpu.VMEM((MACRO, IN_COLS), F32),
            pltpu.VMEM((MACRO, IN_COLS), F32),
            pltpu.VMEM((MACRO, D_MODEL), BF16),
            pltpu.VMEM((MACRO, D_MODEL), BF16),
            pltpu.VMEM((MACRO, D_MODEL), BF16),
            pltpu.VMEM((MACRO, D_MODEL), BF16),
            pltpu.VMEM((HG_HEADS, HG_DK, HG_DK), F32),
            pltpu.VMEM((CHUNK + SUB, CONV_WIDTH), F32),
        ],
        compiler_params=pltpu.CompilerParams(
            dimension_semantics=("arbitrary", "arbitrary"), vmem_limit_bytes=_vmem_limit(est)),
        name="mix",
    )(x, x, norm.reshape(1, D_MODEL), w_in, lb_logits, hg_norm.reshape(1, HG_DK), conv_w,
      conv_norm.reshape(1, CONV_WIDTH), w_out)


def _kv_kernel(m_ref, g_ref, w_ref, o_ref):
    h = _rmsnorm(m_ref[...], g_ref[...]).astype(BF16)
    o_ref[...] = _dot(h, w_ref[...]).astype(BF16)


def _kv(mem2d, norm, w_kv):
    n = mem2d.shape[0]
    est = 2 * KV_ROWS * D_MODEL * 4 + w_kv.size * 2 + 2 * KV_ROWS * 2 * D_MODEL * 2
    return pl.pallas_call(
        _kv_kernel,
        grid=(n // KV_ROWS,),
        in_specs=[
            pl.BlockSpec((KV_ROWS, D_MODEL), lambda i: (i, 0)),
            _resident((1, D_MODEL)),
            _resident((D_MODEL, 2 * D_MODEL)),
        ],
        out_specs=pl.BlockSpec((KV_ROWS, 2 * D_MODEL), lambda i: (i, 0)),
        out_shape=jax.ShapeDtypeStruct((n, 2 * D_MODEL), BF16),
        compiler_params=pltpu.CompilerParams(
            dimension_semantics=("arbitrary",), vmem_limit_bytes=_vmem_limit(est)),
        name="kv",
    )(mem2d, norm.reshape(1, D_MODEL), w_kv)


def _xattn_kernel(x_ref, g_ref, wq_ref, kv_ref, wo_ref, o_ref, h_ref, q_ref, s_ref, p_ref,
                  att_ref):
    n_sub = XA_ROWS // XA_SUB
    scale = 1.0 / math.sqrt(XA_HEAD_DIM)
    gain = g_ref[...]
    heads = [slice(hh * XA_HEAD_DIM, (hh + 1) * XA_HEAD_DIM) for hh in range(XA_HEADS)]

    def rows(i):
        return slice(i * XA_SUB, (i + 1) * XA_SUB)

    def norm(i):
        h_ref[i % 2] = _rmsnorm(x_ref[0, rows(i), :], gain).astype(BF16)

    def q_proj(i):
        for hd in heads:
            q_ref[i % 2, :, hd] = (_dot(h_ref[i % 2], wq_ref[:, hd]) * scale).astype(BF16)

    def scores(i):
        for hd in heads:
            s_ref[i % 2, :, hd] = _dot_nt(q_ref[i % 2, :, hd], kv_ref[0, :, hd])

    def softmax(i):
        for hd in heads:
            s = s_ref[i % 2, :, hd]
            e = jnp.exp(s - jnp.max(s, axis=-1, keepdims=True))
            p_ref[i % 2, :, hd] = (e * (1.0 / jnp.sum(e, axis=-1, keepdims=True))).astype(BF16)

    def pv(i):
        for hh, hd in enumerate(heads):
            v = kv_ref[0, :, D_MODEL + hh * XA_HEAD_DIM:D_MODEL + (hh + 1) * XA_HEAD_DIM]
            att_ref[i % 2, :, hd] = _dot(p_ref[i % 2, :, hd], v).astype(BF16)

    def out_proj(i):
        o_ref[0, rows(i), :] = x_ref[0, rows(i), :] + _dot(att_ref[i % 2], wo_ref[...])

    norm(0)
    if n_sub > 1:
        norm(1)
    q_proj(0)
    scores(0)
    for i in range(n_sub):
        if i + 1 < n_sub:
            q_proj(i + 1)
        softmax(i)
        pv(i)
        if i + 1 < n_sub:
            scores(i + 1)
        if i + 2 < n_sub:
            norm(i + 2)
        out_proj(i)


def _xattn(x, norm, w_q, kv, w_o):
    bsz, t, _ = x.shape
    tm = XA_ROWS
    est = (4 * tm * D_MODEL * 4 + w_q.size * 2 + w_o.size * 2 + 2 * N_MEM * 2 * D_MODEL * 2
           + 2 * XA_SUB * D_MODEL * (4 + 4 * 2))
    return pl.pallas_call(
        _xattn_kernel,
        grid=(bsz, t // tm),
        in_specs=[
            pl.BlockSpec((1, tm, D_MODEL), lambda b, i: (b, i, 0)),
            _resident((1, D_MODEL)),
            _resident((D_MODEL, D_MODEL)),
            pl.BlockSpec((1, N_MEM, 2 * D_MODEL), lambda b, i: (b, 0, 0)),
            _resident((D_MODEL, D_MODEL)),
        ],
        out_specs=pl.BlockSpec((1, tm, D_MODEL), lambda b, i: (b, i, 0)),
        out_shape=jax.ShapeDtypeStruct(x.shape, F32),
        scratch_shapes=[
            pltpu.VMEM((2, XA_SUB, D_MODEL), BF16),
            pltpu.VMEM((2, XA_SUB, D_MODEL), BF16),
            pltpu.VMEM((2, XA_SUB, D_MODEL), F32),
            pltpu.VMEM((2, XA_SUB, D_MODEL), BF16),
            pltpu.VMEM((2, XA_SUB, D_MODEL), BF16),
        ],
        compiler_params=pltpu.CompilerParams(
            dimension_semantics=("arbitrary", "arbitrary"), vmem_limit_bytes=_vmem_limit(est)),
        name="xattn",
    )(x, norm.reshape(1, D_MODEL), w_q, kv, w_o)


def kernel(x, mem, ffn1_norm, ffn1_w_gu, ffn1_w_down, mix_norm, w_in, lb_logits, hg_norm, conv_w,
           conv_norm, w_out, xa_norm, mem_norm, xa_wq, xa_wkv, xa_wo, ffn2_norm, ffn2_w_gu,
           ffn2_w_down, final_norm):
    bsz, t, d = x.shape
    assert d == D_MODEL and t % MIX_ROWS == 0 and (bsz * t) % FFN_ROWS == 0
    assert ffn1_norm.shape[0] == 1, "single-layer block"
    later = [w_in[0], w_out[0], xa_wq[0], xa_wkv[0], xa_wo[0], ffn2_w_gu[0], ffn2_w_down[0]]
    x1, later_bf = _ffn(x.reshape(bsz * t, d), ffn1_norm[0], ffn1_w_gu[0].astype(BF16),
                        ffn1_w_down[0].astype(BF16), final_norm, final_norm=False, cast=later)
    w_in_bf, w_out_bf, wq_bf, wkv_bf, wo_bf, w_gu2_bf, w_down2_bf = later_bf
    x2 = _mix(x1.reshape(bsz, t, d), mix_norm[0], w_in_bf, lb_logits, hg_norm[0], conv_w[0],
              conv_norm[0], w_out_bf)
    kv = _kv(mem.reshape(bsz * N_MEM, d), mem_norm[0], wkv_bf).reshape(bsz, N_MEM, 2 * d)
    x3 = _xattn(x2, xa_norm[0], wq_bf, kv, wo_bf)
    y, _ = _ffn(x3.reshape(bsz * t, d), ffn2_norm[0], w_gu2_bf, w_down2_bf, final_norm,
                final_norm=True)
    return y.reshape(bsz, t, d)
```

```python
import functools
import math

import jax
import jax.numpy as jnp
from jax import lax
from jax.experimental import pallas as pl
from jax.experimental.pallas import tpu as pltpu

D_MODEL = 1024
CHUNK = 64
SUB = 8
DIAG = 4
HG_WIDTH = 512
CONV_WIDTH = 512
HG_HEADS = 4
HG_DK = 128
CONV_K = 3
N_MEM = 256
XA_HEADS = 4
XA_HEAD_DIM = 256
D_FF = 2816
IN_COLS = 4 * HG_WIDTH + 3 * CONV_WIDTH
EPS = 1e-6

V7X_VMEM_BYTES = 64 * 1024 * 1024
MIB = 1024 * 1024

FFN_ROWS = 1024
FFN_SUB = 1024
FFN_COLS = 256
MIX_ROWS = 2048
MACRO = 128
PIECE = 256
GROUP_HEADS = 4
XA_ROWS = 2048
XA_SUB = 512
KV_ROWS = 256

BF16 = jnp.bfloat16
F32 = jnp.float32
BF16_ROWS = 16


VMEM_TEMP_SHARE = 4
VMEM_TEMP_FLOOR = 8 * MIB
V7X_VMEM_KEPT_FREE = 6 * MIB


def _vmem_limit(estimate_bytes):
    request = estimate_bytes + estimate_bytes // VMEM_TEMP_SHARE + VMEM_TEMP_FLOOR
    return int(min(request, V7X_VMEM_BYTES - V7X_VMEM_KEPT_FREE))


def _resident(shape):
    nd = len(shape)
    return pl.BlockSpec(shape, lambda *_: (0,) * nd, pipeline_mode=pl.Buffered(1))


def _rmsnorm(x, g):
    ms = jnp.mean(x * x, axis=-1, keepdims=True)
    return x * lax.rsqrt(ms + EPS) * g


def _silu(x):
    return x * (1.0 / (1.0 + jnp.exp(-x)))


def _dot(a, b):
    return jnp.dot(a, b, preferred_element_type=F32)


def _dot_nt(a, b):
    return lax.dot_general(a, b, (((1,), (1,)), ((), ())), preferred_element_type=F32)


def _dot_tn(a, b):
    return lax.dot_general(a, b, (((0,), (0,)), ((), ())), preferred_element_type=F32)


def _ffn_kernel(x_ref, g_ref, wgu_ref, wd_ref, fin_ref, *refs, final_norm, n_cast):
    cast_in, o_ref, cast_out = refs[:n_cast], refs[n_cast], refs[n_cast + 1:2 * n_cast + 1]
    h_ref, act_ref = refs[2 * n_cast + 1:]
    n_sub = FFN_ROWS // FFN_SUB
    gain = g_ref[...]

    def rows(i):
        return slice(i * FFN_SUB, (i + 1) * FFN_SUB)

    def norm(i):
        h_ref[rows(i), :] = _rmsnorm(x_ref[rows(i), :], gain).astype(BF16)

    def gate_up(i):
        h = h_ref[rows(i), :]
        for c in range(D_FF // FFN_COLS):
            lo = c * FFN_COLS
            gate = _dot(h, wgu_ref[:, lo:lo + FFN_COLS])
            up = _dot(h, wgu_ref[:, D_FF + lo:D_FF + lo + FFN_COLS])
            act_ref[rows(i), lo:lo + FFN_COLS] = (_silu(gate) * up).astype(BF16)

    def down(i):
        y = x_ref[rows(i), :] + 0.5 * _dot(act_ref[rows(i), :], wd_ref[...])
        if final_norm:
            y = _rmsnorm(y, fin_ref[...])
        o_ref[rows(i), :] = y

    norm(0)
    for i in range(n_sub):
        if i + 1 < n_sub:
            norm(i + 1)
        gate_up(i)
        if i >= 1:
            down(i - 1)
    down(n_sub - 1)

    for src, dst in zip(cast_in, cast_out):
        dst[...] = src[...].astype(BF16)


def _ffn(x2d, norm, w_gu, w_down, fin, *, final_norm, cast=()):
    n = x2d.shape[0]
    tm = FFN_ROWS
    steps = n // tm
    slices = [w.shape[0] // steps for w in cast]
    assert all(s % BF16_ROWS == 0 and s * steps == w.shape[0] for s, w in zip(slices, cast))
    est = (4 * tm * D_MODEL * 4 + w_gu.size * 2 + w_down.size * 2 + tm * D_FF * 2
           + tm * D_MODEL * 2 + 4 * FFN_SUB * FFN_COLS * 4
           + 2 * sum(s * w.shape[1] for s, w in zip(slices, cast)) * (4 + 2))
    cast_specs = [pl.BlockSpec((s, w.shape[1]), lambda i: (i, 0)) for s, w in zip(slices, cast)]
    out = pl.pallas_call(
        functools.partial(_ffn_kernel, final_norm=final_norm, n_cast=len(cast)),
        grid=(steps,),
        in_specs=[
            pl.BlockSpec((tm, D_MODEL), lambda i: (i, 0)),
            _resident((1, D_MODEL)),
            _resident((D_MODEL, 2 * D_FF)),
            _resident((D_FF, D_MODEL)),
            _resident((1, D_MODEL)),
        ] + cast_specs,
        out_specs=[pl.BlockSpec((tm, D_MODEL), lambda i: (i, 0))] + cast_specs,
        out_shape=[jax.ShapeDtypeStruct((n, D_MODEL), F32)]
        + [jax.ShapeDtypeStruct(w.shape, BF16) for w in cast],
        scratch_shapes=[pltpu.VMEM((tm, D_MODEL), BF16), pltpu.VMEM((tm, D_FF), BF16)],
        compiler_params=pltpu.CompilerParams(
            dimension_semantics=("arbitrary",), vmem_limit_bytes=_vmem_limit(est)),
        name="ffn_final" if final_norm else "ffn",
    )(x2d, norm.reshape(1, D_MODEL), w_gu, w_down, fin.reshape(1, D_MODEL), *cast)
    return out[0], list(out[1:])


def _chunk_cumsum(g):
    n = g.shape[-1]
    nb = CHUNK // SUB
    row = lax.broadcasted_iota(jnp.int32, (nb, SUB, n), 1)
    s = g.reshape(nb, SUB, n)
    for shift in (1, 2, 4):
        s = s + jnp.where(row >= shift, pltpu.roll(s, shift, 1), 0.0)
    carry = [jnp.zeros((1, 1, n), F32)]
    for i in range(1, nb):
        carry.append(carry[-1] + s[i - 1:i, SUB - 1:SUB, :])
    return (s + jnp.concatenate(carry, axis=0)).reshape(CHUNK, n)


def _hgrn2_masks():
    t = lax.broadcasted_iota(jnp.int32, (CHUNK, CHUNK), 0)
    s = lax.broadcasted_iota(jnp.int32, (CHUNK, CHUNK), 1)
    levels = []
    blk = CHUNK
    while blk > DIAG:
        half = blk // 2
        levels.append((t // blk == s // blk) & (t % blk >= half) & (s % blk < half))
        blk = half
    diag = [(s == t - d) & (t % DIAG >= d) for d in range(DIAG)]
    return levels, diag


def _mix_kernel(x_ref, xn_ref, nrm_ref, win_ref, lbl_ref, hgn_ref, cw_ref, cn_ref, wout_ref, o_ref,
                p_a, p_b, mix_a, mix_b, h_a, h_b, st_ref, ybuf_ref):
    n_macro = MIX_ROWS // MACRO

    @pl.when(pl.program_id(1) == 0)
    def _():
        st_ref[...] = jnp.zeros_like(st_ref)
        ybuf_ref[0:SUB, :] = jnp.zeros((SUB, CONV_WIDTH), F32)

    mix_b[...] = jnp.zeros((MACRO, D_MODEL), BF16)

    lbl = lbl_ref[...]
    lmax = jnp.max(lbl, axis=0, keepdims=True)
    lexp = jnp.exp(lbl - lmax)
    lb_all = lexp[0:1, :] / jnp.sum(lexp, axis=0, keepdims=True)

    level_masks, diag_masks = _hgrn2_masks()
    hgn = hgn_ref[...]
    nrm = nrm_ref[...]
    cw = cw_ref[...]
    cn = cn_ref[...]
    cb = 4 * HG_WIDTH

    def macro_rows(m):
        return pl.ds(pl.multiple_of(m * MACRO, MACRO), MACRO)

    def in_norm(m, h_ref):
        h_ref[...] = _rmsnorm(x_ref[0, macro_rows(m), :], nrm).astype(BF16)

    def in_norm_or_next(m, h_ref):
        rows = x_ref[0, macro_rows(jnp.minimum(m, n_macro - 1)), :]
        rows = jnp.where(m >= n_macro, xn_ref[0], rows)
        h_ref[...] = _rmsnorm(rows, nrm).astype(BF16)

    def in_proj_piece(h_ref, p_ref, c):
        cols = slice(c * PIECE, (c + 1) * PIECE)
        p_ref[:, cols] = _dot(h_ref[...], win_ref[:, cols])

    def out_proj_piece(m, mix_ref, c):
        rows = macro_rows(m)
        cols = slice(c * PIECE, (c + 1) * PIECE)
        o_ref[0, rows, cols] = x_ref[0, rows, cols] + _dot(mix_ref[...], wout_ref[:, cols])


    def hg_ab(p_ref, rows, hh):
        c0 = hh * HG_DK
        q = p_ref[rows, pl.ds(c0, HG_DK)]
        fz = p_ref[rows, pl.ds(HG_WIDTH + c0, HG_DK)]
        v = p_ref[rows, pl.ds(2 * HG_WIDTH + c0, HG_DK)]
        lb = lb_all[:, c0:c0 + HG_DK]
        f = lb + (1.0 - lb) * (1.0 / (1.0 + jnp.exp(-fz)))
        k = 1.0 - f
        qs = _silu(q)
        b = _chunk_cumsum(jnp.log2(f))
        b_end = b[CHUNK - 1:CHUNK, :]
        u = dict(hh=hh, rows=rows, p_ref=p_ref, v_bf=v.astype(BF16))
        q_bf, k_bf = qs.astype(BF16), k.astype(BF16)
        ops = []
        blk = CHUNK
        for _ in level_masks:
            half = blk // 2
            nb = CHUNK // blk
            ref = b.reshape(nb, blk, HG_DK)[:, half - 1:half, :]
            ref = jnp.broadcast_to(ref, (nb, blk, HG_DK)).reshape(CHUNK, HG_DK)
            z = jnp.exp2(-jnp.abs(b - ref)).astype(BF16)
            ops.append((q_bf * z, k_bf * z))
            blk = half
        u["level_ops"] = ops
        attn = jnp.where(diag_masks[0], jnp.sum(qs * k, axis=-1, keepdims=True), 0.0)
        f3 = f.reshape(CHUNK // SUB, SUB, HG_DK)
        decay = f
        for d in range(1, DIAG):
            f_d = pltpu.roll(f3, d, 1).reshape(CHUNK, HG_DK)
            r_d = jnp.sum(qs * (1.0 - f_d) * decay, axis=-1, keepdims=True)
            attn = jnp.where(diag_masks[d], r_d, attn)
            decay = decay * f_d
        u["attn_diag"] = attn
        u["qe"] = (qs * jnp.exp2(b)).astype(BF16)
        u["ke"] = (k * jnp.exp2(b_end - b)).astype(BF16)
        u["decay"] = jnp.exp2(b_end)
        return u

    def hg_mb(u):
        u["level_out"] = [_dot_nt(ql, kl) for ql, kl in u.pop("level_ops")]

    def hg_c(u):
        attn = u.pop("attn_diag")
        for lm, a in zip(level_masks, u.pop("level_out")):
            attn = jnp.where(lm, a, attn)
        u["attn"] = attn.astype(BF16)

    def hg_mc(u):
        hh, v_bf = u["hh"], u.pop("v_bf")
        st = st_ref[hh]
        lhs = jnp.concatenate([u.pop("qe"), u.pop("attn")], axis=1)
        u["o"] = _dot(lhs, jnp.concatenate([st.astype(BF16), v_bf], axis=0))
        decay_col = jnp.transpose(jnp.broadcast_to(u.pop("decay"), (HG_DK, HG_DK)))
        st_ref[hh] = st * decay_col + _dot_tn(u.pop("ke"), v_bf)

    def hg_d(u, mix_ref):
        c0 = u["hh"] * HG_DK
        gt = u["p_ref"][u["rows"], pl.ds(3 * HG_WIDTH + c0, HG_DK)]
        on = _rmsnorm(u.pop("o"), hgn) * _silu(gt)
        mix_ref[u["rows"], pl.ds(c0, HG_DK)] = on.astype(BF16)

    def short_conv(p_ref, mix_ref, rows):
        y = p_ref[rows, cb + CONV_WIDTH:cb + 2 * CONV_WIDTH] * \
            p_ref[rows, cb + 2 * CONV_WIDTH:cb + 3 * CONV_WIDTH]
        ybuf_ref[SUB:SUB + CHUNK, :] = y
        y1 = ybuf_ref[SUB - 1:SUB - 1 + CHUNK, :]
        y2 = ybuf_ref[SUB - 2:SUB - 2 + CHUNK, :]
        conv = cw[0:1, :] * y2 + cw[1:2, :] * y1 + cw[2:3, :] * y
        ocv = _rmsnorm(p_ref[rows, cb:cb + CONV_WIDTH] * conv, cn)
        mix_ref[rows, HG_WIDTH:HG_WIDTH + CONV_WIDTH] = ocv.astype(BF16)
        ybuf_ref[0:SUB, :] = ybuf_ref[CHUNK:CHUNK + SUB, :]

    def each(phase, units, *args):
        for u in units:
            phase(u, *args)

    n_chunks = MACRO // CHUNK
    chunk_rows = [pl.ds(j * CHUNK, CHUNK) for j in range(n_chunks)]
    head_groups = [tuple(range(h, h + GROUP_HEADS)) for h in range(0, HG_HEADS, GROUP_HEADS)]
    last = n_macro - 1

    @pl.when((pl.program_id(0) == 0) & (pl.program_id(1) == 0))
    def _():
        in_norm(0, h_b)
        for c in range(IN_COLS // PIECE):
            in_proj_piece(h_b, p_a, c)

    in_norm(1, h_a)

    def trip(j, carry):
        m = 2 * j
        big_a = [functools.partial(in_proj_piece, h_a, p_b, c) for c in range(IN_COLS // PIECE)]
        big_a += [functools.partial(out_proj_piece, jnp.maximum(m - 1, 0), mix_b, c)
                  for c in range(D_MODEL // PIECE)]
        big_b = [functools.partial(in_proj_piece, h_b, p_a, c) for c in range(IN_COLS // PIECE)]
        big_b += [functools.partial(out_proj_piece, m, mix_a, c) for c in range(D_MODEL // PIECE)]
        groups = [(p_ref, mix_ref, r, hs, hs[0] == 0)
                  for p_ref, mix_ref in ((p_a, mix_a), (p_b, mix_b)) for r in chunk_rows
                  for hs in head_groups]
        ng = len(groups)
        na = ng // 2
        plan = {i: [] for i in range(-1, ng + 1)}
        for pieces, slots in ((big_a, range(-1, na + 1)), (big_b, range(na + 1, ng + 1))):
            for k, piece in enumerate(pieces):
                plan[slots[k * len(slots) // len(pieces)]].append(piece)

        def issue(slot):
            for piece in plan[slot]:
                piece()

        issue(-1)
        units = []
        for i in range(ng + 1):
            if i < ng:
                p_ref, _, rows, hs, _ = groups[i]
                units.append([hg_ab(p_ref, rows, hh) for hh in hs])
                each(hg_mb, units[i])
            if i >= 1:
                p_ref, mix_ref, rows, _, with_conv = groups[i - 1]
                if with_conv:
                    short_conv(p_ref, mix_ref, rows)
                each(hg_c, units[i - 1])
                each(hg_mc, units[i - 1])
            if i >= 2:
                each(hg_d, units[i - 2], groups[i - 2][1])
            if i == na - 1:
                in_norm_or_next(m + 2, h_b)
            if i == ng - 1:
                in_norm(jnp.minimum(m + 3, last), h_a)
            issue(i)
        each(hg_d, units[ng - 1], groups[ng - 1][1])
        return carry

    lax.fori_loop(0, n_macro // 2, trip, 0, unroll=2)
    for c in range(D_MODEL // PIECE):
        out_proj_piece(last, mix_b, c)


def _mix(x, norm, w_in, lb_logits, hg_norm, conv_w, conv_norm, w_out):
    bsz, t, _ = x.shape
    tt = MIX_ROWS
    tiles = t // tt

    def next_group(b, i):
        step = jnp.minimum(b * tiles + i + 1, bsz * tiles - 1)
        return step // tiles, (step % tiles) * (tt // MACRO), 0

    est = (4 * tt * D_MODEL * 4 + 2 * MACRO * D_MODEL * 4 + w_in.size * 2 + w_out.size * 2 + 2 * MACRO * IN_COLS * 4
           + HG_HEADS * HG_DK * HG_DK * 4 + (CHUNK + SUB) * CONV_WIDTH * 4
           + 4 * MACRO * D_MODEL * 2)
    return pl.pallas_call(
        _mix_kernel,
        grid=(bsz, t // tt),
        in_specs=[
            pl.BlockSpec((1, tt, D_MODEL), lambda b, i: (b, i, 0)),
            pl.BlockSpec((1, MACRO, D_MODEL), next_group),
            _resident((1, D_MODEL)),
            _resident((D_MODEL, IN_COLS)),
            _resident(lb_logits.shape),
            _resident((1, HG_DK)),
            _resident((CONV_K, CONV_WIDTH)),
            _resident((1, CONV_WIDTH)),
            _resident((D_MODEL, D_MODEL)),
        ],
        out_specs=pl.BlockSpec((1, tt, D_MODEL), lambda b, i: (b, i, 0)),
        out_shape=jax.ShapeDtypeStruct(x.shape, F32),
        scratch_shapes=[
            pltpu.VMEM((MACRO, IN_COLS), F32),
            pltpu.VMEM((MACRO, IN_COLS), F32),
            pltpu.VMEM((MACRO, D_MODEL), BF16),
            pltpu.VMEM((MACRO, D_MODEL), BF16),
            pltpu.VMEM((MACRO, D_MODEL), BF16),
            pltpu.VMEM((MACRO, D_MODEL), BF16),
            pltpu.VMEM((HG_HEADS, HG_DK, HG_DK), F32),
            pltpu.VMEM((CHUNK + SUB, CONV_WIDTH), F32),
        ],
        compiler_params=pltpu.CompilerParams(
            dimension_semantics=("arbitrary", "arbitrary"), vmem_limit_bytes=_vmem_limit(est)),
        name="mix",
    )(x, x, norm.reshape(1, D_MODEL), w_in, lb_logits, hg_norm.reshape(1, HG_DK), conv_w,
      conv_norm.reshape(1, CONV_WIDTH), w_out)


def _kv_kernel(m_ref, g_ref, w_ref, o_ref):
    h = _rmsnorm(m_ref[...], g_ref[...]).astype(BF16)
    o_ref[...] = _dot(h, w_ref[...]).astype(BF16)


def _kv(mem2d, norm, w_kv):
    n = mem2d.shape[0]
    est = 2 * KV_ROWS * D_MODEL * 4 + w_kv.size * 2 + 2 * KV_ROWS * 2 * D_MODEL * 2
    return pl.pallas_call(
        _kv_kernel,
        grid=(n // KV_ROWS,),
        in_specs=[
            pl.BlockSpec((KV_ROWS, D_MODEL), lambda i: (i, 0)),
            _resident((1, D_MODEL)),
            _resident((D_MODEL, 2 * D_MODEL)),
        ],
        out_specs=pl.BlockSpec((KV_ROWS, 2 * D_MODEL), lambda i: (i, 0)),
        out_shape=jax.ShapeDtypeStruct((n, 2 * D_MODEL), BF16),
        compiler_params=pltpu.CompilerParams(
            dimension_semantics=("arbitrary",), vmem_limit_bytes=_vmem_limit(est)),
        name="kv",
    )(mem2d, norm.reshape(1, D_MODEL), w_kv)


def _xattn_kernel(x_ref, g_ref, wq_ref, kv_ref, wo_ref, o_ref, h_ref, q_ref, s_ref, p_ref,
                  att_ref):
    n_sub = XA_ROWS // XA_SUB
    scale = 1.0 / math.sqrt(XA_HEAD_DIM)
    gain = g_ref[...]
    heads = [slice(hh * XA_HEAD_DIM, (hh + 1) * XA_HEAD_DIM) for hh in range(XA_HEADS)]

    def rows(i):
        return slice(i * XA_SUB, (i + 1) * XA_SUB)

    def norm(i):
        h_ref[i % 2] = _rmsnorm(x_ref[0, rows(i), :], gain).astype(BF16)

    def q_proj(i):
        for hd in heads:
            q_ref[i % 2, :, hd] = (_dot(h_ref[i % 2], wq_ref[:, hd]) * scale).astype(BF16)

    def scores(i):
        for hd in heads:
            s_ref[i % 2, :, hd] = _dot_nt(q_ref[i % 2, :, hd], kv_ref[0, :, hd])

    def softmax(i):
        for hd in heads:
            s = s_ref[i % 2, :, hd]
            e = jnp.exp(s - jnp.max(s, axis=-1, keepdims=True))
            p_ref[i % 2, :, hd] = (e * (1.0 / jnp.sum(e, axis=-1, keepdims=True))).astype(BF16)

    def pv(i):
        for hh, hd in enumerate(heads):
            v = kv_ref[0, :, D_MODEL + hh * XA_HEAD_DIM:D_MODEL + (hh + 1) * XA_HEAD_DIM]
            att_ref[i % 2, :, hd] = _dot(p_ref[i % 2, :, hd], v).astype(BF16)

    def out_proj(i):
        o_ref[0, rows(i), :] = x_ref[0, rows(i), :] + _dot(att_ref[i % 2], wo_ref[...])

    norm(0)
    if n_sub > 1:
        norm(1)
    q_proj(0)
    scores(0)
    for i in range(n_sub):
        if i + 1 < n_sub:
            q_proj(i + 1)
        softmax(i)
        pv(i)
        if i + 1 < n_sub:
            scores(i + 1)
        if i + 2 < n_sub:
            norm(i + 2)
        out_proj(i)


def _xattn(x, norm, w_q, kv, w_o):
    bsz, t, _ = x.shape
    tm = XA_ROWS
    est = (4 * tm * D_MODEL * 4 + w_q.size * 2 + w_o.size * 2 + 2 * N_MEM * 2 * D_MODEL * 2
           + 2 * XA_SUB * D_MODEL * (4 + 4 * 2))
    return pl.pallas_call(
        _xattn_kernel,
        grid=(bsz, t // tm),
        in_specs=[
            pl.BlockSpec((1, tm, D_MODEL), lambda b, i: (b, i, 0)),
            _resident((1, D_MODEL)),
            _resident((D_MODEL, D_MODEL)),
            pl.BlockSpec((1, N_MEM, 2 * D_MODEL), lambda b, i: (b, 0, 0)),
            _resident((D_MODEL, D_MODEL)),
        ],
        out_specs=pl.BlockSpec((1, tm, D_MODEL), lambda b, i: (b, i, 0)),
        out_shape=jax.ShapeDtypeStruct(x.shape, F32),
        scratch_shapes=[
            pltpu.VMEM((2, XA_SUB, D_MODEL), BF16),
            pltpu.VMEM((2, XA_SUB, D_MODEL), BF16),
            pltpu.VMEM((2, XA_SUB, D_MODEL), F32),
            pltpu.VMEM((2, XA_SUB, D_MODEL), BF16),
            pltpu.VMEM((2, XA_SUB, D_MODEL), BF16),
        ],
        compiler_params=pltpu.CompilerParams(
            dimension_semantics=("arbitrary", "arbitrary"), vmem_limit_bytes=_vmem_limit(est)),
        name="xattn",
    )(x, norm.reshape(1, D_MODEL), w_q, kv, w_o)


def kernel(x, mem, ffn1_norm, ffn1_w_gu, ffn1_w_down, mix_norm, w_in, lb_logits, hg_norm, conv_w,
           conv_norm, w_out, xa_norm, mem_norm, xa_wq, xa_wkv, xa_wo, ffn2_norm, ffn2_w_gu,
           ffn2_w_down, final_norm):
    bsz, t, d = x.shape
    assert d == D_MODEL and t % MIX_ROWS == 0 and (bsz * t) % FFN_ROWS == 0
    assert ffn1_norm.shape[0] == 1, "single-layer block"
    later = [w_in[0], w_out[0], xa_wq[0], xa_wkv[0], xa_wo[0], ffn2_w_gu[0], ffn2_w_down[0]]
    x1, later_bf = _ffn(x.reshape(bsz * t, d), ffn1_norm[0], ffn1_w_gu[0].astype(BF16),
                        ffn1_w_down[0].astype(BF16), final_norm, final_norm=False, cast=later)
    w_in_bf, w_out_bf, wq_bf, wkv_bf, wo_bf, w_gu2_bf, w_down2_bf = later_bf
    x2 = _mix(x1.reshape(bsz, t, d), mix_norm[0], w_in_bf, lb_logits, hg_norm[0], conv_w[0],
              conv_norm[0], w_out_bf)
    kv = _kv(mem.reshape(bsz * N_MEM, d), mem_norm[0], wkv_bf).reshape(bsz, N_MEM, 2 * d)
    x3 = _xattn(x2, xa_norm[0], wq_bf, kv, wo_bf)
    y, _ = _ffn(x3.reshape(bsz * t, d), ffn2_norm[0], w_gu2_bf, w_down2_bf, final_norm,
                final_norm=True)
    return y.reshape(bsz, t, d)
```

```python
import functools
import math

import jax
import jax.numpy as jnp
from jax import lax
from jax.experimental import pallas as pl
from jax.experimental.pallas import tpu as pltpu

D_MODEL = 1024
CHUNK = 64
SUB = 8
DIAG = 4
HG_WIDTH = 512
CONV_WIDTH = 512
HG_HEADS = 4
HG_DK = 128
CONV_K = 3
N_MEM = 256
XA_HEADS = 4
XA_HEAD_DIM = 256
D_FF = 2816
IN_COLS = 4 * HG_WIDTH + 3 * CONV_WIDTH
EPS = 1e-6

V7X_VMEM_BYTES = 64 * 1024 * 1024
MIB = 1024 * 1024

FFN_ROWS = 1024
FFN_SUB = 1024
FFN_COLS = 256
MIX_ROWS = 1024
MACRO = 128
PIECE = 256
GROUP_HEADS = 4
XA_ROWS = 2048
XA_SUB = 512
KV_ROWS = 256

BF16 = jnp.bfloat16
F32 = jnp.float32
BF16_ROWS = 16


VMEM_TEMP_SHARE = 4
VMEM_TEMP_FLOOR = 8 * MIB
V7X_VMEM_KEPT_FREE = 6 * MIB


def _vmem_limit(estimate_bytes):
    request = estimate_bytes + estimate_bytes // VMEM_TEMP_SHARE + VMEM_TEMP_FLOOR
    return int(min(request, V7X_VMEM_BYTES - V7X_VMEM_KEPT_FREE))


def _resident(shape):
    nd = len(shape)
    return pl.BlockSpec(shape, lambda *_: (0,) * nd, pipeline_mode=pl.Buffered(1))


def _rmsnorm(x, g):
    ms = jnp.mean(x * x, axis=-1, keepdims=True)
    return x * lax.rsqrt(ms + EPS) * g


def _silu(x):
    return x * (1.0 / (1.0 + jnp.exp(-x)))


def _dot(a, b):
    return jnp.dot(a, b, preferred_element_type=F32)


def _dot_nt(a, b):
    return lax.dot_general(a, b, (((1,), (1,)), ((), ())), preferred_element_type=F32)


def _dot_tn(a, b):
    return lax.dot_general(a, b, (((0,), (0,)), ((), ())), preferred_element_type=F32)


def _ffn_kernel(x_ref, g_ref, wgu_ref, wd_ref, fin_ref, *refs, final_norm, n_cast):
    cast_in, o_ref, cast_out = refs[:n_cast], refs[n_cast], refs[n_cast + 1:2 * n_cast + 1]
    h_ref, act_ref = refs[2 * n_cast + 1:]
    n_sub = FFN_ROWS // FFN_SUB
    gain = g_ref[...]

    def rows(i):
        return slice(i * FFN_SUB, (i + 1) * FFN_SUB)

    def norm(i):
        h_ref[rows(i), :] = _rmsnorm(x_ref[rows(i), :], gain).astype(BF16)

    def gate_up(i):
        h = h_ref[rows(i), :]
        for c in range(D_FF // FFN_COLS):
            lo = c * FFN_COLS
            gate = _dot(h, wgu_ref[:, lo:lo + FFN_COLS])
            up = _dot(h, wgu_ref[:, D_FF + lo:D_FF + lo + FFN_COLS])
            act_ref[rows(i), lo:lo + FFN_COLS] = (_silu(gate) * up).astype(BF16)

    def down(i):
        y = x_ref[rows(i), :] + 0.5 * _dot(act_ref[rows(i), :], wd_ref[...])
        if final_norm:
            y = _rmsnorm(y, fin_ref[...])
        o_ref[rows(i), :] = y

    norm(0)
    for i in range(n_sub):
        if i + 1 < n_sub:
            norm(i + 1)
        gate_up(i)
        if i >= 1:
            down(i - 1)
    down(n_sub - 1)

    for src, dst in zip(cast_in, cast_out):
        dst[...] = src[...].astype(BF16)


def _ffn(x2d, norm, w_gu, w_down, fin, *, final_norm, cast=()):
    n = x2d.shape[0]
    tm = FFN_ROWS
    steps = n // tm
    slices = [w.shape[0] // steps for w in cast]
    assert all(s % BF16_ROWS == 0 and s * steps == w.shape[0] for s, w in zip(slices, cast))
    est = (4 * tm * D_MODEL * 4 + w_gu.size * 2 + w_down.size * 2 + tm * D_FF * 2
           + tm * D_MODEL * 2 + 4 * FFN_SUB * FFN_COLS * 4
           + 2 * sum(s * w.shape[1] for s, w in zip(slices, cast)) * (4 + 2))
    cast_specs = [pl.BlockSpec((s, w.shape[1]), lambda i: (i, 0)) for s, w in zip(slices, cast)]
    out = pl.pallas_call(
        functools.partial(_ffn_kernel, final_norm=final_norm, n_cast=len(cast)),
        grid=(steps,),
        in_specs=[
            pl.BlockSpec((tm, D_MODEL), lambda i: (i, 0)),
            _resident((1, D_MODEL)),
            _resident((D_MODEL, 2 * D_FF)),
            _resident((D_FF, D_MODEL)),
            _resident((1, D_MODEL)),
        ] + cast_specs,
        out_specs=[pl.BlockSpec((tm, D_MODEL), lambda i: (i, 0))] + cast_specs,
        out_shape=[jax.ShapeDtypeStruct((n, D_MODEL), F32)]
        + [jax.ShapeDtypeStruct(w.shape, BF16) for w in cast],
        scratch_shapes=[pltpu.VMEM((tm, D_MODEL), BF16), pltpu.VMEM((tm, D_FF), BF16)],
        compiler_params=pltpu.CompilerParams(
            dimension_semantics=("arbitrary",), vmem_limit_bytes=_vmem_limit(est)),
        name="ffn_final" if final_norm else "ffn",
    )(x2d, norm.reshape(1, D_MODEL), w_gu, w_down, fin.reshape(1, D_MODEL), *cast)
    return out[0], list(out[1:])


def _chunk_cumsum(g):
    n = g.shape[-1]
    nb = CHUNK // SUB
    row = lax.broadcasted_iota(jnp.int32, (nb, SUB, n), 1)
    s = g.reshape(nb, SUB, n)
    for shift in (1, 2, 4):
        s = s + jnp.where(row >= shift, pltpu.roll(s, shift, 1), 0.0)
    carry = [jnp.zeros((1, 1, n), F32)]
    for i in range(1, nb):
        carry.append(carry[-1] + s[i - 1:i, SUB - 1:SUB, :])
    return (s + jnp.concatenate(carry, axis=0)).reshape(CHUNK, n)


def _hgrn2_masks():
    t = lax.broadcasted_iota(jnp.int32, (CHUNK, CHUNK), 0)
    s = lax.broadcasted_iota(jnp.int32, (CHUNK, CHUNK), 1)
    levels = []
    blk = CHUNK
    while blk > DIAG:
        half = blk // 2
        levels.append((t // blk == s // blk) & (t % blk >= half) & (s % blk < half))
        blk = half
    diag = [(s == t - d) & (t % DIAG >= d) for d in range(DIAG)]
    return levels, diag


def _mix_kernel(x_ref, xn_ref, nrm_ref, win_ref, lbl_ref, hgn_ref, cw_ref, cn_ref, wout_ref, o_ref,
                p_a, p_b, mix_a, mix_b, h_a, h_b, st_ref, ybuf_ref):
    n_macro = MIX_ROWS // MACRO

    @pl.when(pl.program_id(1) == 0)
    def _():
        st_ref[...] = jnp.zeros_like(st_ref)
        ybuf_ref[0:SUB, :] = jnp.zeros((SUB, CONV_WIDTH), F32)

    mix_b[...] = jnp.zeros((MACRO, D_MODEL), BF16)

    lbl = lbl_ref[...]
    lmax = jnp.max(lbl, axis=0, keepdims=True)
    lexp = jnp.exp(lbl - lmax)
    lb_all = lexp[0:1, :] / jnp.sum(lexp, axis=0, keepdims=True)

    level_masks, diag_masks = _hgrn2_masks()
    hgn = hgn_ref[...]
    nrm = nrm_ref[...]
    cw = cw_ref[...]
    cn = cn_ref[...]
    cb = 4 * HG_WIDTH

    def macro_rows(m):
        return pl.ds(pl.multiple_of(m * MACRO, MACRO), MACRO)

    def in_norm(m, h_ref):
        h_ref[...] = _rmsnorm(x_ref[0, macro_rows(m), :], nrm).astype(BF16)

    def in_norm_or_next(m, h_ref):
        rows = x_ref[0, macro_rows(jnp.minimum(m, n_macro - 1)), :]
        rows = jnp.where(m >= n_macro, xn_ref[0], rows)
        h_ref[...] = _rmsnorm(rows, nrm).astype(BF16)

    def in_proj_piece(h_ref, p_ref, c):
        cols = slice(c * PIECE, (c + 1) * PIECE)
        p_ref[:, cols] = _dot(h_ref[...], win_ref[:, cols])

    def out_proj_piece(m, mix_ref, c):
        rows = macro_rows(m)
        cols = slice(c * PIECE, (c + 1) * PIECE)
        o_ref[0, rows, cols] = x_ref[0, rows, cols] + _dot(mix_ref[...], wout_ref[:, cols])


    def hg_ab(p_ref, rows, hh):
        c0 = hh * HG_DK
        q = p_ref[rows, pl.ds(c0, HG_DK)]
        fz = p_ref[rows, pl.ds(HG_WIDTH + c0, HG_DK)]
        v = p_ref[rows, pl.ds(2 * HG_WIDTH + c0, HG_DK)]
        lb = lb_all[:, c0:c0 + HG_DK]
        f = lb + (1.0 - lb) * (1.0 / (1.0 + jnp.exp(-fz)))
        k = 1.0 - f
        qs = _silu(q)
        b = _chunk_cumsum(jnp.log2(f))
        b_end = b[CHUNK - 1:CHUNK, :]
        u = dict(hh=hh, rows=rows, p_ref=p_ref, v_bf=v.astype(BF16))
        q_bf, k_bf = qs.astype(BF16), k.astype(BF16)
        ops = []
        blk = CHUNK
        for _ in level_masks:
            half = blk // 2
            nb = CHUNK // blk
            ref = b.reshape(nb, blk, HG_DK)[:, half - 1:half, :]
            ref = jnp.broadcast_to(ref, (nb, blk, HG_DK)).reshape(CHUNK, HG_DK)
            z = jnp.exp2(-jnp.abs(b - ref)).astype(BF16)
            ops.append((q_bf * z, k_bf * z))
            blk = half
        u["level_ops"] = ops
        attn = jnp.where(diag_masks[0], jnp.sum(qs * k, axis=-1, keepdims=True), 0.0)
        f3 = f.reshape(CHUNK // SUB, SUB, HG_DK)
        decay = f
        for d in range(1, DIAG):
            f_d = pltpu.roll(f3, d, 1).reshape(CHUNK, HG_DK)
            r_d = jnp.sum(qs * (1.0 - f_d) * decay, axis=-1, keepdims=True)
            attn = jnp.where(diag_masks[d], r_d, attn)
            decay = decay * f_d
        u["attn_diag"] = attn
        u["qe"] = (qs * jnp.exp2(b)).astype(BF16)
        u["ke"] = (k * jnp.exp2(b_end - b)).astype(BF16)
        u["decay"] = jnp.exp2(b_end)
        return u

    def hg_mb(u):
        u["level_out"] = [_dot_nt(ql, kl) for ql, kl in u.pop("level_ops")]

    def hg_c(u):
        attn = u.pop("attn_diag")
        for lm, a in zip(level_masks, u.pop("level_out")):
            attn = jnp.where(lm, a, attn)
        u["attn"] = attn.astype(BF16)

    def hg_mc(u):
        hh, v_bf = u["hh"], u.pop("v_bf")
        st = st_ref[hh]
        o = _dot(u.pop("attn"), v_bf)
        u["o"] = o + _dot(u.pop("qe"), st.astype(BF16))
        decay_col = jnp.transpose(jnp.broadcast_to(u.pop("decay"), (HG_DK, HG_DK)))
        st_ref[hh] = st * decay_col + _dot_tn(u.pop("ke"), v_bf)

    def hg_d(u, mix_ref):
        c0 = u["hh"] * HG_DK
        gt = u["p_ref"][u["rows"], pl.ds(3 * HG_WIDTH + c0, HG_DK)]
        on = _rmsnorm(u.pop("o"), hgn) * _silu(gt)
        mix_ref[u["rows"], pl.ds(c0, HG_DK)] = on.astype(BF16)

    def short_conv(p_ref, mix_ref, rows):
        y = p_ref[rows, cb + CONV_WIDTH:cb + 2 * CONV_WIDTH] * \
            p_ref[rows, cb + 2 * CONV_WIDTH:cb + 3 * CONV_WIDTH]
        ybuf_ref[SUB:SUB + CHUNK, :] = y
        y1 = ybuf_ref[SUB - 1:SUB - 1 + CHUNK, :]
        y2 = ybuf_ref[SUB - 2:SUB - 2 + CHUNK, :]
        conv = cw[0:1, :] * y2 + cw[1:2, :] * y1 + cw[2:3, :] * y
        ocv = _rmsnorm(p_ref[rows, cb:cb + CONV_WIDTH] * conv, cn)
        mix_ref[rows, HG_WIDTH:HG_WIDTH + CONV_WIDTH] = ocv.astype(BF16)
        ybuf_ref[0:SUB, :] = ybuf_ref[CHUNK:CHUNK + SUB, :]

    def each(phase, units, *args):
        for u in units:
            phase(u, *args)

    n_chunks = MACRO // CHUNK
    chunk_rows = [pl.ds(j * CHUNK, CHUNK) for j in range(n_chunks)]
    head_groups = [tuple(range(h, h + GROUP_HEADS)) for h in range(0, HG_HEADS, GROUP_HEADS)]
    last = n_macro - 1

    @pl.when((pl.program_id(0) == 0) & (pl.program_id(1) == 0))
    def _():
        in_norm(0, h_b)
        for c in range(IN_COLS // PIECE):
            in_proj_piece(h_b, p_a, c)

    in_norm(1, h_a)

    def trip(j, carry):
        m = 2 * j
        big_a = [functools.partial(in_proj_piece, h_a, p_b, c) for c in range(IN_COLS // PIECE)]
        big_a += [functools.partial(out_proj_piece, jnp.maximum(m - 1, 0), mix_b, c)
                  for c in range(D_MODEL // PIECE)]
        big_b = [functools.partial(in_proj_piece, h_b, p_a, c) for c in range(IN_COLS // PIECE)]
        big_b += [functools.partial(out_proj_piece, m, mix_a, c) for c in range(D_MODEL // PIECE)]
        groups = [(p_ref, mix_ref, r, hs, hs[0] == 0)
                  for p_ref, mix_ref in ((p_a, mix_a), (p_b, mix_b)) for r in chunk_rows
                  for hs in head_groups]
        ng = len(groups)
        na = ng // 2
        plan = {i: [] for i in range(-1, ng + 1)}
        for pieces, slots in ((big_a, range(-1, na + 1)), (big_b, range(na + 1, ng + 1))):
            for k, piece in enumerate(pieces):
                plan[slots[k * len(slots) // len(pieces)]].append(piece)

        def issue(slot):
            for piece in plan[slot]:
                piece()

        issue(-1)
        units = []
        for i in range(ng + 1):
            if i < ng:
                p_ref, _, rows, hs, _ = groups[i]
                units.append([hg_ab(p_ref, rows, hh) for hh in hs])
                each(hg_mb, units[i])
            if i >= 1:
                p_ref, mix_ref, rows, _, with_conv = groups[i - 1]
                if with_conv:
                    short_conv(p_ref, mix_ref, rows)
                each(hg_c, units[i - 1])
                each(hg_mc, units[i - 1])
            if i >= 2:
                each(hg_d, units[i - 2], groups[i - 2][1])
            if i == na - 1:
                in_norm_or_next(m + 2, h_b)
            if i == ng - 1:
                in_norm(jnp.minimum(m + 3, last), h_a)
            issue(i)
        each(hg_d, units[ng - 1], groups[ng - 1][1])
        return carry

    lax.fori_loop(0, n_macro // 2, trip, 0, unroll=2)
    for c in range(D_MODEL // PIECE):
        out_proj_piece(last, mix_b, c)


def _mix(x, norm, w_in, lb_logits, hg_norm, conv_w, conv_norm, w_out):
    bsz, t, _ = x.shape
    tt = MIX_ROWS
    tiles = t // tt

    def next_group(b, i):
        step = jnp.minimum(b * tiles + i + 1, bsz * tiles - 1)
        return step // tiles, (step % tiles) * (tt // MACRO), 0

    est = (4 * tt * D_MODEL * 4 + 2 * MACRO * D_MODEL * 4 + w_in.size * 2 + w_out.size * 2 + 2 * MACRO * IN_COLS * 4
           + HG_HEADS * HG_DK * HG_DK * 4 + (CHUNK + SUB) * CONV_WIDTH * 4
           + 4 * MACRO * D_MODEL * 2)
    return pl.pallas_call(
        _mix_kernel,
        grid=(bsz, t // tt),
        in_specs=[
            pl.BlockSpec((1, tt, D_MODEL), lambda b, i: (b, i, 0)),
            pl.BlockSpec((1, MACRO, D_MODEL), next_group),
            _resident((1, D_MODEL)),
            _resident((D_MODEL, IN_COLS)),
            _resident(lb_logits.shape),
            _resident((1, HG_DK)),
            _resident((CONV_K, CONV_WIDTH)),
            _resident((1, CONV_WIDTH)),
            _resident((D_MODEL, D_MODEL)),
        ],
        out_specs=pl.BlockSpec((1, tt, D_MODEL), lambda b, i: (b, i, 0)),
        out_shape=jax.ShapeDtypeStruct(x.shape, F32),
        scratch_shapes=[
            pltpu.VMEM((MACRO, IN_COLS), F32),
            pltpu.VMEM((MACRO, IN_COLS), F32),
            pltpu.VMEM((MACRO, D_MODEL), BF16),
            pltpu.VMEM((MACRO, D_MODEL), BF16),
            pltpu.VMEM((MACRO, D_MODEL), BF16),
            pltpu.VMEM((MACRO, D_MODEL), BF16),
            pltpu.VMEM((HG_HEADS, HG_DK, HG_DK), F32),
            pltpu.VMEM((CHUNK + SUB, CONV_WIDTH), F32),
        ],
        compiler_params=pltpu.CompilerParams(
            dimension_semantics=("arbitrary", "arbitrary"), vmem_limit_bytes=_vmem_limit(est)),
        name="mix",
    )(x, x, norm.reshape(1, D_MODEL), w_in, lb_logits, hg_norm.reshape(1, HG_DK), conv_w,
      conv_norm.reshape(1, CONV_WIDTH), w_out)


def _kv_kernel(m_ref, g_ref, w_ref, o_ref):
    h = _rmsnorm(m_ref[...], g_ref[...]).astype(BF16)
    o_ref[...] = _dot(h, w_ref[...]).astype(BF16)


def _kv(mem2d, norm, w_kv):
    n = mem2d.shape[0]
    est = 2 * KV_ROWS * D_MODEL * 4 + w_kv.size * 2 + 2 * KV_ROWS * 2 * D_MODEL * 2
    return pl.pallas_call(
        _kv_kernel,
        grid=(n // KV_ROWS,),
        in_specs=[
            pl.BlockSpec((KV_ROWS, D_MODEL), lambda i: (i, 0)),
            _resident((1, D_MODEL)),
            _resident((D_MODEL, 2 * D_MODEL)),
        ],
        out_specs=pl.BlockSpec((KV_ROWS, 2 * D_MODEL), lambda i: (i, 0)),
        out_shape=jax.ShapeDtypeStruct((n, 2 * D_MODEL), BF16),
        compiler_params=pltpu.CompilerParams(
            dimension_semantics=("arbitrary",), vmem_limit_bytes=_vmem_limit(est)),
        name="kv",
    )(mem2d, norm.reshape(1, D_MODEL), w_kv)


def _xattn_kernel(x_ref, g_ref, wq_ref, kv_ref, wo_ref, o_ref, h_ref, q_ref, s_ref, p_ref,
                  att_ref):
    n_sub = XA_ROWS // XA_SUB
    scale = 1.0 / math.sqrt(XA_HEAD_DIM)
    gain = g_ref[...]
    heads = [slice(hh * XA_HEAD_DIM, (hh + 1) * XA_HEAD_DIM) for hh in range(XA_HEADS)]

    def rows(i):
        return slice(i * XA_SUB, (i + 1) * XA_SUB)

    def norm(i):
        h_ref[i % 2] = _rmsnorm(x_ref[0, rows(i), :], gain).astype(BF16)

    def q_proj(i):
        for hd in heads:
            q_ref[i % 2, :, hd] = (_dot(h_ref[i % 2], wq_ref[:, hd]) * scale).astype(BF16)

    def scores(i):
        for hd in heads:
            s_ref[i % 2, :, hd] = _dot_nt(q_ref[i % 2, :, hd], kv_ref[0, :, hd])

    def softmax(i):
        for hd in heads:
            s = s_ref[i % 2, :, hd]
            e = jnp.exp(s - jnp.max(s, axis=-1, keepdims=True))
            p_ref[i % 2, :, hd] = (e * (1.0 / jnp.sum(e, axis=-1, keepdims=True))).astype(BF16)

    def pv(i):
        for hh, hd in enumerate(heads):
            v = kv_ref[0, :, D_MODEL + hh * XA_HEAD_DIM:D_MODEL + (hh + 1) * XA_HEAD_DIM]
            att_ref[i % 2, :, hd] = _dot(p_ref[i % 2, :, hd], v).astype(BF16)

    def out_proj(i):
        o_ref[0, rows(i), :] = x_ref[0, rows(i), :] + _dot(att_ref[i % 2], wo_ref[...])

    norm(0)
    if n_sub > 1:
        norm(1)
    q_proj(0)
    scores(0)
    for i in range(n_sub):
        if i + 1 < n_sub:
            q_proj(i + 1)
        softmax(i)
        pv(i)
        if i + 1 < n_sub:
            scores(i + 1)
        if i + 2 < n_sub:
            norm(i + 2)
        out_proj(i)


def _xattn(x, norm, w_q, kv, w_o):
    bsz, t, _ = x.shape
    tm = XA_ROWS
    est = (4 * tm * D_MODEL * 4 + w_q.size * 2 + w_o.size * 2 + 2 * N_MEM * 2 * D_MODEL * 2
           + 2 * XA_SUB * D_MODEL * (4 + 4 * 2))
    return pl.pallas_call(
        _xattn_kernel,
        grid=(bsz, t // tm),
        in_specs=[
            pl.BlockSpec((1, tm, D_MODEL), lambda b, i: (b, i, 0)),
            _resident((1, D_MODEL)),
            _resident((D_MODEL, D_MODEL)),
            pl.BlockSpec((1, N_MEM, 2 * D_MODEL), lambda b, i: (b, 0, 0)),
            _resident((D_MODEL, D_MODEL)),
        ],
        out_specs=pl.BlockSpec((1, tm, D_MODEL), lambda b, i: (b, i, 0)),
        out_shape=jax.ShapeDtypeStruct(x.shape, F32),
        scratch_shapes=[
            pltpu.VMEM((2, XA_SUB, D_MODEL), BF16),
            pltpu.VMEM((2, XA_SUB, D_MODEL), BF16),
            pltpu.VMEM((2, XA_SUB, D_MODEL), F32),
            pltpu.VMEM((2, XA_SUB, D_MODEL), BF16),
            pltpu.VMEM((2, XA_SUB, D_MODEL), BF16),
        ],
        compiler_params=pltpu.CompilerParams(
            dimension_semantics=("arbitrary", "arbitrary"), vmem_limit_bytes=_vmem_limit(est)),
        name="xattn",
    )(x, norm.reshape(1, D_MODEL), w_q, kv, w_o)


def kernel(x, mem, ffn1_norm, ffn1_w_gu, ffn1_w_down, mix_norm, w_in, lb_logits, hg_norm, conv_w,
           conv_norm, w_out, xa_norm, mem_norm, xa_wq, xa_wkv, xa_wo, ffn2_norm, ffn2_w_gu,
           ffn2_w_down, final_norm):
    bsz, t, d = x.shape
    assert d == D_MODEL and t % MIX_ROWS == 0 and (bsz * t) % FFN_ROWS == 0
    assert ffn1_norm.shape[0] == 1, "single-layer block"
    later = [w_in[0], w_out[0], xa_wq[0], xa_wkv[0], xa_wo[0], ffn2_w_gu[0], ffn2_w_down[0]]
    x1, later_bf = _ffn(x.reshape(bsz * t, d), ffn1_norm[0], ffn1_w_gu[0].astype(BF16),
                        ffn1_w_down[0].astype(BF16), final_norm, final_norm=False, cast=later)
    w_in_bf, w_out_bf, wq_bf, wkv_bf, wo_bf, w_gu2_bf, w_down2_bf = later_bf
    x2 = _mix(x1.reshape(bsz, t, d), mix_norm[0], w_in_bf, lb_logits, hg_norm[0], conv_w[0],
              conv_norm[0], w_out_bf)
    kv = _kv(mem.reshape(bsz * N_MEM, d), mem_norm[0], wkv_bf).reshape(bsz, N_MEM, 2 * d)
    x3 = _xattn(x2, xa_norm[0], wq_bf, kv, wo_bf)
    y, _ = _ffn(x3.reshape(bsz * t, d), ffn2_norm[0], w_gu2_bf, w_down2_bf, final_norm,
                final_norm=True)
    return y.reshape(bsz, t, d)
```

```python
import functools
import math

import jax
import jax.numpy as jnp
from jax import lax
from jax.experimental import pallas as pl
from jax.experimental.pallas import tpu as pltpu

D_MODEL = 1024
CHUNK = 64
SUB = 8
DIAG = 4
HG_WIDTH = 512
CONV_WIDTH = 512
HG_HEADS = 4
HG_DK = 128
CONV_K = 3
N_MEM = 256
XA_HEADS = 4
XA_HEAD_DIM = 256
D_FF = 2816
IN_COLS = 4 * HG_WIDTH + 3 * CONV_WIDTH
EPS = 1e-6

V7X_VMEM_BYTES = 64 * 1024 * 1024
MIB = 1024 * 1024

FFN_ROWS = 1024
FFN_SUB = 1024
FFN_COLS = 256
MIX_ROWS = 2048
MACRO = 128
PIECE = 256
GROUP_HEADS = 4
XA_ROWS = 2048
XA_SUB = 512
KV_ROWS = 256

BF16 = jnp.bfloat16
F32 = jnp.float32
BF16_ROWS = 16


VMEM_TEMP_SHARE = 4
VMEM_TEMP_FLOOR = 8 * MIB
V7X_VMEM_KEPT_FREE = 6 * MIB


def _vmem_limit(estimate_bytes):
    request = estimate_bytes + estimate_bytes // VMEM_TEMP_SHARE + VMEM_TEMP_FLOOR
    return int(min(request, V7X_VMEM_BYTES - V7X_VMEM_KEPT_FREE))


def _resident(shape):
    nd = len(shape)
    return pl.BlockSpec(shape, lambda *_: (0,) * nd, pipeline_mode=pl.Buffered(1))


def _rmsnorm(x, g):
    ms = jnp.mean(x * x, axis=-1, keepdims=True)
    return x * lax.rsqrt(ms + EPS) * g


def _silu(x):
    return x * (1.0 / (1.0 + jnp.exp(-x)))


def _dot(a, b):
    return jnp.dot(a, b, preferred_element_type=F32)


def _dot_nt(a, b):
    return lax.dot_general(a, b, (((1,), (1,)), ((), ())), preferred_element_type=F32)


def _dot_tn(a, b):
    return lax.dot_general(a, b, (((0,), (0,)), ((), ())), preferred_element_type=F32)


def _ffn_kernel(x_ref, g_ref, wgu_ref, wd_ref, fin_ref, *refs, final_norm, n_cast):
    cast_in, o_ref, cast_out = refs[:n_cast], refs[n_cast], refs[n_cast + 1:2 * n_cast + 1]
    h_ref, act_ref = refs[2 * n_cast + 1:]
    n_sub = FFN_ROWS // FFN_SUB
    gain = g_ref[...]

    def rows(i):
        return slice(i * FFN_SUB, (i + 1) * FFN_SUB)

    def norm(i):
        h_ref[rows(i), :] = _rmsnorm(x_ref[rows(i), :], gain).astype(BF16)

    def gate_up(i):
        h = h_ref[rows(i), :]
        for c in range(D_FF // FFN_COLS):
            lo = c * FFN_COLS
            gate = _dot(h, wgu_ref[:, lo:lo + FFN_COLS])
            up = _dot(h, wgu_ref[:, D_FF + lo:D_FF + lo + FFN_COLS])
            act_ref[rows(i), lo:lo + FFN_COLS] = (_silu(gate) * up).astype(BF16)

    def down(i):
        y = x_ref[rows(i), :] + 0.5 * _dot(act_ref[rows(i), :], wd_ref[...])
        if final_norm:
            y = _rmsnorm(y, fin_ref[...])
        o_ref[rows(i), :] = y

    norm(0)
    for i in range(n_sub):
        if i + 1 < n_sub:
            norm(i + 1)
        gate_up(i)
        if i >= 1:
            down(i - 1)
    down(n_sub - 1)

    for src, dst in zip(cast_in, cast_out):
        dst[...] = src[...].astype(BF16)


def _ffn(x2d, norm, w_gu, w_down, fin, *, final_norm, cast=()):
    n = x2d.shape[0]
    tm = FFN_ROWS
    steps = n // tm
    slices = [w.shape[0] // steps for w in cast]
    assert all(s % BF16_ROWS == 0 and s * steps == w.shape[0] for s, w in zip(slices, cast))
    est = (4 * tm * D_MODEL * 4 + w_gu.size * 2 + w_down.size * 2 + tm * D_FF * 2
           + tm * D_MODEL * 2 + 4 * FFN_SUB * FFN_COLS * 4
           + 2 * sum(s * w.shape[1] for s, w in zip(slices, cast)) * (4 + 2))
    cast_specs = [pl.BlockSpec((s, w.shape[1]), lambda i: (i, 0)) for s, w in zip(slices, cast)]
    out = pl.pallas_call(
        functools.partial(_ffn_kernel, final_norm=final_norm, n_cast=len(cast)),
        grid=(steps,),
        in_specs=[
            pl.BlockSpec((tm, D_MODEL), lambda i: (i, 0)),
            _resident((1, D_MODEL)),
            _resident((D_MODEL, 2 * D_FF)),
            _resident((D_FF, D_MODEL)),
            _resident((1, D_MODEL)),
        ] + cast_specs,
        out_specs=[pl.BlockSpec((tm, D_MODEL), lambda i: (i, 0))] + cast_specs,
        out_shape=[jax.ShapeDtypeStruct((n, D_MODEL), F32)]
        + [jax.ShapeDtypeStruct(w.shape, BF16) for w in cast],
        scratch_shapes=[pltpu.VMEM((tm, D_MODEL), BF16), pltpu.VMEM((tm, D_FF), BF16)],
        compiler_params=pltpu.CompilerParams(
            dimension_semantics=("arbitrary",), vmem_limit_bytes=_vmem_limit(est)),
        name="ffn_final" if final_norm else "ffn",
    )(x2d, norm.reshape(1, D_MODEL), w_gu, w_down, fin.reshape(1, D_MODEL), *cast)
    return out[0], list(out[1:])


def _chunk_cumsum(g):
    n = g.shape[-1]
    nb = CHUNK // SUB
    row = lax.broadcasted_iota(jnp.int32, (nb, SUB, n), 1)
    s = g.reshape(nb, SUB, n)
    for shift in (1, 2, 4):
        s = s + jnp.where(row >= shift, pltpu.roll(s, shift, 1), 0.0)
    carry = [jnp.zeros((1, 1, n), F32)]
    for i in range(1, nb):
        carry.append(carry[-1] + s[i - 1:i, SUB - 1:SUB, :])
    return (s + jnp.concatenate(carry, axis=0)).reshape(CHUNK, n)


def _hgrn2_masks():
    t = lax.broadcasted_iota(jnp.int32, (CHUNK, CHUNK), 0)
    s = lax.broadcasted_iota(jnp.int32, (CHUNK, CHUNK), 1)
    levels = []
    blk = CHUNK
    while blk > DIAG:
        half = blk // 2
        levels.append((t // blk == s // blk) & (t % blk >= half) & (s % blk < half))
        blk = half
    diag = [(s == t - d) & (t % DIAG >= d) for d in range(DIAG)]
    return levels, diag


def _mix_kernel(x_ref, xn_ref, nrm_ref, win_ref, lbl_ref, hgn_ref, cw_ref, cn_ref, wout_ref, o_ref,
                p_a, p_b, mix_a, mix_b, h_a, h_b, st_ref, ybuf_ref):
    n_macro = MIX_ROWS // MACRO

    @pl.when(pl.program_id(1) == 0)
    def _():
        st_ref[...] = jnp.zeros_like(st_ref)
        ybuf_ref[0:SUB, :] = jnp.zeros((SUB, CONV_WIDTH), F32)

    mix_b[...] = jnp.zeros((MACRO, D_MODEL), BF16)

    lbl = lbl_ref[...]
    lmax = jnp.max(lbl, axis=0, keepdims=True)
    lexp = jnp.exp(lbl - lmax)
    lb_all = lexp[0:1, :] / jnp.sum(lexp, axis=0, keepdims=True)

    level_masks, diag_masks = _hgrn2_masks()
    hgn = hgn_ref[...]
    nrm = nrm_ref[...]
    cw = cw_ref[...]
    cn = cn_ref[...]
    cb = 4 * HG_WIDTH

    def macro_rows(m):
        return pl.ds(pl.multiple_of(m * MACRO, MACRO), MACRO)

    def in_norm(m, h_ref):
        h_ref[...] = _rmsnorm(x_ref[0, macro_rows(m), :], nrm).astype(BF16)

    def in_norm_or_next(m, h_ref):
        rows = x_ref[0, macro_rows(jnp.minimum(m, n_macro - 1)), :]
        rows = jnp.where(m >= n_macro, xn_ref[0], rows)
        h_ref[...] = _rmsnorm(rows, nrm).astype(BF16)

    def in_proj_piece(h_ref, p_ref, c):
        cols = slice(c * PIECE, (c + 1) * PIECE)
        p_ref[:, cols] = _dot(h_ref[...], win_ref[:, cols])

    def out_proj_piece(m, mix_ref, c):
        rows = macro_rows(m)
        cols = slice(c * PIECE, (c + 1) * PIECE)
        o_ref[0, rows, cols] = x_ref[0, rows, cols] + _dot(mix_ref[...], wout_ref[:, cols])


    def hg_ab(p_ref, rows, hh):
        c0 = hh * HG_DK
        q = p_ref[rows, pl.ds(c0, HG_DK)]
        fz = p_ref[rows, pl.ds(HG_WIDTH + c0, HG_DK)]
        v = p_ref[rows, pl.ds(2 * HG_WIDTH + c0, HG_DK)]
        lb = lb_all[:, c0:c0 + HG_DK]
        f = lb + (1.0 - lb) * (1.0 / (1.0 + jnp.exp(-fz)))
        k = 1.0 - f
        qs = _silu(q)
        b = _chunk_cumsum(jnp.log2(f))
        b_end = b[CHUNK - 1:CHUNK, :]
        u = dict(hh=hh, rows=rows, p_ref=p_ref, v_bf=v.astype(BF16))
        q_bf, k_bf = qs.astype(BF16), k.astype(BF16)
        ops = []
        blk = CHUNK
        for _ in level_masks:
            half = blk // 2
            nb = CHUNK // blk
            ref = b.reshape(nb, blk, HG_DK)[:, half - 1:half, :]
            ref = jnp.broadcast_to(ref, (nb, blk, HG_DK)).reshape(CHUNK, HG_DK)
            z = jnp.exp2(-jnp.abs(b - ref)).astype(BF16)
            ops.append((q_bf * z, k_bf * z))
            blk = half
        u["level_ops"] = ops
        attn = jnp.where(diag_masks[0], jnp.sum(qs * k, axis=-1, keepdims=True), 0.0)
        f3 = f.reshape(CHUNK // SUB, SUB, HG_DK)
        decay = f
        for d in range(1, DIAG):
            f_d = pltpu.roll(f3, d, 1).reshape(CHUNK, HG_DK)
            r_d = jnp.sum(qs * (1.0 - f_d) * decay, axis=-1, keepdims=True)
            attn = jnp.where(diag_masks[d], r_d, attn)
            decay = decay * f_d
        u["attn_diag"] = attn
        u["qe"] = (qs * jnp.exp2(b)).astype(BF16)
        u["ke"] = (k * jnp.exp2(b_end - b)).astype(BF16)
        u["decay"] = jnp.exp2(b_end)
        return u

    def hg_mb(u):
        u["level_out"] = [_dot_nt(ql, kl) for ql, kl in u.pop("level_ops")]

    def hg_c(u):
        attn = u.pop("attn_diag")
        for lm, a in zip(level_masks, u.pop("level_out")):
            attn = jnp.where(lm, a, attn)
        u["attn"] = attn.astype(BF16)

    def hg_mc(u):
        hh, v_bf = u["hh"], u.pop("v_bf")
        st = st_ref[hh]
        o = _dot(u.pop("attn"), v_bf)
        u["o"] = o + _dot(u.pop("qe"), st.astype(BF16))
        decay_col = jnp.transpose(jnp.broadcast_to(u.pop("decay"), (HG_DK, HG_DK)))
        st_ref[hh] = st * decay_col + _dot_tn(u.pop("ke"), v_bf)

    def hg_d(u, mix_ref):
        c0 = u["hh"] * HG_DK
        gt = u["p_ref"][u["rows"], pl.ds(3 * HG_WIDTH + c0, HG_DK)]
        on = _rmsnorm(u.pop("o"), hgn) * _silu(gt)
        mix_ref[u["rows"], pl.ds(c0, HG_DK)] = on.astype(BF16)

    def short_conv(p_ref, mix_ref, rows):
        y = p_ref[rows, cb + CONV_WIDTH:cb + 2 * CONV_WIDTH] * \
            p_ref[rows, cb + 2 * CONV_WIDTH:cb + 3 * CONV_WIDTH]
        ybuf_ref[SUB:SUB + CHUNK, :] = y
        y1 = ybuf_ref[SUB - 1:SUB - 1 + CHUNK, :]
        y2 = ybuf_ref[SUB - 2:SUB - 2 + CHUNK, :]
        conv = cw[0:1, :] * y2 + cw[1:2, :] * y1 + cw[2:3, :] * y
        ocv = _rmsnorm(p_ref[rows, cb:cb + CONV_WIDTH] * conv, cn)
        mix_ref[rows, HG_WIDTH:HG_WIDTH + CONV_WIDTH] = ocv.astype(BF16)
        ybuf_ref[0:SUB, :] = ybuf_ref[CHUNK:CHUNK + SUB, :]

    def each(phase, units, *args):
        for u in units:
            phase(u, *args)

    n_chunks = MACRO // CHUNK
    chunk_rows = [pl.ds(j * CHUNK, CHUNK) for j in range(n_chunks)]
    head_groups = [tuple(range(h, h + GROUP_HEADS)) for h in range(0, HG_HEADS, GROUP_HEADS)]
    last = n_macro - 1

    @pl.when((pl.program_id(0) == 0) & (pl.program_id(1) == 0))
    def _():
        in_norm(0, h_b)
        for c in range(IN_COLS // PIECE):
            in_proj_piece(h_b, p_a, c)

    in_norm(1, h_a)

    def trip(j, carry):
        m = 2 * j
        big_a = [functools.partial(in_proj_piece, h_a, p_b, c) for c in range(IN_COLS // PIECE)]
        big_a += [functools.partial(out_proj_piece, jnp.maximum(m - 1, 0), mix_b, c)
                  for c in range(D_MODEL // PIECE)]
        big_b = [functools.partial(in_proj_piece, h_b, p_a, c) for c in range(IN_COLS // PIECE)]
        big_b += [functools.partial(out_proj_piece, m, mix_a, c) for c in range(D_MODEL // PIECE)]
        groups = [(p_ref, mix_ref, r, hs, hs[0] == 0)
                  for p_ref, mix_ref in ((p_a, mix_a), (p_b, mix_b)) for r in chunk_rows
                  for hs in head_groups]
        ng = len(groups)
        na = ng // 2
        plan = {i: [] for i in range(-1, ng + 1)}
        n_in = IN_COLS // PIECE
        for k, piece in enumerate(big_a):
            slots = range(-1, na + 1)
            plan[slots[k * len(slots) // len(big_a)]].append(piece)
        for k, piece in enumerate(big_b):
            plan[na + 1 if k < n_in // 4 else ng].append(piece)

        def issue(slot):
            for piece in plan[slot]:
                piece()

        issue(-1)
        units = []
        for i in range(ng + 1):
            if i < ng:
                p_ref, _, rows, hs, _ = groups[i]
                units.append([hg_ab(p_ref, rows, hh) for hh in hs])
                each(hg_mb, units[i])
            if i >= 1:
                p_ref, mix_ref, rows, _, with_conv = groups[i - 1]
                if with_conv:
                    short_conv(p_ref, mix_ref, rows)
                each(hg_c, units[i - 1])
                each(hg_mc, units[i - 1])
            if i >= 2:
                each(hg_d, units[i - 2], groups[i - 2][1])
            if i == na - 1:
                in_norm_or_next(m + 2, h_b)
            if i == ng - 1:
                in_norm(jnp.minimum(m + 3, last), h_a)
            issue(i)
        each(hg_d, units[ng - 1], groups[ng - 1][1])
        return carry

    lax.fori_loop(0, n_macro // 2, trip, 0, unroll=2)
    for c in range(D_MODEL // PIECE):
        out_proj_piece(last, mix_b, c)


def _mix(x, norm, w_in, lb_logits, hg_norm, conv_w, conv_norm, w_out):
    bsz, t, _ = x.shape
    tt = MIX_ROWS
    tiles = t // tt

    def next_group(b, i):
        step = jnp.minimum(b * tiles + i + 1, bsz * tiles - 1)
        return step // tiles, (step % tiles) * (tt // MACRO), 0

    est = (4 * tt * D_MODEL * 4 + 2 * MACRO * D_MODEL * 4 + w_in.size * 2 + w_out.size * 2 + 2 * MACRO * IN_COLS * 4
           + HG_HEADS * HG_DK * HG_DK * 4 + (CHUNK + SUB) * CONV_WIDTH * 4
           + 4 * MACRO * D_MODEL * 2)
    return pl.pallas_call(
        _mix_kernel,
        grid=(bsz, t // tt),
        in_specs=[
            pl.BlockSpec((1, tt, D_MODEL), lambda b, i: (b, i, 0)),
            pl.BlockSpec((1, MACRO, D_MODEL), next_group),
            _resident((1, D_MODEL)),
            _resident((D_MODEL, IN_COLS)),
            _resident(lb_logits.shape),
            _resident((1, HG_DK)),
            _resident((CONV_K, CONV_WIDTH)),
            _resident((1, CONV_WIDTH)),
            _resident((D_MODEL, D_MODEL)),
        ],
        out_specs=pl.BlockSpec((1, tt, D_MODEL), lambda b, i: (b, i, 0)),
        out_shape=jax.ShapeDtypeStruct(x.shape, F32),
        scratch_shapes=[
            pltpu.VMEM((MACRO, IN_COLS), F32),
            pltpu.VMEM((MACRO, IN_COLS), F32),
            pltpu.VMEM((MACRO, D_MODEL), BF16),
            pltpu.VMEM((MACRO, D_MODEL), BF16),
            pltpu.VMEM((MACRO, D_MODEL), BF16),
            pltpu.VMEM((MACRO, D_MODEL), BF16),
            pltpu.VMEM((HG_HEADS, HG_DK, HG_DK), F32),
            pltpu.VMEM((CHUNK + SUB, CONV_WIDTH), F32),
        ],
        compiler_params=pltpu.CompilerParams(
            dimension_semantics=("arbitrary", "arbitrary"), vmem_limit_bytes=_vmem_limit(est)),
        name="mix",
    )(x, x, norm.reshape(1, D_MODEL), w_in, lb_logits, hg_norm.reshape(1, HG_DK), conv_w,
      conv_norm.reshape(1, CONV_WIDTH), w_out)


def _kv_kernel(m_ref, g_ref, w_ref, o_ref):
    h = _rmsnorm(m_ref[...], g_ref[...]).astype(BF16)
    o_ref[...] = _dot(h, w_ref[...]).astype(BF16)


def _kv(mem2d, norm, w_kv):
    n = mem2d.shape[0]
    est = 2 * KV_ROWS * D_MODEL * 4 + w_kv.size * 2 + 2 * KV_ROWS * 2 * D_MODEL * 2
    return pl.pallas_call(
        _kv_kernel,
        grid=(n // KV_ROWS,),
        in_specs=[
            pl.BlockSpec((KV_ROWS, D_MODEL), lambda i: (i, 0)),
            _resident((1, D_MODEL)),
            _resident((D_MODEL, 2 * D_MODEL)),
        ],
        out_specs=pl.BlockSpec((KV_ROWS, 2 * D_MODEL), lambda i: (i, 0)),
        out_shape=jax.ShapeDtypeStruct((n, 2 * D_MODEL), BF16),
        compiler_params=pltpu.CompilerParams(
            dimension_semantics=("arbitrary",), vmem_limit_bytes=_vmem_limit(est)),
        name="kv",
    )(mem2d, norm.reshape(1, D_MODEL), w_kv)


def _xattn_kernel(x_ref, g_ref, wq_ref, kv_ref, wo_ref, o_ref, h_ref, q_ref, s_ref, p_ref,
                  att_ref):
    n_sub = XA_ROWS // XA_SUB
    scale = 1.0 / math.sqrt(XA_HEAD_DIM)
    gain = g_ref[...]
    heads = [slice(hh * XA_HEAD_DIM, (hh + 1) * XA_HEAD_DIM) for hh in range(XA_HEADS)]

    def rows(i):
        return slice(i * XA_SUB, (i + 1) * XA_SUB)

    def norm(i):
        h_ref[i % 2] = _rmsnorm(x_ref[0, rows(i), :], gain).astype(BF16)

    def q_proj(i):
        for hd in heads:
            q_ref[i % 2, :, hd] = (_dot(h_ref[i % 2], wq_ref[:, hd]) * scale).astype(BF16)

    def scores(i):
        for hd in heads:
            s_ref[i % 2, :, hd] = _dot_nt(q_ref[i % 2, :, hd], kv_ref[0, :, hd])

    def softmax(i):
        for hd in heads:
            s = s_ref[i % 2, :, hd]
            e = jnp.exp(s - jnp.max(s, axis=-1, keepdims=True))
            p_ref[i % 2, :, hd] = (e * (1.0 / jnp.sum(e, axis=-1, keepdims=True))).astype(BF16)

    def pv(i):
        for hh, hd in enumerate(heads):
            v = kv_ref[0, :, D_MODEL + hh * XA_HEAD_DIM:D_MODEL + (hh + 1) * XA_HEAD_DIM]
            att_ref[i % 2, :, hd] = _dot(p_ref[i % 2, :, hd], v).astype(BF16)

    def out_proj(i):
        o_ref[0, rows(i), :] = x_ref[0, rows(i), :] + _dot(att_ref[i % 2], wo_ref[...])

    norm(0)
    if n_sub > 1:
        norm(1)
    q_proj(0)
    scores(0)
    for i in range(n_sub):
        if i + 1 < n_sub:
            q_proj(i + 1)
        softmax(i)
        pv(i)
        if i + 1 < n_sub:
            scores(i + 1)
        if i + 2 < n_sub:
            norm(i + 2)
        out_proj(i)


def _xattn(x, norm, w_q, kv, w_o):
    bsz, t, _ = x.shape
    tm = XA_ROWS
    est = (4 * tm * D_MODEL * 4 + w_q.size * 2 + w_o.size * 2 + 2 * N_MEM * 2 * D_MODEL * 2
           + 2 * XA_SUB * D_MODEL * (4 + 4 * 2))
    return pl.pallas_call(
        _xattn_kernel,
        grid=(bsz, t // tm),
        in_specs=[
            pl.BlockSpec((1, tm, D_MODEL), lambda b, i: (b, i, 0)),
            _resident((1, D_MODEL)),
            _resident((D_MODEL, D_MODEL)),
            pl.BlockSpec((1, N_MEM, 2 * D_MODEL), lambda b, i: (b, 0, 0)),
            _resident((D_MODEL, D_MODEL)),
        ],
        out_specs=pl.BlockSpec((1, tm, D_MODEL), lambda b, i: (b, i, 0)),
        out_shape=jax.ShapeDtypeStruct(x.shape, F32),
        scratch_shapes=[
            pltpu.VMEM((2, XA_SUB, D_MODEL), BF16),
            pltpu.VMEM((2, XA_SUB, D_MODEL), BF16),
            pltpu.VMEM((2, XA_SUB, D_MODEL), F32),
            pltpu.VMEM((2, XA_SUB, D_MODEL), BF16),
            pltpu.VMEM((2, XA_SUB, D_MODEL), BF16),
        ],
        compiler_params=pltpu.CompilerParams(
            dimension_semantics=("arbitrary", "arbitrary"), vmem_limit_bytes=_vmem_limit(est)),
        name="xattn",
    )(x, norm.reshape(1, D_MODEL), w_q, kv, w_o)


def kernel(x, mem, ffn1_norm, ffn1_w_gu, ffn1_w_down, mix_norm, w_in, lb_logits, hg_norm, conv_w,
           conv_norm, w_out, xa_norm, mem_norm, xa_wq, xa_wkv, xa_wo, ffn2_norm, ffn2_w_gu,
           ffn2_w_down, final_norm):
    bsz, t, d = x.shape
    assert d == D_MODEL and t % MIX_ROWS == 0 and (bsz * t) % FFN_ROWS == 0
    assert ffn1_norm.shape[0] == 1, "single-layer block"
    later = [w_in[0], w_out[0], xa_wq[0], xa_wkv[0], xa_wo[0], ffn2_w_gu[0], ffn2_w_down[0]]
    x1, later_bf = _ffn(x.reshape(bsz * t, d), ffn1_norm[0], ffn1_w_gu[0].astype(BF16),
                        ffn1_w_down[0].astype(BF16), final_norm, final_norm=False, cast=later)
    w_in_bf, w_out_bf, wq_bf, wkv_bf, wo_bf, w_gu2_bf, w_down2_bf = later_bf
    x2 = _mix(x1.reshape(bsz, t, d), mix_norm[0], w_in_bf, lb_logits, hg_norm[0], conv_w[0],
              conv_norm[0], w_out_bf)
    kv = _kv(mem.reshape(bsz * N_MEM, d), mem_norm[0], wkv_bf).reshape(bsz, N_MEM, 2 * d)
    x3 = _xattn(x2, xa_norm[0], wq_bf, kv, wo_bf)
    y, _ = _ffn(x3.reshape(bsz * t, d), ffn2_norm[0], w_gu2_bf, w_down2_bf, final_norm,
                final_norm=True)
    return y.reshape(bsz, t, d)
```
